```python
import math
import jax, jax.numpy as jnp
from jax import lax
import numpy as np

D_MODEL = 2048
BATCH = 4
SEQ = 2048
DEPTH = 2
DEC_BATCH = 128
DEC_SEQ = 8
PAST_LEN = 16384
PAGE_SIZE = 128

N_MIXERS = 2
N_SSM_LAYERS = (DEPTH + 1) // 2
N_CONV_LAYERS = DEPTH // 2
SSM_GROUP = 16
SSM_GROUPS = D_MODEL // SSM_GROUP
SSM_STATE = 64
DT_MIN = 1e-3
DT_MAX = 1e-1
CONV_WIDTH = 31
CONV_DIM = D_MODEL
D_FF = 4 * D_MODEL
RMS_EPS = 1e-6
LN_EPS = 1e-5

kernel_name = "s5_conformer_conv_hybrid_decode_step"


def _rmsnorm(x, g):
    x32 = x.astype(jnp.float32)
    y = x32 * lax.rsqrt(jnp.mean(x32 * x32, axis=-1, keepdims=True) + RMS_EPS)
    return (y * g.astype(jnp.float32)).astype(x.dtype)


def _layernorm(x, g, b):
    x32 = x.astype(jnp.float32)
    mu = jnp.mean(x32, axis=-1, keepdims=True)
    xc = x32 - mu
    y = xc * lax.rsqrt(jnp.mean(xc * xc, axis=-1, keepdims=True) + LN_EPS)
    return (y * g.astype(jnp.float32) + b.astype(jnp.float32)).astype(x.dtype)


def _complex_affine_combine(e1, e2):
    a1r, a1i, b1r, b1i = e1
    a2r, a2i, b2r, b2i = e2
    ar = a2r * a1r - a2i * a1i
    ai = a2r * a1i + a2i * a1r
    br = a2r * b1r - a2i * b1i + b2r
    bi = a2r * b1i + a2i * b1r + b2i
    return (ar, ai, br, bi)


def _s5_mixer(u, h0_re, h0_im, a_re, a_im, log_dt, b_re, b_im, c_re, c_im, d_skip, w_glu, b_glu):
    bsz, seq_len, _ = u.shape
    f32 = jnp.float32
    u32 = u.astype(f32)
    dt = jnp.exp(log_dt.astype(f32))[:, None]
    lr, li = a_re.astype(f32), a_im.astype(f32)
    mag = jnp.exp(lr * dt)
    abar_re = mag * jnp.cos(li * dt)
    abar_im = mag * jnp.sin(li * dt)
    er, ei = abar_re - 1.0, abar_im
    den = lr * lr + li * li
    q_re = (er * lr + ei * li) / den
    q_im = (ei * lr - er * li) / den
    br, bi = b_re.astype(f32), b_im.astype(f32)
    bbar_re = q_re[..., None] * br - q_im[..., None] * bi
    bbar_im = q_re[..., None] * bi + q_im[..., None] * br
    ug = u32.reshape(bsz, seq_len, SSM_GROUPS, SSM_GROUP)
    bu_re = jnp.einsum('blgc,gpc->blgp', ug, bbar_re)
    bu_im = jnp.einsum('blgc,gpc->blgp', ug, bbar_im)
    h0r, h0i = h0_re.astype(f32), h0_im.astype(f32)
    bu_re = bu_re.at[:, 0].add(abar_re * h0r - abar_im * h0i)
    bu_im = bu_im.at[:, 0].add(abar_re * h0i + abar_im * h0r)
    a_r = jnp.broadcast_to(abar_re, bu_re.shape)
    a_i = jnp.broadcast_to(abar_im, bu_re.shape)
    _, _, h_re, h_im = lax.associative_scan(_complex_affine_combine, (a_r, a_i, bu_re, bu_im), axis=1)
    y = (jnp.einsum('blgp,gcp->blgc', h_re, c_re.astype(f32))
         - jnp.einsum('blgp,gcp->blgc', h_im, c_im.astype(f32)))
    y = y.reshape(bsz, seq_len, D_MODEL) + d_skip.astype(f32) * u32
    g = jax.nn.gelu(y).astype(u.dtype)
    z = g @ w_glu + b_glu
    za, zb = jnp.split(z, 2, axis=-1)
    out = za * jax.nn.sigmoid(zb)
    return out, h_re[:, -1].astype(h0_re.dtype), h_im[:, -1].astype(h0_im.dtype)


def _conv_module(u, buf, w_pw1, b_pw1, w_dw, b_dw, ln_g, ln_b, w_pw2, b_pw2):
    z = u @ w_pw1 + b_pw1
    za, zb = jnp.split(z, 2, axis=-1)
    v = za * jax.nn.sigmoid(zb)
    padded = jnp.concatenate([buf.astype(v.dtype), v], axis=1)
    conv = lax.conv_general_dilated(
        padded, w_dw[:, None, :].astype(v.dtype), window_strides=(1,), padding='VALID',
        dimension_numbers=('NWC', 'WIO', 'NWC'), feature_group_count=CONV_DIM) + b_dw
    h = jax.nn.silu(_layernorm(conv, ln_g, ln_b))
    out = h @ w_pw2 + b_pw2
    new_buf = padded[:, -(CONV_WIDTH - 1):]
    return out, new_buf.astype(buf.dtype)


def _sqrelu_mlp(h, w1, w2):
    a = jax.nn.relu(h @ w1)
    return (a * a) @ w2


def _trunk(x, c, ssm_h_re, ssm_h_im, conv_buf, p):
    new_re, new_im, new_buf = [], [], []
    c_act = jax.nn.silu(c)
    for i in range(DEPTH):
        mod = c_act @ p['w_ada'][i] + p['b_ada'][i]
        sh1, sc1, g1, sh2, sc2, g2 = [m[:, None, :] for m in jnp.split(mod, 6, axis=-1)]
        h = _rmsnorm(x, p['rms_g_mix'][i]) * (1.0 + sc1) + sh1
        j = i // N_MIXERS
        if i % N_MIXERS == 0:
            out, hr, hi = _s5_mixer(h, ssm_h_re[j], ssm_h_im[j], p['ssm_a_re'][j], p['ssm_a_im'][j],
                                    p['ssm_log_dt'][j], p['ssm_b_re'][j], p['ssm_b_im'][j],
                                    p['ssm_c_re'][j], p['ssm_c_im'][j], p['ssm_d'][j],
                                    p['ssm_w_glu'][j], p['ssm_b_glu'][j])
            new_re.append(hr)
            new_im.append(hi)
        else:
            out, nb = _conv_module(h, conv_buf[j], p['conv_w_pw1'][j], p['conv_b_pw1'][j],
                                   p['conv_w_dw'][j], p['conv_b_dw'][j], p['conv_ln_g'][j],
                                   p['conv_ln_b'][j], p['conv_w_pw2'][j], p['conv_b_pw2'][j])
            new_buf.append(nb)
        x = x + g1 * out
        h = _rmsnorm(x, p['rms_g_mlp'][i]) * (1.0 + sc2) + sh2
        x = x + g2 * _sqrelu_mlp(h, p['mlp_w1'][i], p['mlp_w2'][i])
    y = _rmsnorm(x, p['final_g'])
    return y, jnp.stack(new_re), jnp.stack(new_im), jnp.stack(new_buf)


def setup_inputs(seed: int = 0) -> dict:
    key = jax.random.key(seed)
    ks = iter(jax.random.split(key, 40))
    f32 = jnp.float32
    nrm = lambda shape, s: jax.random.normal(next(ks), shape, f32) * s
    D, G, P, GC = D_MODEL, SSM_GROUPS, SSM_STATE, SSM_GROUP
    NS, NC = N_SSM_LAYERS, N_CONV_LAYERS
    a_re = -0.5 + nrm((NS, G, P), 0.01)
    a_im = jnp.pi * jnp.arange(P, dtype=f32)[None, None, :] + nrm((NS, G, P), 0.01)
    log_dt = jax.random.uniform(next(ks), (NS, G), f32, math.log(DT_MIN), math.log(DT_MAX))
    return {
        "x_prompt": nrm((BATCH, SEQ, D), 1.0),
        "x_sample": nrm((DEC_BATCH, DEC_SEQ, D), 1.0),
        "state_ssm_re": nrm((NS, DEC_BATCH, G, P), 0.1),
        "state_ssm_im": nrm((NS, DEC_BATCH, G, P), 0.1),
        "cache_conv": nrm((NC, DEC_BATCH, CONV_WIDTH - 1, CONV_DIM), 0.5),
        "c_prompt": nrm((BATCH, D), 1.0),
        "c_sample": nrm((DEC_BATCH, D), 1.0),
        "rms_g_mix": 1.0 + nrm((DEPTH, D), 0.02),
        "rms_g_mlp": 1.0 + nrm((DEPTH, D), 0.02),
        "w_ada": nrm((DEPTH, D, 6 * D), 0.2 * D ** -0.5),
        "b_ada": nrm((DEPTH, 6 * D), 0.02),
        "ssm_a_re": a_re,
        "ssm_a_im": a_im,
        "ssm_log_dt": log_dt,
        "ssm_b_re": nrm((NS, G, P, GC), (2 * GC) ** -0.5),
        "ssm_b_im": nrm((NS, G, P, GC), (2 * GC) ** -0.5),
        "ssm_c_re": nrm((NS, G, GC, P), (2 * P) ** -0.5 * 4.0),
        "ssm_c_im": nrm((NS, G, GC, P), (2 * P) ** -0.5 * 4.0),
        "ssm_d": nrm((NS, D), 1.0),
        "ssm_w_glu": nrm((NS, D, 2 * D), D ** -0.5),
        "ssm_b_glu": nrm((NS, 2 * D), 0.02),
        "conv_w_pw1": nrm((NC, D, 2 * CONV_DIM), D ** -0.5),
        "conv_b_pw1": nrm((NC, 2 * CONV_DIM), 0.02),
        "conv_w_dw": nrm((NC, CONV_WIDTH, CONV_DIM), CONV_WIDTH ** -0.5),
        "conv_b_dw": nrm((NC, CONV_DIM), 0.02),
        "conv_ln_g": 1.0 + nrm((NC, CONV_DIM), 0.02),
        "conv_ln_b": nrm((NC, CONV_DIM), 0.02),
        "conv_w_pw2": nrm((NC, CONV_DIM, D), CONV_DIM ** -0.5),
        "conv_b_pw2": nrm((NC, D), 0.02),
        "mlp_w1": nrm((DEPTH, D, D_FF), D ** -0.5),
        "mlp_w2": nrm((DEPTH, D_FF, D), D_FF ** -0.5),
        "final_g": 1.0 + nrm((D,), 0.02),
    }


def reference(x_prompt, x_sample, state_ssm_re, state_ssm_im, cache_conv, c_prompt, c_sample,
              rms_g_mix, rms_g_mlp, w_ada, b_ada, ssm_a_re, ssm_a_im, ssm_log_dt, ssm_b_re, ssm_b_im,
              ssm_c_re, ssm_c_im, ssm_d, ssm_w_glu, ssm_b_glu, conv_w_pw1, conv_b_pw1, conv_w_dw,
              conv_b_dw, conv_ln_g, conv_ln_b, conv_w_pw2, conv_b_pw2, mlp_w1, mlp_w2, final_g):
    params = dict(rms_g_mix=rms_g_mix, rms_g_mlp=rms_g_mlp, w_ada=w_ada, b_ada=b_ada,
                  ssm_a_re=ssm_a_re, ssm_a_im=ssm_a_im, ssm_log_dt=ssm_log_dt,
                  ssm_b_re=ssm_b_re, ssm_b_im=ssm_b_im, ssm_c_re=ssm_c_re, ssm_c_im=ssm_c_im,
                  ssm_d=ssm_d, ssm_w_glu=ssm_w_glu, ssm_b_glu=ssm_b_glu,
                  conv_w_pw1=conv_w_pw1, conv_b_pw1=conv_b_pw1, conv_w_dw=conv_w_dw,
                  conv_b_dw=conv_b_dw, conv_ln_g=conv_ln_g, conv_ln_b=conv_ln_b,
                  conv_w_pw2=conv_w_pw2, conv_b_pw2=conv_b_pw2,
                  mlp_w1=mlp_w1, mlp_w2=mlp_w2, final_g=final_g)
    bp = x_prompt.shape[0]
    zero_re = jnp.zeros((N_SSM_LAYERS, bp, SSM_GROUPS, SSM_STATE), state_ssm_re.dtype)
    zero_im = jnp.zeros((N_SSM_LAYERS, bp, SSM_GROUPS, SSM_STATE), state_ssm_im.dtype)
    zero_buf = jnp.zeros((N_CONV_LAYERS, bp, CONV_WIDTH - 1, CONV_DIM), cache_conv.dtype)
    y_prompt, p_re, p_im, p_buf = _trunk(x_prompt, c_prompt, zero_re, zero_im, zero_buf, params)
    y_sample, s_re, s_im, s_buf = _trunk(x_sample, c_sample, state_ssm_re, state_ssm_im, cache_conv, params)
    return (y_prompt, y_sample, p_re, p_im, p_buf, s_re, s_im, s_buf)
```

```python
import collections
import functools

import jax
import jax.numpy as jnp
from jax import lax
from jax.experimental import pallas as pl
from jax.experimental.pallas import tpu as pltpu

F32 = jnp.float32
BF16 = jnp.bfloat16

D_MODEL = 2048
D_FF = 4 * D_MODEL
SSM_GROUP = 16
SSM_GROUPS = D_MODEL // SSM_GROUP
SSM_STATE = 64
LOG2_GROUP = SSM_GROUP.bit_length() - 1
LOG2_STATE = SSM_STATE.bit_length() - 1
assert SSM_GROUP == 1 << LOG2_GROUP and SSM_STATE == 1 << LOG2_STATE
CONV_WIDTH = 31
CONV_HIST = CONV_WIDTH - 1
RMS_EPS = 1e-6
LN_EPS = 1e-5

LANES = 128
SUBLANES = 8
VMEM_LIMIT_BYTES = 56 * 1024 * 1024

GROUPS_PER_TILE = LANES // SSM_GROUP
STATE_TILE = GROUPS_PER_TILE * SSM_STATE
N_LANE_TILES = D_MODEL // LANES
HALO = 32
HIST_OFF = HALO - CONV_HIST

Trunk = collections.namedtuple("Trunk", "n_seq seq_len tm")


def _params(*sem):
    return pltpu.CompilerParams(dimension_semantics=sem, vmem_limit_bytes=VMEM_LIMIT_BYTES)


def _dot(a, b):
    return jnp.dot(a, b, preferred_element_type=F32)


def _norm_mod(x3, g, sc, sh):
    ms = jnp.mean(x3 * x3, axis=-1, keepdims=True)
    return (x3 * lax.rsqrt(ms + RMS_EPS) * g) * (1.0 + sc) + sh


def _mod_spec(tr, layer, part, tn, ti, tj):
    nblk = D_MODEL // tn
    if tr.seq_len >= tr.tm:
        per = tr.seq_len // tr.tm
        return pl.BlockSpec((None, 1, 1, tn), lambda *g: (layer, ti(*g) // per, 0, part * nblk + tj(*g)))
    assert tr.seq_len == SUBLANES
    return pl.BlockSpec((None, tr.tm // SUBLANES, 1, tn), lambda *g: (layer, ti(*g), 0, part * nblk + tj(*g)))


def _ada_kernel(c_ref, w_ref, b_ref, o_ref):
    ca = jax.nn.silu(c_ref[...]).astype(BF16)
    o_ref[...] = _dot(ca, w_ref[...].astype(BF16)) + b_ref[...]


def _ada(c_all, w_ada, b_ada):
    depth, d, n = w_ada.shape
    rows = c_all.shape[0]
    tn = 1024
    return pl.pallas_call(
        _ada_kernel,
        grid=(depth, n // tn),
        in_specs=[pl.BlockSpec((rows, d), lambda l, j: (0, 0)),
                  pl.BlockSpec((None, d, tn), lambda l, j: (l, 0, j)),
                  pl.BlockSpec((None, 1, tn), lambda l, j: (l, 0, j))],
        out_specs=pl.BlockSpec((None, rows, tn), lambda l, j: (l, 0, j)),
        out_shape=jax.ShapeDtypeStruct((depth, rows, n), F32),
        compiler_params=_params("parallel", "parallel"),
        name="ada",
    )(c_all, w_ada, b_ada.reshape(depth, 1, n))


def _prenorm_kernel(x_ref, g_ref, sc_ref, sh_ref, o_ref):
    o_ref[...] = _norm_mod(x_ref[...], g_ref[...], sc_ref[...], sh_ref[...])


def _prenorm(tr, x3, g, mod, layer):
    tm = min(tr.tm, 512)
    trp = tr._replace(tm=tm)
    nb = tm // SUBLANES
    ti, tj = (lambda i: i), (lambda i: 0)
    return pl.pallas_call(
        _prenorm_kernel,
        grid=(x3.shape[0] // nb,),
        in_specs=[pl.BlockSpec((nb, SUBLANES, D_MODEL), lambda i: (i, 0, 0)),
                  pl.BlockSpec((None, 1, D_MODEL), lambda i: (layer, 0, 0)),
                  _mod_spec(trp, layer, 1, D_MODEL, ti, tj),
                  _mod_spec(trp, layer, 0, D_MODEL, ti, tj)],
        out_specs=pl.BlockSpec((nb, SUBLANES, D_MODEL), lambda i: (i, 0, 0)),
        out_shape=jax.ShapeDtypeStruct(x3.shape, F32),
        compiler_params=_params("parallel"),
        name="prenorm",
    )(x3, g, mod, mod)


def _s5_prep_kernel(lre_ref, lim_ref, ldt_ref, bre_ref, bim_ref, cre_ref, cim_ref,
                    are_ref, aim_ref, bbre_ref, bbim_ref, ccre_ref, ccim_ref, pre_ref, pim_ref, *, n_pow):
    lr, li = lre_ref[...], lim_ref[...]
    dt = jnp.exp(ldt_ref[...])
    mag = jnp.exp(lr * dt)
    are = mag * jnp.cos(li * dt)
    aim = mag * jnp.sin(li * dt)
    er, ei = are - 1.0, aim
    den = lr * lr + li * li
    qre = (er * lr + ei * li) / den
    qim = (ei * lr - er * li) / den
    are_ref[...] = are
    aim_ref[...] = aim

    br, bi = bre_ref[...], bim_ref[...]
    keep = (jnp.right_shift(lax.broadcasted_iota(jnp.int32, br.shape, 0), LOG2_GROUP)
            == jnp.right_shift(lax.broadcasted_iota(jnp.int32, br.shape, 1), LOG2_STATE))
    bbre_ref[...] = jnp.where(keep, qre * br - qim * bi, 0.0).astype(BF16)
    bbim_ref[...] = jnp.where(keep, qre * bi + qim * br, 0.0).astype(BF16)

    cr, ci = cre_ref[...], cim_ref[...]
    keep = (jnp.right_shift(lax.broadcasted_iota(jnp.int32, cr.shape, 0), LOG2_STATE)
            == jnp.right_shift(lax.broadcasted_iota(jnp.int32, cr.shape, 1), LOG2_GROUP))
    ccre_ref[...] = jnp.where(keep, cr, 0.0).astype(BF16)
    ccim_ref[...] = jnp.where(keep, -ci, 0.0).astype(BF16)

    pre_ref[0:1, :] = are
    pim_ref[0:1, :] = aim

    def step(t, carry):
        pr, pi = carry
        nr = pr * are - pi * aim
        ni = pr * aim + pi * are
        pre_ref[pl.ds(t, 1), :] = nr
        pim_ref[pl.ds(t, 1), :] = ni
        return nr, ni

    lax.fori_loop(1, n_pow, step, (are, aim))


def _s5_prep(a_re, a_im, log_dt, b_re, b_im, c_re, c_im, n_pow):
    nt = N_LANE_TILES
    tile3 = lambda a: a.reshape(nt, 1, STATE_TILE)
    ldt = jnp.broadcast_to(log_dt[:, None], (SSM_GROUPS, SSM_STATE))
    b_rows = lambda b: jnp.tile(b.transpose(0, 2, 1).reshape(D_MODEL, SSM_STATE), (1, GROUPS_PER_TILE))
    c_rows = lambda c: jnp.tile(c.transpose(0, 2, 1).reshape(SSM_GROUPS * SSM_STATE, SSM_GROUP),
                                (1, GROUPS_PER_TILE))
    vec = pl.BlockSpec((None, 1, STATE_TILE), lambda k: (k, 0, 0))
    bspec = pl.BlockSpec((LANES, STATE_TILE), lambda k: (k, 0))
    cspec = pl.BlockSpec((STATE_TILE, LANES), lambda k: (k, 0))
    pspec = pl.BlockSpec((None, n_pow, STATE_TILE), lambda k: (k, 0, 0))
    return pl.pallas_call(
        functools.partial(_s5_prep_kernel, n_pow=n_pow),
        grid=(nt,),
        in_specs=[vec, vec, vec, bspec, bspec, cspec, cspec],
        out_specs=[vec, vec, bspec, bspec, cspec, cspec, pspec, pspec],
        out_shape=[jax.ShapeDtypeStruct((nt, 1, STATE_TILE), F32)] * 2
        + [jax.ShapeDtypeStruct((D_MODEL, STATE_TILE), BF16)] * 2
        + [jax.ShapeDtypeStruct((SSM_GROUPS * SSM_STATE, LANES), BF16)] * 2
        + [jax.ShapeDtypeStruct((nt, n_pow, STATE_TILE), F32)] * 2,
        compiler_params=_params("parallel"),
        name="s5_prep",
    )(tile3(a_re), tile3(a_im), tile3(ldt), b_rows(b_re), b_rows(b_im), c_rows(c_re), c_rows(c_im))


S5_CHUNK = 256


def _s5_kernel(*refs, n_blocks, seg_len, chained):
    (u_ref, d_ref, bbre_ref, bbim_ref, ccre_ref, ccim_ref, are_ref, aim_ref, x0_ref, x1_ref,
     g_ref, sre_ref, sim_ref, up, hre, him, gp, gn) = refs
    n_groups = n_blocks * seg_len
    rows = n_groups * SUBLANES

    def src_row(i):
        if n_blocks == 1:
            return i
        return (i // seg_len) * (SUBLANES * seg_len) + (i % seg_len)

    def grp(i):
        return pl.ds(pl.multiple_of(i * SUBLANES, SUBLANES), SUBLANES)

    def regroup(i, c):
        up[grp(i), :] = u_ref[pl.ds(src_row(i), SUBLANES, stride=seg_len), :]
        return c

    lax.fori_loop(0, n_groups, regroup, 0, unroll=8)

    for c in range(rows // S5_CHUNK):
        sl = slice(c * S5_CHUNK, (c + 1) * S5_CHUNK)
        ub = up[sl, :].astype(BF16)
        hre[sl, :] = _dot(ub, bbre_ref[...])
        him[sl, :] = _dot(ub, bbim_ref[...])

    ar = jnp.broadcast_to(are_ref[...], (SUBLANES, STATE_TILE))
    ai = jnp.broadcast_to(aim_ref[...], (SUBLANES, STATE_TILE))

    def scan_block(nb, c):
        if chained:
            h0 = (jnp.zeros((SUBLANES, STATE_TILE), F32),) * 2
        else:
            h0 = (x0_ref[grp(nb), :], x1_ref[grp(nb), :])

        def step(t, carry):
            hr, hi = carry
            r = grp(nb * seg_len + t)
            nr = ar * hr - ai * hi + hre[r, :]
            ni = ar * hi + ai * hr + him[r, :]
            hre[r, :] = nr
            him[r, :] = ni
            return nr, ni

        hr, hi = lax.fori_loop(0, seg_len, step, h0, unroll=8)
        if not chained:
            sre_ref[grp(nb), :] = hr
            sim_ref[grp(nb), :] = hi
        return c

    lax.fori_loop(0, n_blocks, scan_block, 0)

    if chained:
        er, ei = hre[rows - SUBLANES:rows, :], him[rows - SUBLANES:rows, :]
        pr, pi = x0_ref[seg_len - 1:seg_len, :], x1_ref[seg_len - 1:seg_len, :]
        row = lax.broadcasted_iota(jnp.int32, (SUBLANES, STATE_TILE), 0)
        xr = jnp.zeros((SUBLANES, STATE_TILE), F32)
        xi = xr
        for _ in range(SUBLANES - 1):
            yr = er + pr * xr - pi * xi
            yi = ei + pr * xi + pi * xr
            xr = jnp.where(row == 0, 0.0, pltpu.roll(yr, 1, 0))
            xi = jnp.where(row == 0, 0.0, pltpu.roll(yi, 1, 0))

        def fixup(t, c):
            r = grp(t)
            pr, pi = x0_ref[pl.ds(t, 1), :], x1_ref[pl.ds(t, 1), :]
            hre[r, :] = hre[r, :] + (pr * xr - pi * xi)
            him[r, :] = him[r, :] + (pr * xi + pi * xr)
            return c

        lax.fori_loop(0, seg_len, fixup, 0, unroll=8)
        sre_ref[...] = hre[rows - 1:rows, :]
        sim_ref[...] = him[rows - 1:rows, :]

    d = d_ref[...]
    for c in range(rows // S5_CHUNK):
        sl = slice(c * S5_CHUNK, (c + 1) * S5_CHUNK)
        y = _dot(hre[sl, :].astype(BF16), ccre_ref[...]) + _dot(him[sl, :].astype(BF16), ccim_ref[...])
        gp[sl, :] = jax.nn.gelu(y + d * up[sl, :])

    def ungroup(i, c):
        gn[pl.ds(src_row(i), SUBLANES, stride=seg_len), :] = gp[grp(i), :]
        return c

    lax.fori_loop(0, n_groups, ungroup, 0, unroll=8)
    g_ref[...] = gn[...].astype(BF16)


def _s5(tr, u2d, d_skip, tabs, h0):
    are, aim, bbre, bbim, ccre, ccim, pre, pim = tabs
    chained = h0 is None
    if chained:
        n_batch, rows = tr.n_seq, tr.seq_len
        n_blocks, seg_len = 1, tr.seq_len // SUBLANES
        assert pre.shape[1] == seg_len
        x_specs = [pl.BlockSpec((None, seg_len, STATE_TILE), lambda k, b: (k, 0, 0))] * 2
        x_args = (pre, pim)
        st_spec = pl.BlockSpec((None, 1, STATE_TILE), lambda k, b: (b, 0, k))
        st_shape = jax.ShapeDtypeStruct((tr.n_seq, 1, SSM_GROUPS * SSM_STATE), F32)
    else:
        n_batch, rows = 1, tr.n_seq * tr.seq_len
        n_blocks, seg_len = tr.n_seq // SUBLANES, tr.seq_len
        x_specs = [pl.BlockSpec((tr.n_seq, STATE_TILE), lambda k, b: (0, k))] * 2
        x_args = h0
        st_spec = pl.BlockSpec((tr.n_seq, STATE_TILE), lambda k, b: (0, k))
        st_shape = jax.ShapeDtypeStruct((tr.n_seq, SSM_GROUPS * SSM_STATE), F32)
    assert rows % S5_CHUNK == 0
    vec = pl.BlockSpec((None, 1, STATE_TILE), lambda k, b: (k, 0, 0))
    bspec = pl.BlockSpec((LANES, STATE_TILE), lambda k, b: (k, 0))
    cspec = pl.BlockSpec((STATE_TILE, LANES), lambda k, b: (k, 0))
    tok = pl.BlockSpec((rows, LANES), lambda k, b: (b, k))
    return pl.pallas_call(
        functools.partial(_s5_kernel, n_blocks=n_blocks, seg_len=seg_len, chained=chained),
        grid=(N_LANE_TILES, n_batch),
        in_specs=[tok, pl.BlockSpec((1, LANES), lambda k, b: (0, k)), bspec, bspec, cspec, cspec, vec, vec]
        + x_specs,
        out_specs=[tok, st_spec, st_spec],
        out_shape=[jax.ShapeDtypeStruct(u2d.shape, BF16), st_shape, st_shape],
        scratch_shapes=[pltpu.VMEM((rows, LANES), F32),
                        pltpu.VMEM((rows, STATE_TILE), F32),
                        pltpu.VMEM((rows, STATE_TILE), F32),
                        pltpu.VMEM((rows, LANES), F32),
                        pltpu.VMEM((rows, LANES), F32)],
        compiler_params=_params("parallel", "parallel"),
        name="s5",
    )(u2d, d_skip, bbre, bbim, ccre, ccim, are, aim, *x_args)


def _glu_res_kernel(a_ref, wa_ref, wb_ref, ba_ref, bb_ref, x_ref, gate_ref, o_ref):
    a = a_ref[...]
    za = _dot(a, wa_ref[...].astype(BF16)) + ba_ref[...]
    zb = _dot(a, wb_ref[...].astype(BF16)) + bb_ref[...]
    out = za * jax.nn.sigmoid(zb)
    o_ref[...] = x_ref[...] + gate_ref[...] * out.reshape(o_ref.shape)


def _glu_res(tr, a2d, w, b, x3, mod, layer):
    tm, tn = tr.tm, 512
    nb, nj = tm // SUBLANES, D_MODEL // tn
    b3 = b.reshape(1, 1, 2 * D_MODEL)
    ti, tj = (lambda j, i: i), (lambda j, i: j)
    xspec = pl.BlockSpec((nb, SUBLANES, tn), lambda j, i: (i, 0, j))
    return pl.pallas_call(
        _glu_res_kernel,
        grid=(nj, a2d.shape[0] // tm),
        in_specs=[pl.BlockSpec((tm, D_MODEL), lambda j, i: (i, 0)),
                  pl.BlockSpec((None, D_MODEL, tn), lambda j, i: (0, 0, j)),
                  pl.BlockSpec((None, D_MODEL, tn), lambda j, i: (0, 0, j + nj)),
                  pl.BlockSpec((None, 1, tn), lambda j, i: (0, 0, j)),
                  pl.BlockSpec((None, 1, tn), lambda j, i: (0, 0, j + nj)),
                  xspec,
                  _mod_spec(tr, layer, 2, tn, ti, tj)],
        out_specs=xspec,
        out_shape=jax.ShapeDtypeStruct(x3.shape, F32),
        compiler_params=_params("parallel", "parallel"),
        name="glu_res",
    )(a2d, w, w, b3, b3, x3, mod)


def _pw1_kernel(x_ref, g_ref, sc_ref, sh_ref, wa_ref, wb_ref, ba_ref, bb_ref, v_ref, h_scr):
    @pl.when(pl.program_id(1) == 0)
    def _():
        h = _norm_mod(x_ref[...], g_ref[...], sc_ref[...], sh_ref[...])
        h_scr[...] = h.reshape(h_scr.shape).astype(BF16)

    a = h_scr[...]
    za = _dot(a, wa_ref[...].astype(BF16)) + ba_ref[...]
    zb = _dot(a, wb_ref[...].astype(BF16)) + bb_ref[...]
    v_ref[...] = za * jax.nn.sigmoid(zb)


def _pw1(tr, x3, g, w, b, mod, layer):
    tm, tn = tr.tm, 512
    nb, nj = tm // SUBLANES, D_MODEL // tn
    b3 = b.reshape(1, 1, 2 * D_MODEL)
    ti, tj = (lambda i, j: i), (lambda i, j: 0)
    return pl.pallas_call(
        _pw1_kernel,
        grid=(x3.shape[0] // nb, nj),
        in_specs=[pl.BlockSpec((nb, SUBLANES, D_MODEL), lambda i, j: (i, 0, 0)),
                  pl.BlockSpec((None, 1, D_MODEL), lambda i, j: (layer, 0, 0)),
                  _mod_spec(tr, layer, 1, D_MODEL, ti, tj),
                  _mod_spec(tr, layer, 0, D_MODEL, ti, tj),
                  pl.BlockSpec((None, D_MODEL, tn), lambda i, j: (0, 0, j)),
                  pl.BlockSpec((None, D_MODEL, tn), lambda i, j: (0, 0, j + nj)),
                  pl.BlockSpec((None, 1, tn), lambda i, j: (0, 0, j)),
                  pl.BlockSpec((None, 1, tn), lambda i, j: (0, 0, j + nj))],
        out_specs=pl.BlockSpec((tm, tn), lambda i, j: (i, j)),
        out_shape=jax.ShapeDtypeStruct((x3.shape[0] * SUBLANES, D_MODEL), F32),
        scratch_shapes=[pltpu.VMEM((tm, D_MODEL), BF16)],
        compiler_params=_params("parallel", "arbitrary"),
        name="pw1",
    )(x3, g, mod, mod, w, w, b3, b3)


CONV_CHUNK = SUBLANES * SUBLANES
CONV_LANES = 512


def _conv_kernel(v_ref, halo_ref, w_ref, b_ref, o_ref, pad, *, tiles_per_seq):
    rows = v_ref.shape[0]
    first = (pl.program_id(0) % tiles_per_seq) == 0
    pad[0:HALO, :] = jnp.where(first, 0.0, halo_ref[...])
    pad[HALO:, :] = v_ref[...]
    w = [jnp.broadcast_to(w_ref[k:k + 1, :], (SUBLANES, LANES)) for k in range(CONV_WIDTH)]
    bias = jnp.broadcast_to(b_ref[...], (SUBLANES, LANES))

    def chunk(c, carry):
        base = pl.multiple_of(c * CONV_CHUNK, CONV_CHUNK)
        acc = [bias] * SUBLANES
        for o in range(CONV_WIDTH + SUBLANES - 1):
            win = pad[pl.ds(base + HIST_OFF + o, SUBLANES, stride=SUBLANES), :]
            for r in range(SUBLANES):
                if 0 <= o - r < CONV_WIDTH:
                    acc[r] = acc[r] + w[o - r] * win
        for r in range(SUBLANES):
            o_ref[pl.ds(base + r, SUBLANES, stride=SUBLANES), :] = acc[r]
        return carry

    lax.fori_loop(0, rows // CONV_CHUNK, chunk, 0)


def _conv_long(tr, v2d, w, b):
    rows = min(tr.seq_len, 2048)
    assert tr.seq_len % rows == 0 and rows % CONV_CHUNK == 0
    hb = rows // HALO
    return pl.pallas_call(
        functools.partial(_conv_kernel, tiles_per_seq=tr.seq_len // rows),
        grid=(v2d.shape[0] // rows, D_MODEL // LANES),
        in_specs=[pl.BlockSpec((rows, LANES), lambda i, l: (i, l)),
                  pl.BlockSpec((HALO, LANES), lambda i, l: (jnp.maximum(i * hb - 1, 0), l)),
                  pl.BlockSpec((None, CONV_WIDTH, LANES), lambda i, l: (0, 0, l)),
                  pl.BlockSpec((1, LANES), lambda i, l: (0, l))],
        out_specs=pl.BlockSpec((rows, LANES), lambda i, l: (i, l)),
        out_shape=jax.ShapeDtypeStruct(v2d.shape, F32),
        scratch_shapes=[pltpu.VMEM((HALO + rows, LANES), F32)],
        compiler_params=_params("parallel", "parallel"),
        name="conv_long",
    )(v2d, v2d, w, b)


def _conv_step_kernel(v_ref, cache_ref, w_ref, b_ref, o_ref, nc_ref, pad):
    seq_len = v_ref.shape[1]
    pad[:, HIST_OFF:HALO, :] = cache_ref[...]
    pad[:, HALO:, :] = v_ref[...]
    for lc in range(CONV_LANES // LANES):
        ls = slice(lc * LANES, (lc + 1) * LANES)
        acc = jnp.broadcast_to(b_ref[:, ls], (v_ref.shape[0], seq_len, LANES))
        for k in range(CONV_WIDTH):
            acc = acc + w_ref[k:k + 1, ls] * pad[:, HIST_OFF + k:HIST_OFF + k + seq_len, ls]
        o_ref[:, :, ls] = acc
    nc_ref[...] = pad[:, HALO + seq_len - CONV_HIST:, :]


def _conv_step(tr, v3, cache, w, b):
    ns = 16
    assert tr.seq_len <= CONV_HIST
    vspec = pl.BlockSpec((ns, tr.seq_len, CONV_LANES), lambda s, l: (s, 0, l))
    cspec = pl.BlockSpec((None, ns, CONV_HIST, CONV_LANES), lambda s, l: (0, s, 0, l))
    return pl.pallas_call(
        _conv_step_kernel,
        grid=(tr.n_seq // ns, D_MODEL // CONV_LANES),
        in_specs=[vspec, cspec,
                  pl.BlockSpec((None, CONV_WIDTH, CONV_LANES), lambda s, l: (0, 0, l)),
                  pl.BlockSpec((1, CONV_LANES), lambda s, l: (0, l))],
        out_specs=[vspec, cspec],
        out_shape=[jax.ShapeDtypeStruct(v3.shape, F32), jax.ShapeDtypeStruct(cache.shape, F32)],
        scratch_shapes=[pltpu.VMEM((ns, HALO + tr.seq_len, CONV_LANES), F32)],
        compiler_params=_params("parallel", "parallel"),
        name="conv_step",
    )(v3, cache, w, b)


def _pw2_kernel(c_ref, lg_ref, lb_ref, w_ref, b_ref, x_ref, gate_ref, o_ref, h_scr):
    @pl.when(pl.program_id(1) == 0)
    def _():
        c = c_ref[...]
        xc = c - jnp.mean(c, axis=-1, keepdims=True)
        y = xc * lax.rsqrt(jnp.mean(xc * xc, axis=-1, keepdims=True) + LN_EPS)
        h_scr[...] = jax.nn.silu(y * lg_ref[...] + lb_ref[...]).astype(BF16)

    out = _dot(h_scr[...], w_ref[...].astype(BF16)) + b_ref[...]
    o_ref[...] = x_ref[...] + gate_ref[...] * out.reshape(o_ref.shape)


def _pw2(tr, conv2d, ln_g, ln_b, w, b, x3, mod, layer):
    tm, tn = tr.tm, 512
    nb = tm // SUBLANES
    ti, tj = (lambda i, j: i), (lambda i, j: j)
    xspec = pl.BlockSpec((nb, SUBLANES, tn), lambda i, j: (i, 0, j))
    row = pl.BlockSpec((1, D_MODEL), lambda i, j: (0, 0))
    return pl.pallas_call(
        _pw2_kernel,
        grid=(conv2d.shape[0] // tm, D_MODEL // tn),
        in_specs=[pl.BlockSpec((tm, D_MODEL), lambda i, j: (i, 0)), row, row,
                  pl.BlockSpec((None, D_MODEL, tn), lambda i, j: (0, 0, j)),
                  pl.BlockSpec((1, tn), lambda i, j: (0, j)),
                  xspec,
                  _mod_spec(tr, layer, 2, tn, ti, tj)],
        out_specs=xspec,
        out_shape=jax.ShapeDtypeStruct(x3.shape, F32),
        scratch_shapes=[pltpu.VMEM((tm, D_MODEL), BF16)],
        compiler_params=_params("parallel", "arbitrary"),
        name="pw2",
    )(conv2d, ln_g, ln_b, w, b, x3, mod)


def _mlp_kernel(*refs, final):
    if final:
        x_ref, g_ref, sc_ref, sh_ref, gate_ref, w1_ref, w2_ref, fg_ref, o_ref, h_scr = refs
    else:
        x_ref, g_ref, sc_ref, sh_ref, gate_ref, w1_ref, w2_ref, o_ref, h_scr = refs
    f = pl.program_id(1)

    @pl.when(f == 0)
    def _():
        h = _norm_mod(x_ref[...], g_ref[...], sc_ref[...], sh_ref[...])
        h_scr[...] = h.reshape(h_scr.shape).astype(BF16)
        o_ref[...] = jnp.zeros(o_ref.shape, F32)

    a = jnp.maximum(_dot(h_scr[...], w1_ref[...].astype(BF16)), 0.0)
    o_ref[...] += _dot((a * a).astype(BF16), w2_ref[...].astype(BF16)).reshape(o_ref.shape)

    @pl.when(f == pl.num_programs(1) - 1)
    def _():
        out = x_ref[...] + gate_ref[...] * o_ref[...]
        if final:
            ms = jnp.mean(out * out, axis=-1, keepdims=True)
            out = out * lax.rsqrt(ms + RMS_EPS) * fg_ref[...]
        o_ref[...] = out


def _mlp(tr, x3, g, w1, w2, mod, layer, final_g=None):
    tm, tf = tr.tm, 256
    nb = tm // SUBLANES
    ti, tj = (lambda i, f: i), (lambda i, f: 0)
    xspec = pl.BlockSpec((nb, SUBLANES, D_MODEL), lambda i, f: (i, 0, 0))
    final = final_g is not None
    extra_specs = [pl.BlockSpec((1, D_MODEL), lambda i, f: (0, 0))] if final else []
    extra_args = (final_g,) if final else ()
    return pl.pallas_call(
        functools.partial(_mlp_kernel, final=final),
        grid=(x3.shape[0] // nb, D_FF // tf),
        in_specs=[pl.BlockSpec((nb, SUBLANES, D_MODEL), lambda i, f: (i, 0, 0),
                               pipeline_mode=pl.Buffered(1)),
                  pl.BlockSpec((None, 1, D_MODEL), lambda i, f: (layer, 0, 0)),
                  _mod_spec(tr, layer, 4, D_MODEL, ti, tj),
                  _mod_spec(tr, layer, 3, D_MODEL, ti, tj),
                  _mod_spec(tr, layer, 5, D_MODEL, ti, tj),
                  pl.BlockSpec((None, D_MODEL, tf), lambda i, f: (layer, 0, f)),
                  pl.BlockSpec((None, tf, D_MODEL), lambda i, f: (layer, f, 0))] + extra_specs,
        out_specs=xspec,
        out_shape=jax.ShapeDtypeStruct(x3.shape, F32),
        scratch_shapes=[pltpu.VMEM((tm, D_MODEL), BF16)],
        compiler_params=_params("parallel", "arbitrary"),
        name="mlp",
    )(x3, g, mod, mod, mod, w1, w2, *extra_args)


def _trunk(tr, x, mod, h0, cache, tabs, p):
    tokens = tr.n_seq * tr.seq_len
    x3 = x.reshape(tokens // SUBLANES, SUBLANES, D_MODEL)

    u3 = _prenorm(tr, x3, p["rms_g_mix"], mod, 0)
    g2d, s_re, s_im = _s5(tr, u3.reshape(tokens, D_MODEL), p["ssm_d"], tabs, h0)
    x3 = _glu_res(tr, g2d, p["ssm_w_glu"], p["ssm_b_glu"], x3, mod, 0)
    x3 = _mlp(tr, x3, p["rms_g_mlp"], p["mlp_w1"], p["mlp_w2"], mod, 0)

    v2d = _pw1(tr, x3, p["rms_g_mix"], p["conv_w_pw1"], p["conv_b_pw1"], mod, 1)
    if cache is None:
        conv2d = _conv_long(tr, v2d, p["conv_w_dw"], p["conv_b_dw"])
        new_cache = v2d.reshape(1, tr.n_seq, tr.seq_len, D_MODEL)[:, :, tr.seq_len - CONV_HIST:]
    else:
        conv3, new_cache = _conv_step(tr, v2d.reshape(tr.n_seq, tr.seq_len, D_MODEL), cache,
                                      p["conv_w_dw"], p["conv_b_dw"])
        conv2d = conv3.reshape(tokens, D_MODEL)
    x3 = _pw2(tr, conv2d, p["conv_ln_g"], p["conv_ln_b"], p["conv_w_pw2"], p["conv_b_pw2"], x3, mod, 1)
    y3 = _mlp(tr, x3, p["rms_g_mlp"], p["mlp_w1"], p["mlp_w2"], mod, 1, final_g=p["final_g"])

    state_shape = (1, tr.n_seq, SSM_GROUPS, SSM_STATE)
    return (y3.reshape(tr.n_seq, tr.seq_len, D_MODEL), s_re.reshape(state_shape), s_im.reshape(state_shape),
            new_cache)


def kernel(x_prompt, x_sample, state_ssm_re, state_ssm_im, cache_conv, c_prompt, c_sample, rms_g_mix, rms_g_mlp, w_ada, b_ada, ssm_a_re, ssm_a_im, ssm_log_dt, ssm_b_re, ssm_b_im, ssm_c_re, ssm_c_im, ssm_d, ssm_w_glu, ssm_b_glu, conv_w_pw1, conv_b_pw1, conv_w_dw, conv_b_dw, conv_ln_g, conv_ln_b, conv_w_pw2, conv_b_pw2, mlp_w1, mlp_w2, final_g):
    bp, lp, _ = x_prompt.shape
    bs, ls, _ = x_sample.shape
    assert w_ada.shape[0] == 2 and ssm_a_re.shape[0] == 1 and conv_w_dw.shape[0] == 1
    prompt = Trunk(bp, lp, 1024)
    sample = Trunk(bs, ls, bs * ls)

    depth = w_ada.shape[0]
    p = dict(rms_g_mix=rms_g_mix.reshape(depth, 1, D_MODEL), rms_g_mlp=rms_g_mlp.reshape(depth, 1, D_MODEL),
             ssm_d=ssm_d, ssm_w_glu=ssm_w_glu, ssm_b_glu=ssm_b_glu,
             conv_w_pw1=conv_w_pw1, conv_b_pw1=conv_b_pw1, conv_w_dw=conv_w_dw, conv_b_dw=conv_b_dw,
             conv_ln_g=conv_ln_g, conv_ln_b=conv_ln_b, conv_w_pw2=conv_w_pw2, conv_b_pw2=conv_b_pw2,
             mlp_w1=mlp_w1, mlp_w2=mlp_w2, final_g=final_g.reshape(1, D_MODEL))

    n_c = bp + bs
    pad_rows = -n_c % SUBLANES
    c_all = jnp.concatenate([c_prompt, c_sample, jnp.zeros((pad_rows, D_MODEL), F32)], axis=0)
    mod = _ada(c_all, w_ada, b_ada)
    mod_p = mod[:, :bp].reshape(2, bp, 1, 6 * D_MODEL)
    mod_s = mod[:, bp:n_c].reshape(2, bs, 1, 6 * D_MODEL)

    tabs = _s5_prep(ssm_a_re[0], ssm_a_im[0], ssm_log_dt[0], ssm_b_re[0], ssm_b_im[0],
                    ssm_c_re[0], ssm_c_im[0], lp // SUBLANES)

    n_state = SSM_GROUPS * SSM_STATE
    h0 = (state_ssm_re.reshape(bs, n_state), state_ssm_im.reshape(bs, n_state))
    y_p, p_re, p_im, p_buf = _trunk(prompt, x_prompt, mod_p, None, None, tabs, p)
    y_s, s_re, s_im, s_buf = _trunk(sample, x_sample, mod_s, h0, cache_conv, tabs, p)
    return (y_p, y_s, p_re, p_im, p_buf, s_re, s_im, s_buf)
```

```python
import collections
import functools

import jax
import jax.numpy as jnp
from jax import lax
from jax.experimental import pallas as pl
from jax.experimental.pallas import tpu as pltpu

F32 = jnp.float32
BF16 = jnp.bfloat16

D_MODEL = 2048
D_FF = 4 * D_MODEL
SSM_GROUP = 16
SSM_GROUPS = D_MODEL // SSM_GROUP
SSM_STATE = 64
LOG2_GROUP = SSM_GROUP.bit_length() - 1
LOG2_STATE = SSM_STATE.bit_length() - 1
assert SSM_GROUP == 1 << LOG2_GROUP and SSM_STATE == 1 << LOG2_STATE
CONV_WIDTH = 31
CONV_HIST = CONV_WIDTH - 1
RMS_EPS = 1e-6
LN_EPS = 1e-5

LANES = 128
SUBLANES = 8
VMEM_LIMIT_BYTES = 56 * 1024 * 1024

GROUPS_PER_TILE = LANES // SSM_GROUP
STATE_TILE = GROUPS_PER_TILE * SSM_STATE
N_LANE_TILES = D_MODEL // LANES
HALO = 32
HIST_OFF = HALO - CONV_HIST

Trunk = collections.namedtuple("Trunk", "n_seq seq_len tm")


def _params(*sem):
    return pltpu.CompilerParams(dimension_semantics=sem, vmem_limit_bytes=VMEM_LIMIT_BYTES)


def _dot(a, b):
    return jnp.dot(a, b, preferred_element_type=F32)


def _norm_mod(x3, g, sc, sh):
    ms = jnp.mean(x3 * x3, axis=-1, keepdims=True)
    return (x3 * lax.rsqrt(ms + RMS_EPS) * g) * (1.0 + sc) + sh


def _mod_spec(tr, layer, part, tn, ti, tj):
    nblk = D_MODEL // tn
    if tr.seq_len >= tr.tm:
        per = tr.seq_len // tr.tm
        return pl.BlockSpec((None, 1, 1, tn), lambda *g: (layer, ti(*g) // per, 0, part * nblk + tj(*g)))
    assert tr.seq_len == SUBLANES
    return pl.BlockSpec((None, tr.tm // SUBLANES, 1, tn), lambda *g: (layer, ti(*g), 0, part * nblk + tj(*g)))


def _ada_kernel(c_ref, w_ref, b_ref, o_ref):
    ca = jax.nn.silu(c_ref[...]).astype(BF16)
    o_ref[...] = _dot(ca, w_ref[...].astype(BF16)) + b_ref[...]


def _ada(c_all, w_ada, b_ada):
    depth, d, n = w_ada.shape
    rows = c_all.shape[0]
    tn = 1024
    return pl.pallas_call(
        _ada_kernel,
        grid=(depth, n // tn),
        in_specs=[pl.BlockSpec((rows, d), lambda l, j: (0, 0)),
                  pl.BlockSpec((None, d, tn), lambda l, j: (l, 0, j)),
                  pl.BlockSpec((None, 1, tn), lambda l, j: (l, 0, j))],
        out_specs=pl.BlockSpec((None, rows, tn), lambda l, j: (l, 0, j)),
        out_shape=jax.ShapeDtypeStruct((depth, rows, n), F32),
        compiler_params=_params("parallel", "parallel"),
        name="ada",
    )(c_all, w_ada, b_ada.reshape(depth, 1, n))


def _prenorm_kernel(x_ref, g_ref, sc_ref, sh_ref, o_ref):
    h = _norm_mod(x_ref[...], g_ref[...], sc_ref[...], sh_ref[...])
    o_ref[...] = h.reshape(o_ref.shape).astype(o_ref.dtype)


def _prenorm(tr, x3, g, mod, layer, sublayer, dtype):
    tm = min(tr.tm, 512)
    trp = tr._replace(tm=tm)
    nb = tm // SUBLANES
    ti, tj = (lambda i: i), (lambda i: 0)
    return pl.pallas_call(
        _prenorm_kernel,
        grid=(x3.shape[0] // nb,),
        in_specs=[pl.BlockSpec((nb, SUBLANES, D_MODEL), lambda i: (i, 0, 0)),
                  pl.BlockSpec((None, 1, D_MODEL), lambda i: (layer, 0, 0)),
                  _mod_spec(trp, layer, 3 * sublayer + 1, D_MODEL, ti, tj),
                  _mod_spec(trp, layer, 3 * sublayer, D_MODEL, ti, tj)],
        out_specs=pl.BlockSpec((tm, D_MODEL), lambda i: (i, 0)),
        out_shape=jax.ShapeDtypeStruct((x3.shape[0] * SUBLANES, D_MODEL), dtype),
        compiler_params=_params("parallel"),
        name="prenorm",
    )(x3, g, mod, mod)


def _s5_prep_kernel(lre_ref, lim_ref, ldt_ref, bre_ref, bim_ref, cre_ref, cim_ref,
                    are_ref, aim_ref, bbre_ref, bbim_ref, ccre_ref, ccim_ref, pre_ref, pim_ref, *, n_pow):
    lr, li = lre_ref[...], lim_ref[...]
    dt = jnp.exp(ldt_ref[...])
    mag = jnp.exp(lr * dt)
    are = mag * jnp.cos(li * dt)
    aim = mag * jnp.sin(li * dt)
    er, ei = are - 1.0, aim
    den = lr * lr + li * li
    qre = (er * lr + ei * li) / den
    qim = (ei * lr - er * li) / den
    are_ref[...] = are
    aim_ref[...] = aim

    br, bi = bre_ref[...], bim_ref[...]
    keep = (jnp.right_shift(lax.broadcasted_iota(jnp.int32, br.shape, 0), LOG2_GROUP)
            == jnp.right_shift(lax.broadcasted_iota(jnp.int32, br.shape, 1), LOG2_STATE))
    bbre_ref[...] = jnp.where(keep, qre * br - qim * bi, 0.0).astype(BF16)
    bbim_ref[...] = jnp.where(keep, qre * bi + qim * br, 0.0).astype(BF16)

    cr, ci = cre_ref[...], cim_ref[...]
    keep = (jnp.right_shift(lax.broadcasted_iota(jnp.int32, cr.shape, 0), LOG2_STATE)
            == jnp.right_shift(lax.broadcasted_iota(jnp.int32, cr.shape, 1), LOG2_GROUP))
    ccre_ref[...] = jnp.where(keep, cr, 0.0).astype(BF16)
    ccim_ref[...] = jnp.where(keep, -ci, 0.0).astype(BF16)

    pre_ref[0:1, :] = are
    pim_ref[0:1, :] = aim

    def step(t, carry):
        pr, pi = carry
        nr = pr * are - pi * aim
        ni = pr * aim + pi * are
        pre_ref[pl.ds(t, 1), :] = nr
        pim_ref[pl.ds(t, 1), :] = ni
        return nr, ni

    lax.fori_loop(1, n_pow, step, (are, aim))


def _s5_prep(a_re, a_im, log_dt, b_re, b_im, c_re, c_im, n_pow):
    nt = N_LANE_TILES
    tile3 = lambda a: a.reshape(nt, 1, STATE_TILE)
    ldt = jnp.broadcast_to(log_dt[:, None], (SSM_GROUPS, SSM_STATE))
    b_rows = lambda b: jnp.tile(b.transpose(0, 2, 1).reshape(D_MODEL, SSM_STATE), (1, GROUPS_PER_TILE))
    c_rows = lambda c: jnp.tile(c.transpose(0, 2, 1).reshape(SSM_GROUPS * SSM_STATE, SSM_GROUP),
                                (1, GROUPS_PER_TILE))
    vec = pl.BlockSpec((None, 1, STATE_TILE), lambda k: (k, 0, 0))
    bspec = pl.BlockSpec((LANES, STATE_TILE), lambda k: (k, 0))
    cspec = pl.BlockSpec((STATE_TILE, LANES), lambda k: (k, 0))
    pspec = pl.BlockSpec((None, n_pow, STATE_TILE), lambda k: (k, 0, 0))
    return pl.pallas_call(
        functools.partial(_s5_prep_kernel, n_pow=n_pow),
        grid=(nt,),
        in_specs=[vec, vec, vec, bspec, bspec, cspec, cspec],
        out_specs=[vec, vec, bspec, bspec, cspec, cspec, pspec, pspec],
        out_shape=[jax.ShapeDtypeStruct((nt, 1, STATE_TILE), F32)] * 2
        + [jax.ShapeDtypeStruct((D_MODEL, STATE_TILE), BF16)] * 2
        + [jax.ShapeDtypeStruct((SSM_GROUPS * SSM_STATE, LANES), BF16)] * 2
        + [jax.ShapeDtypeStruct((nt, n_pow, STATE_TILE), F32)] * 2,
        compiler_params=_params("parallel"),
        name="s5_prep",
    )(tile3(a_re), tile3(a_im), tile3(ldt), b_rows(b_re), b_rows(b_im), c_rows(c_re), c_rows(c_im))


S5_CHUNK = 256


def _s5_kernel(*refs, n_blocks, seg_len, chained):
    (u_ref, d_ref, bbre_ref, bbim_ref, ccre_ref, ccim_ref, are_ref, aim_ref, x0_ref, x1_ref,
     g_ref, sre_ref, sim_ref, up, hre, him, gp, gn) = refs
    n_groups = n_blocks * seg_len
    rows = n_groups * SUBLANES

    def grp(i):
        return pl.ds(pl.multiple_of(i * SUBLANES, SUBLANES), SUBLANES)

    def natural(lane):
        start = lane * seg_len
        return pl.ds(start if isinstance(lane, int) else pl.multiple_of(start, SUBLANES), seg_len)

    def regrouped(lane):
        return pl.ds((lane // SUBLANES) * (seg_len * SUBLANES) + lane % SUBLANES, seg_len, stride=SUBLANES)

    def for_each_lane(body):
        if n_blocks == 1:
            for lane in range(SUBLANES):
                body(lane, 0)
        else:
            lax.fori_loop(0, n_blocks * SUBLANES, body, 0, unroll=8)

    def regroup(lane, c):
        up[regrouped(lane), :] = u_ref[natural(lane), :]
        return c

    for_each_lane(regroup)

    for c in range(rows // S5_CHUNK):
        sl = slice(c * S5_CHUNK, (c + 1) * S5_CHUNK)
        ub = up[sl, :].astype(BF16)
        hre[sl, :] = _dot(ub, bbre_ref[...])
        him[sl, :] = _dot(ub, bbim_ref[...])

    ar = jnp.broadcast_to(are_ref[...], (SUBLANES, STATE_TILE))
    ai = jnp.broadcast_to(aim_ref[...], (SUBLANES, STATE_TILE))

    def scan_block(nb, c):
        if chained:
            h0 = (jnp.zeros((SUBLANES, STATE_TILE), F32),) * 2
        else:
            h0 = (x0_ref[grp(nb), :], x1_ref[grp(nb), :])

        def step(t, carry):
            hr, hi = carry
            r = grp(nb * seg_len + t)
            nr = ar * hr - ai * hi + hre[r, :]
            ni = ar * hi + ai * hr + him[r, :]
            hre[r, :] = nr
            him[r, :] = ni
            return nr, ni

        hr, hi = lax.fori_loop(0, seg_len, step, h0, unroll=8)
        if not chained:
            sre_ref[grp(nb), :] = hr
            sim_ref[grp(nb), :] = hi
        return c

    lax.fori_loop(0, n_blocks, scan_block, 0)

    if chained:
        er, ei = hre[rows - SUBLANES:rows, :], him[rows - SUBLANES:rows, :]
        pr, pi = x0_ref[seg_len - 1:seg_len, :], x1_ref[seg_len - 1:seg_len, :]
        row = lax.broadcasted_iota(jnp.int32, (SUBLANES, STATE_TILE), 0)
        xr = jnp.zeros((SUBLANES, STATE_TILE), F32)
        xi = xr
        for _ in range(SUBLANES - 1):
            yr = er + pr * xr - pi * xi
            yi = ei + pr * xi + pi * xr
            xr = jnp.where(row == 0, 0.0, pltpu.roll(yr, 1, 0))
            xi = jnp.where(row == 0, 0.0, pltpu.roll(yi, 1, 0))

        def fixup(t, c):
            r = grp(t)
            pr, pi = x0_ref[pl.ds(t, 1), :], x1_ref[pl.ds(t, 1), :]
            hre[r, :] = hre[r, :] + (pr * xr - pi * xi)
            him[r, :] = him[r, :] + (pr * xi + pi * xr)
            return c

        lax.fori_loop(0, seg_len, fixup, 0, unroll=8)
        sre_ref[...] = hre[rows - 1:rows, :]
        sim_ref[...] = him[rows - 1:rows, :]

    d = d_ref[...]
    for c in range(rows // S5_CHUNK):
        sl = slice(c * S5_CHUNK, (c + 1) * S5_CHUNK)
        y = _dot(hre[sl, :].astype(BF16), ccre_ref[...]) + _dot(him[sl, :].astype(BF16), ccim_ref[...])
        gp[sl, :] = jax.nn.gelu(y + d * up[sl, :])

    def ungroup(lane, c):
        gn[natural(lane), :] = gp[regrouped(lane), :]
        return c

    for_each_lane(ungroup)
    g_ref[...] = gn[...].astype(BF16)


def _s5(tr, u2d, d_skip, tabs, h0):
    are, aim, bbre, bbim, ccre, ccim, pre, pim = tabs
    chained = h0 is None
    if chained:
        n_batch, rows = tr.n_seq, tr.seq_len
        n_blocks, seg_len = 1, tr.seq_len // SUBLANES
        assert pre.shape[1] == seg_len
        x_specs = [pl.BlockSpec((None, seg_len, STATE_TILE), lambda k, b: (k, 0, 0))] * 2
        x_args = (pre, pim)
        st_spec = pl.BlockSpec((None, 1, STATE_TILE), lambda k, b: (b, 0, k))
        st_shape = jax.ShapeDtypeStruct((tr.n_seq, 1, SSM_GROUPS * SSM_STATE), F32)
    else:
        n_batch, rows = 1, tr.n_seq * tr.seq_len
        n_blocks, seg_len = tr.n_seq // SUBLANES, tr.seq_len
        x_specs = [pl.BlockSpec((tr.n_seq, STATE_TILE), lambda k, b: (0, k))] * 2
        x_args = h0
        st_spec = pl.BlockSpec((tr.n_seq, STATE_TILE), lambda k, b: (0, k))
        st_shape = jax.ShapeDtypeStruct((tr.n_seq, SSM_GROUPS * SSM_STATE), F32)
    assert rows % S5_CHUNK == 0
    vec = pl.BlockSpec((None, 1, STATE_TILE), lambda k, b: (k, 0, 0))
    bspec = pl.BlockSpec((LANES, STATE_TILE), lambda k, b: (k, 0))
    cspec = pl.BlockSpec((STATE_TILE, LANES), lambda k, b: (k, 0))
    tok = pl.BlockSpec((rows, LANES), lambda k, b: (b, k))
    return pl.pallas_call(
        functools.partial(_s5_kernel, n_blocks=n_blocks, seg_len=seg_len, chained=chained),
        grid=(N_LANE_TILES, n_batch),
        in_specs=[tok, pl.BlockSpec((1, LANES), lambda k, b: (0, k)), bspec, bspec, cspec, cspec, vec, vec]
        + x_specs,
        out_specs=[tok, st_spec, st_spec],
        out_shape=[jax.ShapeDtypeStruct(u2d.shape, BF16), st_shape, st_shape],
        scratch_shapes=[pltpu.VMEM((rows, LANES), F32),
                        pltpu.VMEM((rows, STATE_TILE), F32),
                        pltpu.VMEM((rows, STATE_TILE), F32),
                        pltpu.VMEM((rows, LANES), F32),
                        pltpu.VMEM((rows, LANES), F32)],
        compiler_params=_params("parallel", "parallel"),
        name="s5",
    )(u2d, d_skip, bbre, bbim, ccre, ccim, are, aim, *x_args)


def _glu_kernel(*refs, residual):
    if residual:
        a_ref, wa_ref, wb_ref, ba_ref, bb_ref, x_ref, gate_ref, o_ref = refs
    else:
        a_ref, wa_ref, wb_ref, ba_ref, bb_ref, o_ref = refs
    a = a_ref[...]
    za = _dot(a, wa_ref[...].astype(BF16)) + ba_ref[...]
    zb = _dot(a, wb_ref[...].astype(BF16)) + bb_ref[...]
    out = za * jax.nn.sigmoid(zb)
    if residual:
        out = x_ref[...] + gate_ref[...] * out.reshape(o_ref.shape)
    o_ref[...] = out


def _glu(tr, a2d, w, b, x3=None, mod=None, layer=None):
    tm, tn = tr.tm, 512
    nb, nj = tm // SUBLANES, D_MODEL // tn
    b3 = b.reshape(1, 1, 2 * D_MODEL)
    residual = x3 is not None
    in_specs = [pl.BlockSpec((tm, D_MODEL), lambda j, i: (i, 0)),
                pl.BlockSpec((None, D_MODEL, tn), lambda j, i: (0, 0, j)),
                pl.BlockSpec((None, D_MODEL, tn), lambda j, i: (0, 0, j + nj)),
                pl.BlockSpec((None, 1, tn), lambda j, i: (0, 0, j)),
                pl.BlockSpec((None, 1, tn), lambda j, i: (0, 0, j + nj))]
    args = (a2d, w, w, b3, b3)
    if residual:
        out_spec = pl.BlockSpec((nb, SUBLANES, tn), lambda j, i: (i, 0, j))
        out_shape = jax.ShapeDtypeStruct(x3.shape, F32)
        in_specs += [out_spec, _mod_spec(tr, layer, 2, tn, lambda j, i: i, lambda j, i: j)]
        args += (x3, mod)
    else:
        out_spec = pl.BlockSpec((tm, tn), lambda j, i: (i, j))
        out_shape = jax.ShapeDtypeStruct(a2d.shape, F32)
    return pl.pallas_call(
        functools.partial(_glu_kernel, residual=residual),
        grid=(nj, a2d.shape[0] // tm),
        in_specs=in_specs,
        out_specs=out_spec,
        out_shape=out_shape,
        compiler_params=_params("parallel", "parallel"),
        name="glu",
    )(*args)


CONV_CHUNK = SUBLANES * SUBLANES
CONV_LANES = 512


def _conv_kernel(v_ref, halo_ref, w_ref, b_ref, o_ref, pad, *, tiles_per_seq):
    rows = v_ref.shape[0]
    first = (pl.program_id(0) % tiles_per_seq) == 0
    pad[0:HALO, :] = jnp.where(first, 0.0, halo_ref[...])
    pad[HALO:, :] = v_ref[...]
    w = [jnp.broadcast_to(w_ref[k:k + 1, :], (SUBLANES, LANES)) for k in range(CONV_WIDTH)]
    bias = jnp.broadcast_to(b_ref[...], (SUBLANES, LANES))

    def chunk(c, carry):
        base = pl.multiple_of(c * CONV_CHUNK, CONV_CHUNK)
        acc = [bias] * SUBLANES
        for o in range(CONV_WIDTH + SUBLANES - 1):
            win = pad[pl.ds(base + HIST_OFF + o, SUBLANES, stride=SUBLANES), :]
            for r in range(SUBLANES):
                if 0 <= o - r < CONV_WIDTH:
                    acc[r] = acc[r] + w[o - r] * win
        for r in range(SUBLANES):
            o_ref[pl.ds(base + r, SUBLANES, stride=SUBLANES), :] = acc[r]
        return carry

    lax.fori_loop(0, rows // CONV_CHUNK, chunk, 0)


def _conv_long(tr, v2d, w, b):
    rows = min(tr.seq_len, 2048)
    assert tr.seq_len % rows == 0 and rows % CONV_CHUNK == 0
    hb = rows // HALO
    return pl.pallas_call(
        functools.partial(_conv_kernel, tiles_per_seq=tr.seq_len // rows),
        grid=(v2d.shape[0] // rows, D_MODEL // LANES),
        in_specs=[pl.BlockSpec((rows, LANES), lambda i, l: (i, l)),
                  pl.BlockSpec((HALO, LANES), lambda i, l: (jnp.maximum(i * hb - 1, 0), l)),
                  pl.BlockSpec((None, CONV_WIDTH, LANES), lambda i, l: (0, 0, l)),
                  pl.BlockSpec((1, LANES), lambda i, l: (0, l))],
        out_specs=pl.BlockSpec((rows, LANES), lambda i, l: (i, l)),
        out_shape=jax.ShapeDtypeStruct(v2d.shape, F32),
        scratch_shapes=[pltpu.VMEM((HALO + rows, LANES), F32)],
        compiler_params=_params("parallel", "parallel"),
        name="conv_long",
    )(v2d, v2d, w, b)


def _conv_step_kernel(v_ref, cache_ref, w_ref, b_ref, o_ref, nc_ref, pad):
    seq_len = v_ref.shape[1]
    pad[:, HIST_OFF:HALO, :] = cache_ref[...]
    pad[:, HALO:, :] = v_ref[...]
    for lc in range(CONV_LANES // LANES):
        ls = slice(lc * LANES, (lc + 1) * LANES)
        acc = jnp.broadcast_to(b_ref[:, ls], (v_ref.shape[0], seq_len, LANES))
        for k in range(CONV_WIDTH):
            acc = acc + w_ref[k:k + 1, ls] * pad[:, HIST_OFF + k:HIST_OFF + k + seq_len, ls]
        o_ref[:, :, ls] = acc
    nc_ref[...] = pad[:, HALO + seq_len - CONV_HIST:, :]


def _conv_step(tr, v3, cache, w, b):
    ns = 16
    assert tr.seq_len <= CONV_HIST
    vspec = pl.BlockSpec((ns, tr.seq_len, CONV_LANES), lambda s, l: (s, 0, l))
    cspec = pl.BlockSpec((None, ns, CONV_HIST, CONV_LANES), lambda s, l: (0, s, 0, l))
    return pl.pallas_call(
        _conv_step_kernel,
        grid=(tr.n_seq // ns, D_MODEL // CONV_LANES),
        in_specs=[vspec, cspec,
                  pl.BlockSpec((None, CONV_WIDTH, CONV_LANES), lambda s, l: (0, 0, l)),
                  pl.BlockSpec((1, CONV_LANES), lambda s, l: (0, l))],
        out_specs=[vspec, cspec],
        out_shape=[jax.ShapeDtypeStruct(v3.shape, F32), jax.ShapeDtypeStruct(cache.shape, F32)],
        scratch_shapes=[pltpu.VMEM((ns, HALO + tr.seq_len, CONV_LANES), F32)],
        compiler_params=_params("parallel", "parallel"),
        name="conv_step",
    )(v3, cache, w, b)


def _pw2_kernel(c_ref, lg_ref, lb_ref, w_ref, b_ref, x_ref, gate_ref, o_ref, h_scr):
    @pl.when(pl.program_id(1) == 0)
    def _():
        c = c_ref[...]
        xc = c - jnp.mean(c, axis=-1, keepdims=True)
        y = xc * lax.rsqrt(jnp.mean(xc * xc, axis=-1, keepdims=True) + LN_EPS)
        h_scr[...] = jax.nn.silu(y * lg_ref[...] + lb_ref[...]).astype(BF16)

    out = _dot(h_scr[...], w_ref[...].astype(BF16)) + b_ref[...]
    o_ref[...] = x_ref[...] + gate_ref[...] * out.reshape(o_ref.shape)


def _pw2(tr, conv2d, ln_g, ln_b, w, b, x3, mod, layer):
    tm, tn = tr.tm, 512
    nb = tm // SUBLANES
    ti, tj = (lambda i, j: i), (lambda i, j: j)
    xspec = pl.BlockSpec((nb, SUBLANES, tn), lambda i, j: (i, 0, j))
    row = pl.BlockSpec((1, D_MODEL), lambda i, j: (0, 0))
    return pl.pallas_call(
        _pw2_kernel,
        grid=(conv2d.shape[0] // tm, D_MODEL // tn),
        in_specs=[pl.BlockSpec((tm, D_MODEL), lambda i, j: (i, 0)), row, row,
                  pl.BlockSpec((None, D_MODEL, tn), lambda i, j: (0, 0, j)),
                  pl.BlockSpec((1, tn), lambda i, j: (0, j)),
                  xspec,
                  _mod_spec(tr, layer, 2, tn, ti, tj)],
        out_specs=xspec,
        out_shape=jax.ShapeDtypeStruct(x3.shape, F32),
        scratch_shapes=[pltpu.VMEM((tm, D_MODEL), BF16)],
        compiler_params=_params("parallel", "arbitrary"),
        name="pw2",
    )(conv2d, ln_g, ln_b, w, b, x3, mod)


MLP_TILE = 1024
MLP_TK = 2048


def _mlp_up_kernel(h_ref, w_ref, o_ref):
    a = jnp.maximum(_dot(h_ref[...], w_ref[...].astype(BF16)), 0.0)
    o_ref[...] = (a * a).astype(BF16)


def _mlp_up(tr, h2d, w1, layer):
    tm, tn = tr.tm, MLP_TILE
    return pl.pallas_call(
        _mlp_up_kernel,
        grid=(D_FF // tn, h2d.shape[0] // tm),
        in_specs=[pl.BlockSpec((tm, D_MODEL), lambda j, i: (i, 0)),
                  pl.BlockSpec((None, D_MODEL, tn), lambda j, i: (layer, 0, j))],
        out_specs=pl.BlockSpec((tm, tn), lambda j, i: (i, j)),
        out_shape=jax.ShapeDtypeStruct((h2d.shape[0], D_FF), BF16),
        compiler_params=_params("parallel", "parallel"),
        name="mlp_up",
    )(h2d, w1)


def _mlp_down_kernel(a_ref, w_ref, x_ref, gate_ref, o_ref):
    k = pl.program_id(2)

    @pl.when(k == 0)
    def _():
        o_ref[...] = jnp.zeros(o_ref.shape, F32)

    o_ref[...] += _dot(a_ref[...], w_ref[...].astype(BF16)).reshape(o_ref.shape)

    @pl.when(k == pl.num_programs(2) - 1)
    def _():
        o_ref[...] = x_ref[...] + gate_ref[...] * o_ref[...]


def _mlp_down(tr, a2d, w2, x3, mod, layer):
    tm, tn, tk = tr.tm, MLP_TILE, MLP_TK
    nb = tm // SUBLANES
    xspec = pl.BlockSpec((nb, SUBLANES, tn), lambda i, j, k: (i, 0, j))
    return pl.pallas_call(
        _mlp_down_kernel,
        grid=(a2d.shape[0] // tm, D_MODEL // tn, D_FF // tk),
        in_specs=[pl.BlockSpec((tm, tk), lambda i, j, k: (i, k)),
                  pl.BlockSpec((None, tk, tn), lambda i, j, k: (layer, k, j)),
                  xspec,
                  _mod_spec(tr, layer, 5, tn, lambda i, j, k: i, lambda i, j, k: j)],
        out_specs=xspec,
        out_shape=jax.ShapeDtypeStruct(x3.shape, F32),
        compiler_params=_params("parallel", "parallel", "arbitrary"),
        name="mlp_down",
    )(a2d, w2, x3, mod)


def _final_norm_kernel(x_ref, g_ref, o_ref):
    x = x_ref[...]
    ms = jnp.mean(x * x, axis=-1, keepdims=True)
    o_ref[...] = x * lax.rsqrt(ms + RMS_EPS) * g_ref[...]


def _final_norm(tr, x3, g):
    nb = min(tr.tm, 512) // SUBLANES
    spec = pl.BlockSpec((nb, SUBLANES, D_MODEL), lambda i: (i, 0, 0))
    return pl.pallas_call(
        _final_norm_kernel,
        grid=(x3.shape[0] // nb,),
        in_specs=[spec, pl.BlockSpec((1, D_MODEL), lambda i: (0, 0))],
        out_specs=spec,
        out_shape=jax.ShapeDtypeStruct(x3.shape, F32),
        compiler_params=_params("parallel"),
        name="final_norm",
    )(x3, g)


def _mlp(tr, x3, g, w1, w2, mod, layer):
    h2d = _prenorm(tr, x3, g, mod, layer, 1, BF16)
    return _mlp_down(tr, _mlp_up(tr, h2d, w1, layer), w2, x3, mod, layer)


def _trunk(tr, x, mod, h0, cache, tabs, p):
    tokens = tr.n_seq * tr.seq_len
    x3 = x.reshape(tokens // SUBLANES, SUBLANES, D_MODEL)

    u2d = _prenorm(tr, x3, p["rms_g_mix"], mod, 0, 0, F32)
    g2d, s_re, s_im = _s5(tr, u2d, p["ssm_d"], tabs, h0)
    x3 = _glu(tr, g2d, p["ssm_w_glu"], p["ssm_b_glu"], x3, mod, 0)
    x3 = _mlp(tr, x3, p["rms_g_mlp"], p["mlp_w1"], p["mlp_w2"], mod, 0)

    h2d = _prenorm(tr, x3, p["rms_g_mix"], mod, 1, 0, BF16)
    v2d = _glu(tr, h2d, p["conv_w_pw1"], p["conv_b_pw1"])
    if cache is None:
        conv2d = _conv_long(tr, v2d, p["conv_w_dw"], p["conv_b_dw"])
        new_cache = v2d.reshape(1, tr.n_seq, tr.seq_len, D_MODEL)[:, :, tr.seq_len - CONV_HIST:]
    else:
        conv3, new_cache = _conv_step(tr, v2d.reshape(tr.n_seq, tr.seq_len, D_MODEL), cache,
                                      p["conv_w_dw"], p["conv_b_dw"])
        conv2d = conv3.reshape(tokens, D_MODEL)
    x3 = _pw2(tr, conv2d, p["conv_ln_g"], p["conv_ln_b"], p["conv_w_pw2"], p["conv_b_pw2"], x3, mod, 1)
    x3 = _mlp(tr, x3, p["rms_g_mlp"], p["mlp_w1"], p["mlp_w2"], mod, 1)
    y3 = _final_norm(tr, x3, p["final_g"])

    state_shape = (1, tr.n_seq, SSM_GROUPS, SSM_STATE)
    return (y3.reshape(tr.n_seq, tr.seq_len, D_MODEL), s_re.reshape(state_shape), s_im.reshape(state_shape),
            new_cache)


def kernel(x_prompt, x_sample, state_ssm_re, state_ssm_im, cache_conv, c_prompt, c_sample, rms_g_mix, rms_g_mlp, w_ada, b_ada, ssm_a_re, ssm_a_im, ssm_log_dt, ssm_b_re, ssm_b_im, ssm_c_re, ssm_c_im, ssm_d, ssm_w_glu, ssm_b_glu, conv_w_pw1, conv_b_pw1, conv_w_dw, conv_b_dw, conv_ln_g, conv_ln_b, conv_w_pw2, conv_b_pw2, mlp_w1, mlp_w2, final_g):
    bp, lp, _ = x_prompt.shape
    bs, ls, _ = x_sample.shape
    assert w_ada.shape[0] == 2 and ssm_a_re.shape[0] == 1 and conv_w_dw.shape[0] == 1
    prompt = Trunk(bp, lp, 1024)
    sample = Trunk(bs, ls, bs * ls)

    depth = w_ada.shape[0]
    p = dict(rms_g_mix=rms_g_mix.reshape(depth, 1, D_MODEL), rms_g_mlp=rms_g_mlp.reshape(depth, 1, D_MODEL),
             ssm_d=ssm_d, ssm_w_glu=ssm_w_glu, ssm_b_glu=ssm_b_glu,
             conv_w_pw1=conv_w_pw1, conv_b_pw1=conv_b_pw1, conv_w_dw=conv_w_dw, conv_b_dw=conv_b_dw,
             conv_ln_g=conv_ln_g, conv_ln_b=conv_ln_b, conv_w_pw2=conv_w_pw2, conv_b_pw2=conv_b_pw2,
             mlp_w1=mlp_w1, mlp_w2=mlp_w2, final_g=final_g.reshape(1, D_MODEL))

    n_c = bp + bs
    pad_rows = -n_c % SUBLANES
    c_all = jnp.concatenate([c_prompt, c_sample, jnp.zeros((pad_rows, D_MODEL), F32)], axis=0)
    mod = _ada(c_all, w_ada, b_ada)
    mod_p = mod[:, :bp].reshape(2, bp, 1, 6 * D_MODEL)
    mod_s = mod[:, bp:n_c].reshape(2, bs, 1, 6 * D_MODEL)

    tabs = _s5_prep(ssm_a_re[0], ssm_a_im[0], ssm_log_dt[0], ssm_b_re[0], ssm_b_im[0],
                    ssm_c_re[0], ssm_c_im[0], lp // SUBLANES)

    n_state = SSM_GROUPS * SSM_STATE
    h0 = (state_ssm_re.reshape(bs, n_state), state_ssm_im.reshape(bs, n_state))
    y_p, p_re, p_im, p_buf = _trunk(prompt, x_prompt, mod_p, None, None, tabs, p)
    y_s, s_re, s_im, s_buf = _trunk(sample, x_sample, mod_s, h0, cache_conv, tabs, p)
    return (y_p, y_s, p_re, p_im, p_buf, s_re, s_im, s_buf)
```

```python
import collections
import functools

import jax
import jax.numpy as jnp
from jax import lax
from jax.experimental import pallas as pl
from jax.experimental.pallas import tpu as pltpu

F32 = jnp.float32
BF16 = jnp.bfloat16

D_MODEL = 2048
D_FF = 4 * D_MODEL
SSM_GROUP = 16
SSM_GROUPS = D_MODEL // SSM_GROUP
SSM_STATE = 64
LOG2_GROUP = SSM_GROUP.bit_length() - 1
LOG2_STATE = SSM_STATE.bit_length() - 1
assert SSM_GROUP == 1 << LOG2_GROUP and SSM_STATE == 1 << LOG2_STATE
CONV_WIDTH = 31
CONV_HIST = CONV_WIDTH - 1
RMS_EPS = 1e-6
LN_EPS = 1e-5

LANES = 128
SUBLANES = 8
VMEM_LIMIT_BYTES = 56 * 1024 * 1024

GROUPS_PER_TILE = LANES // SSM_GROUP
STATE_TILE = GROUPS_PER_TILE * SSM_STATE
N_LANE_TILES = D_MODEL // LANES
HALO = 32
HIST_OFF = HALO - CONV_HIST

Trunk = collections.namedtuple("Trunk", "n_seq seq_len tm mod_row")


def _params(*sem):
    return pltpu.CompilerParams(dimension_semantics=sem, vmem_limit_bytes=VMEM_LIMIT_BYTES)


def _dot(a, b):
    return jnp.dot(a, b, preferred_element_type=F32)


def _norm_mod(x3, g, sc, sh):
    ms = jnp.mean(x3 * x3, axis=-1, keepdims=True)
    return (x3 * lax.rsqrt(ms + RMS_EPS) * g) * (1.0 + sc) + sh


def _mod_spec(tr, layer, part, tn, ti, tj):
    nblk = D_MODEL // tn
    if tr.seq_len >= tr.tm:
        per = tr.seq_len // tr.tm
        return pl.BlockSpec((None, 1, 1, tn),
                            lambda *g: (layer, tr.mod_row + ti(*g) // per, 0, part * nblk + tj(*g)))
    nbm = tr.tm // SUBLANES
    assert tr.seq_len == SUBLANES and tr.mod_row % nbm == 0
    return pl.BlockSpec((None, nbm, 1, tn),
                        lambda *g: (layer, tr.mod_row // nbm + ti(*g), 0, part * nblk + tj(*g)))


def _ada_kernel(c_ref, w_ref, b_ref, o_ref):
    ca = jax.nn.silu(c_ref[...]).astype(BF16)
    mod = _dot(ca, w_ref[...].astype(BF16)) + b_ref[...]
    for r in range(o_ref.shape[0]):
        o_ref[r] = mod[r:r + 1, :]


def _ada(c_all, w_ada, b_ada):
    depth, d, n = w_ada.shape
    rows = c_all.shape[0]
    tn = 1024
    return pl.pallas_call(
        _ada_kernel,
        grid=(depth, n // tn),
        in_specs=[pl.BlockSpec((rows, d), lambda l, j: (0, 0)),
                  pl.BlockSpec((None, d, tn), lambda l, j: (l, 0, j)),
                  pl.BlockSpec((None, 1, tn), lambda l, j: (l, 0, j))],
        out_specs=pl.BlockSpec((None, rows, 1, tn), lambda l, j: (l, 0, 0, j)),
        out_shape=jax.ShapeDtypeStruct((depth, rows, 1, n), F32),
        compiler_params=_params("parallel", "parallel"),
        name="ada",
    )(c_all, w_ada, b_ada.reshape(depth, 1, n))


def _prenorm_kernel(x_ref, g_ref, sc_ref, sh_ref, o_ref):
    h = _norm_mod(x_ref[...], g_ref[...], sc_ref[...], sh_ref[...])
    o_ref[...] = h.reshape(o_ref.shape).astype(o_ref.dtype)


def _prenorm(tr, x3, g, mod, layer, sublayer, dtype):
    tm = min(tr.tm, 512)
    trp = tr._replace(tm=tm)
    nb = tm // SUBLANES
    ti, tj = (lambda i: i), (lambda i: 0)
    return pl.pallas_call(
        _prenorm_kernel,
        grid=(x3.shape[0] // nb,),
        in_specs=[pl.BlockSpec((nb, SUBLANES, D_MODEL), lambda i: (i, 0, 0)),
                  pl.BlockSpec((None, 1, D_MODEL), lambda i: (layer, 0, 0)),
                  _mod_spec(trp, layer, 3 * sublayer + 1, D_MODEL, ti, tj),
                  _mod_spec(trp, layer, 3 * sublayer, D_MODEL, ti, tj)],
        out_specs=pl.BlockSpec((tm, D_MODEL), lambda i: (i, 0)),
        out_shape=jax.ShapeDtypeStruct((x3.shape[0] * SUBLANES, D_MODEL), dtype),
        compiler_params=_params("parallel"),
        name="prenorm",
    )(x3, g, mod, mod)


def _s5_prep_kernel(lre_ref, lim_ref, ldt_ref, bre_ref, bim_ref, cre_ref, cim_ref,
                    are_ref, aim_ref, bbre_ref, bbim_ref, ccre_ref, ccim_ref, pre_ref, pim_ref, *, n_pow):
    lr, li = lre_ref[...], lim_ref[...]
    dt = jnp.exp(ldt_ref[...])
    mag = jnp.exp(lr * dt)
    are = mag * jnp.cos(li * dt)
    aim = mag * jnp.sin(li * dt)
    er, ei = are - 1.0, aim
    den = lr * lr + li * li
    qre = (er * lr + ei * li) / den
    qim = (ei * lr - er * li) / den
    are_ref[...] = are
    aim_ref[...] = aim

    br, bi = bre_ref[...], bim_ref[...]
    keep = (jnp.right_shift(lax.broadcasted_iota(jnp.int32, br.shape, 0), LOG2_GROUP)
            == jnp.right_shift(lax.broadcasted_iota(jnp.int32, br.shape, 1), LOG2_STATE))
    bbre_ref[...] = jnp.where(keep, qre * br - qim * bi, 0.0).astype(BF16)
    bbim_ref[...] = jnp.where(keep, qre * bi + qim * br, 0.0).astype(BF16)

    cr, ci = cre_ref[...], cim_ref[...]
    keep = (jnp.right_shift(lax.broadcasted_iota(jnp.int32, cr.shape, 0), LOG2_STATE)
            == jnp.right_shift(lax.broadcasted_iota(jnp.int32, cr.shape, 1), LOG2_GROUP))
    ccre_ref[...] = jnp.where(keep, cr, 0.0).astype(BF16)
    ccim_ref[...] = jnp.where(keep, -ci, 0.0).astype(BF16)

    pre_ref[0:1, :] = are
    pim_ref[0:1, :] = aim

    def step(t, carry):
        pr, pi = carry
        nr = pr * are - pi * aim
        ni = pr * aim + pi * are
        pre_ref[pl.ds(t, 1), :] = nr
        pim_ref[pl.ds(t, 1), :] = ni
        return nr, ni

    lax.fori_loop(1, n_pow, step, (are, aim))


def _s5_prep(a_re, a_im, log_dt, b_re, b_im, c_re, c_im, n_pow):
    nt = N_LANE_TILES
    tile3 = lambda a: a.reshape(nt, 1, STATE_TILE)
    ldt = jnp.broadcast_to(log_dt[:, None], (SSM_GROUPS, SSM_STATE))
    b_rows = lambda b: jnp.tile(b.transpose(0, 2, 1).reshape(D_MODEL, SSM_STATE), (1, GROUPS_PER_TILE))
    c_rows = lambda c: jnp.tile(c.transpose(0, 2, 1).reshape(SSM_GROUPS * SSM_STATE, SSM_GROUP),
                                (1, GROUPS_PER_TILE))
    vec = pl.BlockSpec((None, 1, STATE_TILE), lambda k: (k, 0, 0))
    bspec = pl.BlockSpec((LANES, STATE_TILE), lambda k: (k, 0))
    cspec = pl.BlockSpec((STATE_TILE, LANES), lambda k: (k, 0))
    pspec = pl.BlockSpec((None, n_pow, STATE_TILE), lambda k: (k, 0, 0))
    return pl.pallas_call(
        functools.partial(_s5_prep_kernel, n_pow=n_pow),
        grid=(nt,),
        in_specs=[vec, vec, vec, bspec, bspec, cspec, cspec],
        out_specs=[vec, vec, bspec, bspec, cspec, cspec, pspec, pspec],
        out_shape=[jax.ShapeDtypeStruct((nt, 1, STATE_TILE), F32)] * 2
        + [jax.ShapeDtypeStruct((D_MODEL, STATE_TILE), BF16)] * 2
        + [jax.ShapeDtypeStruct((SSM_GROUPS * SSM_STATE, LANES), BF16)] * 2
        + [jax.ShapeDtypeStruct((nt, n_pow, STATE_TILE), F32)] * 2,
        compiler_params=_params("parallel"),
        name="s5_prep",
    )(tile3(a_re), tile3(a_im), tile3(ldt), b_rows(b_re), b_rows(b_im), c_rows(c_re), c_rows(c_im))


S5_CHUNK = 256


def _s5_kernel(*refs, n_blocks, seg_len, chained):
    (u_ref, d_ref, bbre_ref, bbim_ref, ccre_ref, ccim_ref, are_ref, aim_ref, x0_ref, x1_ref,
     g_ref, sre_ref, sim_ref, up, hre, him, gp, gn) = refs
    n_groups = n_blocks * seg_len
    rows = n_groups * SUBLANES

    def grp(i):
        return pl.ds(pl.multiple_of(i * SUBLANES, SUBLANES), SUBLANES)

    def natural(lane):
        start = lane * seg_len
        return pl.ds(start if isinstance(lane, int) else pl.multiple_of(start, SUBLANES), seg_len)

    def regrouped(lane):
        return pl.ds((lane // SUBLANES) * (seg_len * SUBLANES) + lane % SUBLANES, seg_len, stride=SUBLANES)

    def for_each_lane(body):
        if n_blocks == 1:
            for lane in range(SUBLANES):
                body(lane, 0)
        else:
            lax.fori_loop(0, n_blocks * SUBLANES, body, 0, unroll=8)

    def regroup(lane, c):
        up[regrouped(lane), :] = u_ref[natural(lane), :]
        return c

    for_each_lane(regroup)

    for c in range(rows // S5_CHUNK):
        sl = slice(c * S5_CHUNK, (c + 1) * S5_CHUNK)
        ub = up[sl, :].astype(BF16)
        hre[sl, :] = _dot(ub, bbre_ref[...])
        him[sl, :] = _dot(ub, bbim_ref[...])

    ar = jnp.broadcast_to(are_ref[...], (SUBLANES, STATE_TILE))
    ai = jnp.broadcast_to(aim_ref[...], (SUBLANES, STATE_TILE))

    def scan_block(nb, c):
        if chained:
            h0 = (jnp.zeros((SUBLANES, STATE_TILE), F32),) * 2
        else:
            h0 = (x0_ref[grp(nb), :], x1_ref[grp(nb), :])

        def step(t, carry):
            hr, hi = carry
            r = grp(nb * seg_len + t)
            nr = ar * hr - ai * hi + hre[r, :]
            ni = ar * hi + ai * hr + him[r, :]
            hre[r, :] = nr
            him[r, :] = ni
            return nr, ni

        hr, hi = lax.fori_loop(0, seg_len, step, h0, unroll=8)
        if not chained:
            sre_ref[grp(nb), :] = hr
            sim_ref[grp(nb), :] = hi
        return c

    lax.fori_loop(0, n_blocks, scan_block, 0)

    if chained:
        er, ei = hre[rows - SUBLANES:rows, :], him[rows - SUBLANES:rows, :]
        pr, pi = x0_ref[seg_len - 1:seg_len, :], x1_ref[seg_len - 1:seg_len, :]
        row = lax.broadcasted_iota(jnp.int32, (SUBLANES, STATE_TILE), 0)
        xr = jnp.zeros((SUBLANES, STATE_TILE), F32)
        xi = xr
        for _ in range(SUBLANES - 1):
            yr = er + pr * xr - pi * xi
            yi = ei + pr * xi + pi * xr
            xr = jnp.where(row == 0, 0.0, pltpu.roll(yr, 1, 0))
            xi = jnp.where(row == 0, 0.0, pltpu.roll(yi, 1, 0))

        def fixup(t, c):
            r = grp(t)
            pr, pi = x0_ref[pl.ds(t, 1), :], x1_ref[pl.ds(t, 1), :]
            hre[r, :] = hre[r, :] + (pr * xr - pi * xi)
            him[r, :] = him[r, :] + (pr * xi + pi * xr)
            return c

        lax.fori_loop(0, seg_len, fixup, 0, unroll=8)
        sre_ref[...] = hre[rows - 1:rows, :]
        sim_ref[...] = him[rows - 1:rows, :]

    d = d_ref[...]
    for c in range(rows // S5_CHUNK):
        sl = slice(c * S5_CHUNK, (c + 1) * S5_CHUNK)
        y = _dot(hre[sl, :].astype(BF16), ccre_ref[...]) + _dot(him[sl, :].astype(BF16), ccim_ref[...])
        gp[sl, :] = jax.nn.gelu(y + d * up[sl, :])

    def ungroup(lane, c):
        gn[natural(lane), :] = gp[regrouped(lane), :]
        return c

    for_each_lane(ungroup)
    g_ref[...] = gn[...].astype(BF16)


def _s5(tr, u2d, d_skip, tabs, h0):
    are, aim, bbre, bbim, ccre, ccim, pre, pim = tabs
    chained = h0 is None
    if chained:
        n_batch, rows = tr.n_seq, tr.seq_len
        n_blocks, seg_len = 1, tr.seq_len // SUBLANES
        assert pre.shape[1] == seg_len
        x_specs = [pl.BlockSpec((None, seg_len, STATE_TILE), lambda k, b: (k, 0, 0))] * 2
        x_args = (pre, pim)
        st_spec = pl.BlockSpec((None, 1, STATE_TILE), lambda k, b: (b, 0, k))
        st_shape = jax.ShapeDtypeStruct((tr.n_seq, 1, SSM_GROUPS * SSM_STATE), F32)
    else:
        n_batch, rows = 1, tr.n_seq * tr.seq_len
        n_blocks, seg_len = tr.n_seq // SUBLANES, tr.seq_len
        x_specs = [pl.BlockSpec((tr.n_seq, STATE_TILE), lambda k, b: (0, k))] * 2
        x_args = h0
        st_spec = pl.BlockSpec((tr.n_seq, STATE_TILE), lambda k, b: (0, k))
        st_shape = jax.ShapeDtypeStruct((tr.n_seq, SSM_GROUPS * SSM_STATE), F32)
    assert rows % S5_CHUNK == 0
    vec = pl.BlockSpec((None, 1, STATE_TILE), lambda k, b: (k, 0, 0))
    bspec = pl.BlockSpec((LANES, STATE_TILE), lambda k, b: (k, 0))
    cspec = pl.BlockSpec((STATE_TILE, LANES), lambda k, b: (k, 0))
    tok = pl.BlockSpec((rows, LANES), lambda k, b: (b, k))
    return pl.pallas_call(
        functools.partial(_s5_kernel, n_blocks=n_blocks, seg_len=seg_len, chained=chained),
        grid=(N_LANE_TILES, n_batch),
        in_specs=[tok, pl.BlockSpec((1, LANES), lambda k, b: (0, k)), bspec, bspec, cspec, cspec, vec, vec]
        + x_specs,
        out_specs=[tok, st_spec, st_spec],
        out_shape=[jax.ShapeDtypeStruct(u2d.shape, BF16), st_shape, st_shape],
        scratch_shapes=[pltpu.VMEM((rows, LANES), F32),
                        pltpu.VMEM((rows, STATE_TILE), F32),
                        pltpu.VMEM((rows, STATE_TILE), F32),
                        pltpu.VMEM((rows, LANES), F32),
                        pltpu.VMEM((rows, LANES), F32)],
        compiler_params=_params("parallel", "parallel"),
        name="s5",
    )(u2d, d_skip, bbre, bbim, ccre, ccim, are, aim, *x_args)


def _glu_kernel(*refs, residual):
    if residual:
        a_ref, wa_ref, wb_ref, ba_ref, bb_ref, x_ref, gate_ref, o_ref = refs
    else:
        a_ref, wa_ref, wb_ref, ba_ref, bb_ref, o_ref = refs
    a = a_ref[...]
    za = _dot(a, wa_ref[...].astype(BF16)) + ba_ref[...]
    zb = _dot(a, wb_ref[...].astype(BF16)) + bb_ref[...]
    out = za * jax.nn.sigmoid(zb)
    if residual:
        out = x_ref[...] + gate_ref[...] * out.reshape(o_ref.shape)
    o_ref[...] = out


def _glu(tr, a2d, w, b, x3=None, mod=None, layer=None):
    tm, tn = tr.tm, 512
    nb, nj = tm // SUBLANES, D_MODEL // tn
    b3 = b.reshape(1, 1, 2 * D_MODEL)
    residual = x3 is not None
    in_specs = [pl.BlockSpec((tm, D_MODEL), lambda j, i: (i, 0)),
                pl.BlockSpec((None, D_MODEL, tn), lambda j, i: (0, 0, j)),
                pl.BlockSpec((None, D_MODEL, tn), lambda j, i: (0, 0, j + nj)),
                pl.BlockSpec((None, 1, tn), lambda j, i: (0, 0, j)),
                pl.BlockSpec((None, 1, tn), lambda j, i: (0, 0, j + nj))]
    args = (a2d, w, w, b3, b3)
    if residual:
        out_spec = pl.BlockSpec((nb, SUBLANES, tn), lambda j, i: (i, 0, j))
        out_shape = jax.ShapeDtypeStruct(x3.shape, F32)
        in_specs += [out_spec, _mod_spec(tr, layer, 2, tn, lambda j, i: i, lambda j, i: j)]
        args += (x3, mod)
    else:
        out_spec = pl.BlockSpec((tm, tn), lambda j, i: (i, j))
        out_shape = jax.ShapeDtypeStruct(a2d.shape, F32)
    return pl.pallas_call(
        functools.partial(_glu_kernel, residual=residual),
        grid=(nj, a2d.shape[0] // tm),
        in_specs=in_specs,
        out_specs=out_spec,
        out_shape=out_shape,
        compiler_params=_params("parallel", "parallel"),
        name="glu",
    )(*args)


CONV_CHUNK = SUBLANES * SUBLANES
CONV_ROWS = 256


def _conv_taps(win, w_ref, b_ref, ls):
    w = [jnp.broadcast_to(w_ref[k:k + 1, ls], (SUBLANES, LANES)) for k in range(CONV_WIDTH)]
    acc = [jnp.broadcast_to(b_ref[:, ls], (SUBLANES, LANES))] * SUBLANES
    for o in range(CONV_WIDTH + SUBLANES - 1):
        x = win(o)
        for r in range(SUBLANES):
            if 0 <= o - r < CONV_WIDTH:
                acc[r] = acc[r] + w[o - r] * x
    return acc


def _ln_silu_store(cbuf, lg_ref, lb_ref, o_ref):
    n = cbuf.shape[0]
    inv_d = 1.0 / (n * LANES)
    tot = cbuf[0]
    for l in range(1, n):
        tot = tot + cbuf[l]
    mean = jnp.sum(tot, axis=-1, keepdims=True) * inv_d
    sq = jnp.zeros_like(tot)
    for l in range(n):
        xc = cbuf[l] - mean
        sq = sq + xc * xc
    rstd = lax.rsqrt(jnp.sum(sq, axis=-1, keepdims=True) * inv_d + LN_EPS)
    for l in range(n):
        ls = slice(l * LANES, (l + 1) * LANES)
        y = (cbuf[l] - mean) * rstd * lg_ref[:, ls] + lb_ref[:, ls]
        o_ref[:, ls] = jax.nn.silu(y).astype(BF16)


def _conv_ln_kernel(v_ref, halo_ref, w_ref, b_ref, lg_ref, lb_ref, o_ref, pad, cbuf, *, tiles_per_seq):
    rows = v_ref.shape[0]
    first = (pl.program_id(0) % tiles_per_seq) == 0
    for l in range(N_LANE_TILES):
        ls = slice(l * LANES, (l + 1) * LANES)
        pad[l, 0:HALO, :] = jnp.where(first, 0.0, halo_ref[:, ls])
        pad[l, HALO:, :] = v_ref[:, ls]

        def chunk(c, carry, l=l, ls=ls):
            base = pl.multiple_of(c * CONV_CHUNK, CONV_CHUNK)
            acc = _conv_taps(lambda o: pad[l, pl.ds(base + HIST_OFF + o, SUBLANES, stride=SUBLANES), :],
                             w_ref, b_ref, ls)
            for r in range(SUBLANES):
                cbuf[l, pl.ds(base + r, SUBLANES, stride=SUBLANES), :] = acc[r]
            return carry

        lax.fori_loop(0, rows // CONV_CHUNK, chunk, 0)
    _ln_silu_store(cbuf, lg_ref, lb_ref, o_ref)


def _conv_ln_long(tr, v2d, w, b, ln_g, ln_b):
    rows = CONV_ROWS
    assert tr.seq_len % rows == 0 and rows % CONV_CHUNK == 0 and rows % HALO == 0
    hb = rows // HALO
    row = pl.BlockSpec((1, D_MODEL), lambda i: (0, 0))
    return pl.pallas_call(
        functools.partial(_conv_ln_kernel, tiles_per_seq=tr.seq_len // rows),
        grid=(v2d.shape[0] // rows,),
        in_specs=[pl.BlockSpec((rows, D_MODEL), lambda i: (i, 0)),
                  pl.BlockSpec((HALO, D_MODEL), lambda i: (jnp.maximum(i * hb - 1, 0), 0)),
                  pl.BlockSpec((None, CONV_WIDTH, D_MODEL), lambda i: (0, 0, 0)),
                  row, row, row],
        out_specs=pl.BlockSpec((rows, D_MODEL), lambda i: (i, 0)),
        out_shape=jax.ShapeDtypeStruct(v2d.shape, BF16),
        scratch_shapes=[pltpu.VMEM((N_LANE_TILES, HALO + rows, LANES), F32),
                        pltpu.VMEM((N_LANE_TILES, rows, LANES), F32)],
        compiler_params=_params("parallel"),
        name="conv_ln_long",
    )(v2d, v2d, w, b, ln_g, ln_b)


def _conv_ln_step_kernel(v_ref, cache_ref, w_ref, b_ref, lg_ref, lb_ref, o_ref, nc_ref, vs, cbuf):
    for l in range(N_LANE_TILES):
        ls = slice(l * LANES, (l + 1) * LANES)
        vs[l] = v_ref[:, ls]
        new = [vs[l, pl.ds(t, SUBLANES, stride=SUBLANES), :] for t in range(SUBLANES)]

        def padded(o, ls=ls, new=new):
            return cache_ref[o, :, ls] if o < CONV_HIST else new[o - CONV_HIST]

        acc = _conv_taps(padded, w_ref, b_ref, ls)
        for t in range(SUBLANES):
            cbuf[l, pl.ds(t, SUBLANES, stride=SUBLANES), :] = acc[t]
        for q in range(CONV_HIST):
            nc_ref[q, :, ls] = padded(q + SUBLANES)
    _ln_silu_store(cbuf, lg_ref, lb_ref, o_ref)


def _conv_ln_step(tr, v2d, cache_t, w, b, ln_g, ln_b):
    assert tr.seq_len == SUBLANES
    rows = SUBLANES * tr.seq_len
    row = pl.BlockSpec((1, D_MODEL), lambda s: (0, 0))
    cspec = pl.BlockSpec((CONV_HIST, SUBLANES, D_MODEL), lambda s: (0, s, 0))
    return pl.pallas_call(
        _conv_ln_step_kernel,
        grid=(tr.n_seq // SUBLANES,),
        in_specs=[pl.BlockSpec((rows, D_MODEL), lambda s: (s, 0)), cspec,
                  pl.BlockSpec((None, CONV_WIDTH, D_MODEL), lambda s: (0, 0, 0)),
                  row, row, row],
        out_specs=[pl.BlockSpec((rows, D_MODEL), lambda s: (s, 0)), cspec],
        out_shape=[jax.ShapeDtypeStruct(v2d.shape, BF16), jax.ShapeDtypeStruct(cache_t.shape, F32)],
        scratch_shapes=[pltpu.VMEM((N_LANE_TILES, rows, LANES), F32),
                        pltpu.VMEM((N_LANE_TILES, rows, LANES), F32)],
        compiler_params=_params("parallel"),
        name="conv_ln_step",
    )(v2d, cache_t, w, b, ln_g, ln_b)


MLP_TILE = 1024
MLP_TK = 2048


def _mlp_up_kernel(h_ref, w_ref, o_ref):
    a = jnp.maximum(_dot(h_ref[...], w_ref[...].astype(BF16)), 0.0)
    o_ref[...] = (a * a).astype(BF16)


def _mlp_up(tr, h2d, w1, layer):
    tm, tn = tr.tm, MLP_TILE
    return pl.pallas_call(
        _mlp_up_kernel,
        grid=(D_FF // tn, h2d.shape[0] // tm),
        in_specs=[pl.BlockSpec((tm, D_MODEL), lambda j, i: (i, 0)),
                  pl.BlockSpec((None, D_MODEL, tn), lambda j, i: (layer, 0, j))],
        out_specs=pl.BlockSpec((tm, tn), lambda j, i: (i, j)),
        out_shape=jax.ShapeDtypeStruct((h2d.shape[0], D_FF), BF16),
        compiler_params=_params("parallel", "parallel"),
        name="mlp_up",
    )(h2d, w1)


def _mm_res_kernel(*refs, has_bias):
    if has_bias:
        a_ref, w_ref, b_ref, x_ref, gate_ref, o_ref = refs
    else:
        a_ref, w_ref, x_ref, gate_ref, o_ref = refs
    k = pl.program_id(2)

    @pl.when(k == 0)
    def _():
        o_ref[...] = jnp.zeros(o_ref.shape, F32)

    o_ref[...] += _dot(a_ref[...], w_ref[...].astype(BF16)).reshape(o_ref.shape)

    @pl.when(k == pl.num_programs(2) - 1)
    def _():
        out = o_ref[...]
        if has_bias:
            out = out + b_ref[...]
        o_ref[...] = x_ref[...] + gate_ref[...] * out


def _mm_res(tr, a2d, w, w_idx, x3, mod, layer, part, bias=None):
    tm, tn = tr.tm, MLP_TILE
    kdim = a2d.shape[1]
    tk = min(kdim, MLP_TK)
    nb = tm // SUBLANES
    xspec = pl.BlockSpec((nb, SUBLANES, tn), lambda i, j, k: (i, 0, j))
    has_bias = bias is not None
    bias_specs = [pl.BlockSpec((1, tn), lambda i, j, k: (0, j))] if has_bias else []
    bias_args = (bias,) if has_bias else ()
    return pl.pallas_call(
        functools.partial(_mm_res_kernel, has_bias=has_bias),
        grid=(a2d.shape[0] // tm, D_MODEL // tn, kdim // tk),
        in_specs=[pl.BlockSpec((tm, tk), lambda i, j, k: (i, k)),
                  pl.BlockSpec((None, tk, tn), lambda i, j, k: (w_idx, k, j))] + bias_specs
        + [xspec, _mod_spec(tr, layer, part, tn, lambda i, j, k: i, lambda i, j, k: j)],
        out_specs=xspec,
        out_shape=jax.ShapeDtypeStruct(x3.shape, F32),
        compiler_params=_params("parallel", "parallel", "arbitrary"),
        name="mm_res",
    )(a2d, w, *bias_args, x3, mod)


def _final_norm_kernel(x_ref, g_ref, o_ref):
    x = x_ref[...]
    ms = jnp.mean(x * x, axis=-1, keepdims=True)
    o_ref[...] = x * lax.rsqrt(ms + RMS_EPS) * g_ref[...]


def _final_norm(tr, x3, g):
    nb = min(tr.tm, 512) // SUBLANES
    spec = pl.BlockSpec((nb, SUBLANES, D_MODEL), lambda i: (i, 0, 0))
    return pl.pallas_call(
        _final_norm_kernel,
        grid=(x3.shape[0] // nb,),
        in_specs=[spec, pl.BlockSpec((1, D_MODEL), lambda i: (0, 0))],
        out_specs=spec,
        out_shape=jax.ShapeDtypeStruct(x3.shape, F32),
        compiler_params=_params("parallel"),
        name="final_norm",
    )(x3, g)


def _mlp(tr, x3, g, w1, w2, mod, layer):
    h2d = _prenorm(tr, x3, g, mod, layer, 1, BF16)
    return _mm_res(tr, _mlp_up(tr, h2d, w1, layer), w2, layer, x3, mod, layer, 5)


def _trunk(tr, x, mod, h0, cache, tabs, p):
    tokens = tr.n_seq * tr.seq_len
    x3 = x.reshape(tokens // SUBLANES, SUBLANES, D_MODEL)

    u2d = _prenorm(tr, x3, p["rms_g_mix"], mod, 0, 0, F32)
    g2d, s_re, s_im = _s5(tr, u2d, p["ssm_d"], tabs, h0)
    x3 = _glu(tr, g2d, p["ssm_w_glu"], p["ssm_b_glu"], x3, mod, 0)
    x3 = _mlp(tr, x3, p["rms_g_mlp"], p["mlp_w1"], p["mlp_w2"], mod, 0)

    h2d = _prenorm(tr, x3, p["rms_g_mix"], mod, 1, 0, BF16)
    v2d = _glu(tr, h2d, p["conv_w_pw1"], p["conv_b_pw1"])
    conv_args = (p["conv_w_dw"], p["conv_b_dw"], p["conv_ln_g"], p["conv_ln_b"])
    if cache is None:
        hc2d = _conv_ln_long(tr, v2d, *conv_args)
        new_cache = v2d.reshape(1, tr.n_seq, tr.seq_len, D_MODEL)[:, :, tr.seq_len - CONV_HIST:]
    else:
        hc2d, cache_t = _conv_ln_step(tr, v2d, jnp.transpose(cache[0], (1, 0, 2)), *conv_args)
        new_cache = jnp.transpose(cache_t, (1, 0, 2))[None]
    x3 = _mm_res(tr, hc2d, p["conv_w_pw2"], 0, x3, mod, 1, 2, bias=p["conv_b_pw2"])
    x3 = _mlp(tr, x3, p["rms_g_mlp"], p["mlp_w1"], p["mlp_w2"], mod, 1)
    y3 = _final_norm(tr, x3, p["final_g"])

    state_shape = (1, tr.n_seq, SSM_GROUPS, SSM_STATE)
    return (y3.reshape(tr.n_seq, tr.seq_len, D_MODEL), s_re.reshape(state_shape), s_im.reshape(state_shape),
            new_cache)


def kernel(x_prompt, x_sample, state_ssm_re, state_ssm_im, cache_conv, c_prompt, c_sample, rms_g_mix, rms_g_mlp, w_ada, b_ada, ssm_a_re, ssm_a_im, ssm_log_dt, ssm_b_re, ssm_b_im, ssm_c_re, ssm_c_im, ssm_d, ssm_w_glu, ssm_b_glu, conv_w_pw1, conv_b_pw1, conv_w_dw, conv_b_dw, conv_ln_g, conv_ln_b, conv_w_pw2, conv_b_pw2, mlp_w1, mlp_w2, final_g):
    bp, lp, _ = x_prompt.shape
    bs, ls, _ = x_sample.shape
    assert w_ada.shape[0] == 2 and ssm_a_re.shape[0] == 1 and conv_w_dw.shape[0] == 1
    prompt = Trunk(bp, lp, 1024, mod_row=bs)
    sample = Trunk(bs, ls, bs * ls, mod_row=0)

    depth = w_ada.shape[0]
    p = dict(rms_g_mix=rms_g_mix.reshape(depth, 1, D_MODEL), rms_g_mlp=rms_g_mlp.reshape(depth, 1, D_MODEL),
             ssm_d=ssm_d, ssm_w_glu=ssm_w_glu, ssm_b_glu=ssm_b_glu,
             conv_w_pw1=conv_w_pw1, conv_b_pw1=conv_b_pw1, conv_w_dw=conv_w_dw, conv_b_dw=conv_b_dw,
             conv_ln_g=conv_ln_g, conv_ln_b=conv_ln_b, conv_w_pw2=conv_w_pw2, conv_b_pw2=conv_b_pw2,
             mlp_w1=mlp_w1, mlp_w2=mlp_w2, final_g=final_g.reshape(1, D_MODEL))

    n_c = bp + bs
    pad_rows = -n_c % SUBLANES
    c_all = jnp.concatenate([c_sample, c_prompt, jnp.zeros((pad_rows, D_MODEL), F32)], axis=0)
    mod = _ada(c_all, w_ada, b_ada)

    tabs = _s5_prep(ssm_a_re[0], ssm_a_im[0], ssm_log_dt[0], ssm_b_re[0], ssm_b_im[0],
                    ssm_c_re[0], ssm_c_im[0], lp // SUBLANES)

    n_state = SSM_GROUPS * SSM_STATE
    h0 = (state_ssm_re.reshape(bs, n_state), state_ssm_im.reshape(bs, n_state))
    y_p, p_re, p_im, p_buf = _trunk(prompt, x_prompt, mod, None, None, tabs, p)
    y_s, s_re, s_im, s_buf = _trunk(sample, x_sample, mod, h0, cache_conv, tabs, p)
    return (y_p, y_s, p_re, p_im, p_buf, s_re, s_im, s_buf)
```

```python
import collections
import functools

import jax
import jax.numpy as jnp
from jax import lax
from jax.experimental import pallas as pl
from jax.experimental.pallas import tpu as pltpu

F32 = jnp.float32
BF16 = jnp.bfloat16

D_MODEL = 2048
D_FF = 4 * D_MODEL
SSM_GROUP = 16
SSM_GROUPS = D_MODEL // SSM_GROUP
SSM_STATE = 64
LOG2_GROUP = SSM_GROUP.bit_length() - 1
LOG2_STATE = SSM_STATE.bit_length() - 1
assert SSM_GROUP == 1 << LOG2_GROUP and SSM_STATE == 1 << LOG2_STATE
CONV_WIDTH = 31
CONV_HIST = CONV_WIDTH - 1
RMS_EPS = 1e-6
LN_EPS = 1e-5

LANES = 128
SUBLANES = 8
VMEM_LIMIT_BYTES = 56 * 1024 * 1024

GROUPS_PER_TILE = LANES // SSM_GROUP
STATE_TILE = GROUPS_PER_TILE * SSM_STATE
N_LANE_TILES = D_MODEL // LANES
HALO = 32
HIST_OFF = HALO - CONV_HIST

Trunk = collections.namedtuple("Trunk", "n_seq seq_len tm mod_row")


def _params(*sem):
    return pltpu.CompilerParams(dimension_semantics=sem, vmem_limit_bytes=VMEM_LIMIT_BYTES)


def _dot(a, b):
    return jnp.dot(a, b, preferred_element_type=F32)


def _norm_mod(x3, g, sc, sh):
    ms = jnp.mean(x3 * x3, axis=-1, keepdims=True)
    return (x3 * lax.rsqrt(ms + RMS_EPS) * g) * (1.0 + sc) + sh


def _mod_spec(tr, layer, part, tn, ti, tj):
    nblk = D_MODEL // tn
    if tr.seq_len >= tr.tm:
        per = tr.seq_len // tr.tm
        return pl.BlockSpec((None, 1, 1, tn),
                            lambda *g: (layer, tr.mod_row + ti(*g) // per, 0, part * nblk + tj(*g)))
    nbm = tr.tm // SUBLANES
    assert tr.seq_len == SUBLANES and tr.mod_row % nbm == 0
    return pl.BlockSpec((None, nbm, 1, tn),
                        lambda *g: (layer, tr.mod_row // nbm + ti(*g), 0, part * nblk + tj(*g)))


def _ada_kernel(c_ref, w_ref, b_ref, o_ref):
    ca = jax.nn.silu(c_ref[...]).astype(BF16)
    mod = _dot(ca, w_ref[...].astype(BF16)) + b_ref[...]
    for r in range(o_ref.shape[0]):
        o_ref[r] = mod[r:r + 1, :]


def _ada(c_all, w_ada, b_ada):
    depth, d, n = w_ada.shape
    rows = c_all.shape[0]
    tn = 1024
    return pl.pallas_call(
        _ada_kernel,
        grid=(depth, n // tn),
        in_specs=[pl.BlockSpec((rows, d), lambda l, j: (0, 0)),
                  pl.BlockSpec((None, d, tn), lambda l, j: (l, 0, j)),
                  pl.BlockSpec((None, 1, tn), lambda l, j: (l, 0, j))],
        out_specs=pl.BlockSpec((None, rows, 1, tn), lambda l, j: (l, 0, 0, j)),
        out_shape=jax.ShapeDtypeStruct((depth, rows, 1, n), F32),
        compiler_params=_params("parallel", "parallel"),
        name="ada",
    )(c_all, w_ada, b_ada.reshape(depth, 1, n))


def _prenorm_kernel(x_ref, g_ref, sc_ref, sh_ref, o_ref):
    h = _norm_mod(x_ref[...], g_ref[...], sc_ref[...], sh_ref[...])
    o_ref[...] = h.reshape(o_ref.shape).astype(o_ref.dtype)


def _prenorm(tr, x3, g, mod, layer, sublayer, dtype):
    tm = min(tr.tm, 512)
    trp = tr._replace(tm=tm)
    nb = tm // SUBLANES
    ti, tj = (lambda i: i), (lambda i: 0)
    return pl.pallas_call(
        _prenorm_kernel,
        grid=(x3.shape[0] // nb,),
        in_specs=[pl.BlockSpec((nb, SUBLANES, D_MODEL), lambda i: (i, 0, 0)),
                  pl.BlockSpec((None, 1, D_MODEL), lambda i: (layer, 0, 0)),
                  _mod_spec(trp, layer, 3 * sublayer + 1, D_MODEL, ti, tj),
                  _mod_spec(trp, layer, 3 * sublayer, D_MODEL, ti, tj)],
        out_specs=pl.BlockSpec((tm, D_MODEL), lambda i: (i, 0)),
        out_shape=jax.ShapeDtypeStruct((x3.shape[0] * SUBLANES, D_MODEL), dtype),
        compiler_params=_params("parallel"),
        name="prenorm",
    )(x3, g, mod, mod)


def _s5_prep_kernel(lre_ref, lim_ref, ldt_ref, bre_ref, bim_ref, cre_ref, cim_ref,
                    are_ref, aim_ref, bbre_ref, bbim_ref, ccre_ref, ccim_ref):
    lr, li = lre_ref[...], lim_ref[...]
    dt = jnp.exp(ldt_ref[...])
    mag = jnp.exp(lr * dt)
    are = mag * jnp.cos(li * dt)
    aim = mag * jnp.sin(li * dt)
    er, ei = are - 1.0, aim
    den = lr * lr + li * li
    qre = (er * lr + ei * li) / den
    qim = (ei * lr - er * li) / den
    are_ref[...] = are
    aim_ref[...] = aim

    br, bi = bre_ref[...], bim_ref[...]
    keep = (jnp.right_shift(lax.broadcasted_iota(jnp.int32, br.shape, 0), LOG2_GROUP)
            == jnp.right_shift(lax.broadcasted_iota(jnp.int32, br.shape, 1), LOG2_STATE))
    bbre_ref[...] = jnp.where(keep, qre * br - qim * bi, 0.0).astype(BF16)
    bbim_ref[...] = jnp.where(keep, qre * bi + qim * br, 0.0).astype(BF16)

    cr, ci = cre_ref[...], cim_ref[...]
    keep = (jnp.right_shift(lax.broadcasted_iota(jnp.int32, cr.shape, 0), LOG2_STATE)
            == jnp.right_shift(lax.broadcasted_iota(jnp.int32, cr.shape, 1), LOG2_GROUP))
    ccre_ref[...] = jnp.where(keep, cr, 0.0).astype(BF16)
    ccim_ref[...] = jnp.where(keep, -ci, 0.0).astype(BF16)


def _s5_prep(a_re, a_im, log_dt, b_re, b_im, c_re, c_im):
    nt = N_LANE_TILES
    tile3 = lambda a: a.reshape(nt, 1, STATE_TILE)
    ldt = jnp.broadcast_to(log_dt[:, None], (SSM_GROUPS, SSM_STATE))
    b_rows = lambda b: jnp.tile(b.transpose(0, 2, 1).reshape(D_MODEL, SSM_STATE), (1, GROUPS_PER_TILE))
    c_rows = lambda c: jnp.tile(c.transpose(0, 2, 1).reshape(SSM_GROUPS * SSM_STATE, SSM_GROUP),
                                (1, GROUPS_PER_TILE))
    vec = pl.BlockSpec((None, 1, STATE_TILE), lambda k: (k, 0, 0))
    bspec = pl.BlockSpec((LANES, STATE_TILE), lambda k: (k, 0))
    cspec = pl.BlockSpec((STATE_TILE, LANES), lambda k: (k, 0))
    return pl.pallas_call(
        _s5_prep_kernel,
        grid=(nt,),
        in_specs=[vec, vec, vec, bspec, bspec, cspec, cspec],
        out_specs=[vec, vec, bspec, bspec, cspec, cspec],
        out_shape=[jax.ShapeDtypeStruct((nt, 1, STATE_TILE), F32)] * 2
        + [jax.ShapeDtypeStruct((D_MODEL, STATE_TILE), BF16)] * 2
        + [jax.ShapeDtypeStruct((SSM_GROUPS * SSM_STATE, LANES), BF16)] * 2,
        compiler_params=_params("parallel"),
        name="s5_prep",
    )(tile3(a_re), tile3(a_im), tile3(ldt), b_rows(b_re), b_rows(b_im), c_rows(c_re), c_rows(c_im))


S5_CHUNK = 512


def _s5_kernel(*refs, n_blocks, seg_len, chained):
    u_ref, d_ref, bbre_ref, bbim_ref, ccre_ref, ccim_ref, are_ref, aim_ref = refs[:8]
    refs = refs[8:]
    if not chained:
        x0_ref, x1_ref = refs[:2]
        refs = refs[2:]
    g_ref, sre_ref, sim_ref, up, gp, gn = refs[:6]
    rows = n_blocks * seg_len * SUBLANES
    n_chunks = rows // S5_CHUNK
    hre, him = refs[6:6 + n_chunks], refs[6 + n_chunks:]

    def natural(lane):
        start = lane * seg_len
        return pl.ds(start if isinstance(lane, int) else pl.multiple_of(start, SUBLANES), seg_len)

    def regrouped(lane):
        return pl.ds((lane // SUBLANES) * (seg_len * SUBLANES) + lane % SUBLANES, seg_len, stride=SUBLANES)

    def for_each_lane(body):
        if n_blocks == 1:
            for lane in range(SUBLANES):
                body(lane, 0)
        else:
            lax.fori_loop(0, n_blocks * SUBLANES, body, 0, unroll=8)

    def regroup(lane, c):
        up[regrouped(lane), :] = u_ref[natural(lane), :]
        return c

    for_each_lane(regroup)

    ar = jnp.broadcast_to(are_ref[...], (SUBLANES, STATE_TILE))
    ai = jnp.broadcast_to(aim_ref[...], (SUBLANES, STATE_TILE))
    d = d_ref[...]
    groups_per_chunk = S5_CHUNK // SUBLANES

    def chunk_rows(q):
        return slice(q * S5_CHUNK, (q + 1) * S5_CHUNK)

    def local_rows(i):
        j = i % groups_per_chunk
        return slice(j * SUBLANES, (j + 1) * SUBLANES)

    def input_chunk(q):
        ub = up[chunk_rows(q), :].astype(BF16)
        hre[q][...] = _dot(ub, bbre_ref[...])
        him[q][...] = _dot(ub, bbim_ref[...])

    def scan_chunk(q, carry):
        for i in range(q * groups_per_chunk, (q + 1) * groups_per_chunk):
            nb, r = i // seg_len, local_rows(i)
            blk = slice(nb * SUBLANES, (nb + 1) * SUBLANES)
            if i % seg_len == 0:
                if chained:
                    carry = (jnp.zeros((SUBLANES, STATE_TILE), F32),) * 2
                else:
                    carry = (x0_ref[blk, :], x1_ref[blk, :])
            hr, hi = carry
            carry = (ar * hr - ai * hi + hre[q][r, :], ar * hi + ai * hr + him[q][r, :])
            hre[q][r, :], him[q][r, :] = carry
            if not chained and i % seg_len == seg_len - 1:
                sre_ref[blk, :], sim_ref[blk, :] = carry
        return carry

    def carry_chunk(q, f):
        for i in range(q * groups_per_chunk, (q + 1) * groups_per_chunk):
            r = local_rows(i)
            fr, fi = f
            hre[q][r, :] = hre[q][r, :] + fr
            him[q][r, :] = him[q][r, :] + fi
            f = (ar * fr - ai * fi, ar * fi + ai * fr)
        return f

    def output_chunk(q):
        y = _dot(hre[q][...].astype(BF16), ccre_ref[...]) + _dot(him[q][...].astype(BF16), ccim_ref[...])
        gp[chunk_rows(q), :] = jax.nn.gelu(y + d * up[chunk_rows(q), :])

    input_chunk(0)
    carry = None
    for q in range(n_chunks):
        if q + 1 < n_chunks:
            input_chunk(q + 1)
        carry = scan_chunk(q, carry)
        if not chained and q >= 1:
            output_chunk(q - 1)

    if chained:
        er, ei = carry
        pr, pi = are_ref[...], aim_ref[...]
        for _ in range(seg_len.bit_length() - 1):
            pr, pi = pr * pr - pi * pi, 2.0 * (pr * pi)
        row = lax.broadcasted_iota(jnp.int32, (SUBLANES, STATE_TILE), 0)
        xr = jnp.zeros((SUBLANES, STATE_TILE), F32)
        xi = xr
        for _ in range(SUBLANES - 1):
            yr = er + pr * xr - pi * xi
            yi = ei + pr * xi + pi * xr
            xr = jnp.where(row == 0, 0.0, pltpu.roll(yr, 1, 0))
            xi = jnp.where(row == 0, 0.0, pltpu.roll(yi, 1, 0))
        f = (ar * xr - ai * xi, ar * xi + ai * xr)
        for q in range(n_chunks):
            f = carry_chunk(q, f)
            if q >= 1:
                output_chunk(q - 1)
        last = n_chunks - 1
        sre_ref[...] = hre[last][S5_CHUNK - 1:S5_CHUNK, :]
        sim_ref[...] = him[last][S5_CHUNK - 1:S5_CHUNK, :]
    output_chunk(n_chunks - 1)

    def ungroup(lane, c):
        gn[natural(lane), :] = gp[regrouped(lane), :]
        return c

    for_each_lane(ungroup)
    g_ref[...] = gn[...].astype(BF16)


def _s5(tr, u2d, d_skip, tabs, h0):
    are, aim, bbre, bbim, ccre, ccim = tabs
    chained = h0 is None
    if chained:
        n_batch, rows = tr.n_seq, tr.seq_len
        n_blocks, seg_len = 1, tr.seq_len // SUBLANES
        assert seg_len == 1 << (seg_len.bit_length() - 1)
        x_specs, x_args = [], ()
        st_spec = pl.BlockSpec((None, 1, STATE_TILE), lambda k, b: (b, 0, k))
        st_shape = jax.ShapeDtypeStruct((tr.n_seq, 1, SSM_GROUPS * SSM_STATE), F32)
    else:
        n_batch, rows = 1, tr.n_seq * tr.seq_len
        n_blocks, seg_len = tr.n_seq // SUBLANES, tr.seq_len
        x_specs = [pl.BlockSpec((tr.n_seq, STATE_TILE), lambda k, b: (0, k))] * 2
        x_args = h0
        st_spec = pl.BlockSpec((tr.n_seq, STATE_TILE), lambda k, b: (0, k))
        st_shape = jax.ShapeDtypeStruct((tr.n_seq, SSM_GROUPS * SSM_STATE), F32)
    assert rows % S5_CHUNK == 0
    vec = pl.BlockSpec((None, 1, STATE_TILE), lambda k, b: (k, 0, 0))
    bspec = pl.BlockSpec((LANES, STATE_TILE), lambda k, b: (k, 0))
    cspec = pl.BlockSpec((STATE_TILE, LANES), lambda k, b: (k, 0))
    tok = pl.BlockSpec((rows, LANES), lambda k, b: (b, k))
    return pl.pallas_call(
        functools.partial(_s5_kernel, n_blocks=n_blocks, seg_len=seg_len, chained=chained),
        grid=(N_LANE_TILES, n_batch),
        in_specs=[tok, pl.BlockSpec((1, LANES), lambda k, b: (0, k)), bspec, bspec, cspec, cspec, vec, vec]
        + x_specs,
        out_specs=[tok, st_spec, st_spec],
        out_shape=[jax.ShapeDtypeStruct(u2d.shape, BF16), st_shape, st_shape],
        scratch_shapes=[pltpu.VMEM((rows, LANES), F32)] * 3
        + [pltpu.VMEM((S5_CHUNK, STATE_TILE), F32)] * (2 * (rows // S5_CHUNK)),
        compiler_params=_params("parallel", "parallel"),
        name="s5",
    )(u2d, d_skip, bbre, bbim, ccre, ccim, are, aim, *x_args)


def _glu_kernel(*refs, residual):
    if residual:
        a_ref, wa_ref, wb_ref, ba_ref, bb_ref, x_ref, gate_ref, o_ref = refs
    else:
        a_ref, wa_ref, wb_ref, ba_ref, bb_ref, o_ref = refs
    a = a_ref[...]
    za = _dot(a, wa_ref[...].astype(BF16)) + ba_ref[...]
    zb = _dot(a, wb_ref[...].astype(BF16)) + bb_ref[...]
    out = za * jax.nn.sigmoid(zb)
    if residual:
        out = x_ref[...] + gate_ref[...] * out.reshape(o_ref.shape)
    o_ref[...] = out


def _glu(tr, a2d, w, b, x3=None, mod=None, layer=None):
    tm, tn = tr.tm, 512
    nb, nj = tm // SUBLANES, D_MODEL // tn
    b3 = b.reshape(1, 1, 2 * D_MODEL)
    residual = x3 is not None
    in_specs = [pl.BlockSpec((tm, D_MODEL), lambda j, i: (i, 0)),
                pl.BlockSpec((None, D_MODEL, tn), lambda j, i: (0, 0, j)),
                pl.BlockSpec((None, D_MODEL, tn), lambda j, i: (0, 0, j + nj)),
                pl.BlockSpec((None, 1, tn), lambda j, i: (0, 0, j)),
                pl.BlockSpec((None, 1, tn), lambda j, i: (0, 0, j + nj))]
    args = (a2d, w, w, b3, b3)
    if residual:
        out_spec = pl.BlockSpec((nb, SUBLANES, tn), lambda j, i: (i, 0, j))
        out_shape = jax.ShapeDtypeStruct(x3.shape, F32)
        in_specs += [out_spec, _mod_spec(tr, layer, 2, tn, lambda j, i: i, lambda j, i: j)]
        args += (x3, mod)
    else:
        out_spec = pl.BlockSpec((tm, tn), lambda j, i: (i, j))
        out_shape = jax.ShapeDtypeStruct(a2d.shape, F32)
    return pl.pallas_call(
        functools.partial(_glu_kernel, residual=residual),
        grid=(nj, a2d.shape[0] // tm),
        in_specs=in_specs,
        out_specs=out_spec,
        out_shape=out_shape,
        compiler_params=_params("parallel", "parallel"),
        name="glu",
    )(*args)


CONV_CHUNK = SUBLANES * SUBLANES
CONV_ROWS = 256


def _conv_taps(win, w_ref, b_ref, ls):
    w = [jnp.broadcast_to(w_ref[k:k + 1, ls], (SUBLANES, LANES)) for k in range(CONV_WIDTH)]
    acc = [jnp.broadcast_to(b_ref[:, ls], (SUBLANES, LANES))] * SUBLANES
    for o in range(CONV_WIDTH + SUBLANES - 1):
        x = win(o)
        for r in range(SUBLANES):
            if 0 <= o - r < CONV_WIDTH:
                acc[r] = acc[r] + w[o - r] * x
    return acc


def _ln_silu_store(cbuf, lg_ref, lb_ref, o_ref):
    n = cbuf.shape[0]
    inv_d = 1.0 / (n * LANES)
    tot = cbuf[0]
    for l in range(1, n):
        tot = tot + cbuf[l]
    mean = jnp.sum(tot, axis=-1, keepdims=True) * inv_d
    sq = jnp.zeros_like(tot)
    for l in range(n):
        xc = cbuf[l] - mean
        sq = sq + xc * xc
    rstd = lax.rsqrt(jnp.sum(sq, axis=-1, keepdims=True) * inv_d + LN_EPS)
    for l in range(n):
        ls = slice(l * LANES, (l + 1) * LANES)
        y = (cbuf[l] - mean) * rstd * lg_ref[:, ls] + lb_ref[:, ls]
        o_ref[:, ls] = jax.nn.silu(y).astype(BF16)


def _conv_ln_kernel(v_ref, halo_ref, w_ref, b_ref, lg_ref, lb_ref, o_ref, pad, cbuf, *, tiles_per_seq):
    rows = v_ref.shape[0]
    first = (pl.program_id(0) % tiles_per_seq) == 0
    for l in range(N_LANE_TILES):
        ls = slice(l * LANES, (l + 1) * LANES)
        pad[l, 0:HALO, :] = jnp.where(first, 0.0, halo_ref[:, ls])
        pad[l, HALO:, :] = v_ref[:, ls]

        def chunk(c, carry, l=l, ls=ls):
            base = pl.multiple_of(c * CONV_CHUNK, CONV_CHUNK)
            acc = _conv_taps(lambda o: pad[l, pl.ds(base + HIST_OFF + o, SUBLANES, stride=SUBLANES), :],
                             w_ref, b_ref, ls)
            for r in range(SUBLANES):
                cbuf[l, pl.ds(base + r, SUBLANES, stride=SUBLANES), :] = acc[r]
            return carry

        lax.fori_loop(0, rows // CONV_CHUNK, chunk, 0)
    _ln_silu_store(cbuf, lg_ref, lb_ref, o_ref)


def _conv_ln_long(tr, v2d, w, b, ln_g, ln_b):
    rows = CONV_ROWS
    assert tr.seq_len % rows == 0 and rows % CONV_CHUNK == 0 and rows % HALO == 0
    hb = rows // HALO
    row = pl.BlockSpec((1, D_MODEL), lambda i: (0, 0))
    return pl.pallas_call(
        functools.partial(_conv_ln_kernel, tiles_per_seq=tr.seq_len // rows),
        grid=(v2d.shape[0] // rows,),
        in_specs=[pl.BlockSpec((rows, D_MODEL), lambda i: (i, 0)),
                  pl.BlockSpec((HALO, D_MODEL), lambda i: (jnp.maximum(i * hb - 1, 0), 0)),
                  pl.BlockSpec((None, CONV_WIDTH, D_MODEL), lambda i: (0, 0, 0)),
                  row, row, row],
        out_specs=pl.BlockSpec((rows, D_MODEL), lambda i: (i, 0)),
        out_shape=jax.ShapeDtypeStruct(v2d.shape, BF16),
        scratch_shapes=[pltpu.VMEM((N_LANE_TILES, HALO + rows, LANES), F32),
                        pltpu.VMEM((N_LANE_TILES, rows, LANES), F32)],
        compiler_params=_params("parallel"),
        name="conv_ln_long",
    )(v2d, v2d, w, b, ln_g, ln_b)


def _conv_ln_step_kernel(v_ref, cache_ref, w_ref, b_ref, lg_ref, lb_ref, o_ref, nc_ref, vs, cbuf):
    for l in range(N_LANE_TILES):
        ls = slice(l * LANES, (l + 1) * LANES)
        vs[l] = v_ref[:, ls]
        new = [vs[l, pl.ds(t, SUBLANES, stride=SUBLANES), :] for t in range(SUBLANES)]

        def padded(o, ls=ls, new=new):
            return cache_ref[o, :, ls] if o < CONV_HIST else new[o - CONV_HIST]

        acc = _conv_taps(padded, w_ref, b_ref, ls)
        for t in range(SUBLANES):
            cbuf[l, pl.ds(t, SUBLANES, stride=SUBLANES), :] = acc[t]
        for q in range(CONV_HIST):
            nc_ref[q, :, ls] = padded(q + SUBLANES)
    _ln_silu_store(cbuf, lg_ref, lb_ref, o_ref)


def _conv_ln_step(tr, v2d, cache_t, w, b, ln_g, ln_b):
    assert tr.seq_len == SUBLANES
    rows = SUBLANES * tr.seq_len
    row = pl.BlockSpec((1, D_MODEL), lambda s: (0, 0))
    cspec = pl.BlockSpec((CONV_HIST, SUBLANES, D_MODEL), lambda s: (0, s, 0))
    return pl.pallas_call(
        _conv_ln_step_kernel,
        grid=(tr.n_seq // SUBLANES,),
        in_specs=[pl.BlockSpec((rows, D_MODEL), lambda s: (s, 0)), cspec,
                  pl.BlockSpec((None, CONV_WIDTH, D_MODEL), lambda s: (0, 0, 0)),
                  row, row, row],
        out_specs=[pl.BlockSpec((rows, D_MODEL), lambda s: (s, 0)), cspec],
        out_shape=[jax.ShapeDtypeStruct(v2d.shape, BF16), jax.ShapeDtypeStruct(cache_t.shape, F32)],
        scratch_shapes=[pltpu.VMEM((N_LANE_TILES, rows, LANES), F32),
                        pltpu.VMEM((N_LANE_TILES, rows, LANES), F32)],
        compiler_params=_params("parallel"),
        name="conv_ln_step",
    )(v2d, cache_t, w, b, ln_g, ln_b)


MLP_TILE = 1024
MLP_TK = 2048


def _mlp_up_kernel(h_ref, w_ref, o_ref):
    a = jnp.maximum(_dot(h_ref[...], w_ref[...].astype(BF16)), 0.0)
    o_ref[...] = (a * a).astype(BF16)


def _mlp_up(tr, h2d, w1, layer):
    tm, tn = tr.tm, MLP_TILE
    return pl.pallas_call(
        _mlp_up_kernel,
        grid=(D_FF // tn, h2d.shape[0] // tm),
        in_specs=[pl.BlockSpec((tm, D_MODEL), lambda j, i: (i, 0)),
                  pl.BlockSpec((None, D_MODEL, tn), lambda j, i: (layer, 0, j))],
        out_specs=pl.BlockSpec((tm, tn), lambda j, i: (i, j)),
        out_shape=jax.ShapeDtypeStruct((h2d.shape[0], D_FF), BF16),
        compiler_params=_params("parallel", "parallel"),
        name="mlp_up",
    )(h2d, w1)


def _mm_res_kernel(*refs, has_bias, single_k):
    if has_bias:
        a_ref, w_ref, b_ref, x_ref, gate_ref, o_ref = refs
    else:
        a_ref, w_ref, x_ref, gate_ref, o_ref = refs
    def finish(out):
        if has_bias:
            out = out + b_ref[...]
        o_ref[...] = x_ref[...] + gate_ref[...] * out

    prod = _dot(a_ref[...], w_ref[...].astype(BF16)).reshape(o_ref.shape)
    if single_k:
        finish(prod)
        return
    k = pl.program_id(2)

    @pl.when(k == 0)
    def _():
        o_ref[...] = prod

    @pl.when(k > 0)
    def _():
        o_ref[...] += prod

    @pl.when(k == pl.num_programs(2) - 1)
    def _():
        finish(o_ref[...])


def _mm_res(tr, a2d, w, w_idx, x3, mod, layer, part, bias=None):
    tm, tn = tr.tm, MLP_TILE
    kdim = a2d.shape[1]
    tk = min(kdim, MLP_TK)
    nb = tm // SUBLANES
    xspec = pl.BlockSpec((nb, SUBLANES, tn), lambda i, j, k: (i, 0, j))
    has_bias = bias is not None
    bias_specs = [pl.BlockSpec((1, tn), lambda i, j, k: (0, j))] if has_bias else []
    bias_args = (bias,) if has_bias else ()
    return pl.pallas_call(
        functools.partial(_mm_res_kernel, has_bias=has_bias, single_k=kdim == tk),
        grid=(a2d.shape[0] // tm, D_MODEL // tn, kdim // tk),
        in_specs=[pl.BlockSpec((tm, tk), lambda i, j, k: (i, k)),
                  pl.BlockSpec((None, tk, tn), lambda i, j, k: (w_idx, k, j))] + bias_specs
        + [xspec, _mod_spec(tr, layer, part, tn, lambda i, j, k: i, lambda i, j, k: j)],
        out_specs=xspec,
        out_shape=jax.ShapeDtypeStruct(x3.shape, F32),
        compiler_params=_params("parallel", "parallel", "arbitrary"),
        name="mm_res",
    )(a2d, w, *bias_args, x3, mod)


def _final_norm_kernel(x_ref, g_ref, o_ref):
    x = x_ref[...]
    ms = jnp.mean(x * x, axis=-1, keepdims=True)
    o_ref[...] = x * lax.rsqrt(ms + RMS_EPS) * g_ref[...]


def _final_norm(tr, x3, g):
    nb = min(tr.tm, 512) // SUBLANES
    spec = pl.BlockSpec((nb, SUBLANES, D_MODEL), lambda i: (i, 0, 0))
    return pl.pallas_call(
        _final_norm_kernel,
        grid=(x3.shape[0] // nb,),
        in_specs=[spec, pl.BlockSpec((1, D_MODEL), lambda i: (0, 0))],
        out_specs=spec,
        out_shape=jax.ShapeDtypeStruct(x3.shape, F32),
        compiler_params=_params("parallel"),
        name="final_norm",
    )(x3, g)


def _mlp(tr, x3, g, w1, w2, mod, layer):
    h2d = _prenorm(tr, x3, g, mod, layer, 1, BF16)
    return _mm_res(tr, _mlp_up(tr, h2d, w1, layer), w2, layer, x3, mod, layer, 5)


def _trunk(tr, x, mod, h0, cache, tabs, p):
    tokens = tr.n_seq * tr.seq_len
    x3 = x.reshape(tokens // SUBLANES, SUBLANES, D_MODEL)

    u2d = _prenorm(tr, x3, p["rms_g_mix"], mod, 0, 0, F32)
    g2d, s_re, s_im = _s5(tr, u2d, p["ssm_d"], tabs, h0)
    x3 = _glu(tr, g2d, p["ssm_w_glu"], p["ssm_b_glu"], x3, mod, 0)
    x3 = _mlp(tr, x3, p["rms_g_mlp"], p["mlp_w1"], p["mlp_w2"], mod, 0)

    h2d = _prenorm(tr, x3, p["rms_g_mix"], mod, 1, 0, BF16)
    v2d = _glu(tr, h2d, p["conv_w_pw1"], p["conv_b_pw1"])
    conv_args = (p["conv_w_dw"], p["conv_b_dw"], p["conv_ln_g"], p["conv_ln_b"])
    if cache is None:
        hc2d = _conv_ln_long(tr, v2d, *conv_args)
        new_cache = v2d.reshape(1, tr.n_seq, tr.seq_len, D_MODEL)[:, :, tr.seq_len - CONV_HIST:]
    else:
        hc2d, cache_t = _conv_ln_step(tr, v2d, jnp.transpose(cache[0], (1, 0, 2)), *conv_args)
        new_cache = jnp.transpose(cache_t, (1, 0, 2))[None]
    x3 = _mm_res(tr, hc2d, p["conv_w_pw2"], 0, x3, mod, 1, 2, bias=p["conv_b_pw2"])
    x3 = _mlp(tr, x3, p["rms_g_mlp"], p["mlp_w1"], p["mlp_w2"], mod, 1)
    y3 = _final_norm(tr, x3, p["final_g"])

    state_shape = (1, tr.n_seq, SSM_GROUPS, SSM_STATE)
    return (y3.reshape(tr.n_seq, tr.seq_len, D_MODEL), s_re.reshape(state_shape), s_im.reshape(state_shape),
            new_cache)


def kernel(x_prompt, x_sample, state_ssm_re, state_ssm_im, cache_conv, c_prompt, c_sample, rms_g_mix, rms_g_mlp, w_ada, b_ada, ssm_a_re, ssm_a_im, ssm_log_dt, ssm_b_re, ssm_b_im, ssm_c_re, ssm_c_im, ssm_d, ssm_w_glu, ssm_b_glu, conv_w_pw1, conv_b_pw1, conv_w_dw, conv_b_dw, conv_ln_g, conv_ln_b, conv_w_pw2, conv_b_pw2, mlp_w1, mlp_w2, final_g):
    bp, lp, _ = x_prompt.shape
    bs, ls, _ = x_sample.shape
    assert w_ada.shape[0] == 2 and ssm_a_re.shape[0] == 1 and conv_w_dw.shape[0] == 1
    prompt = Trunk(bp, lp, 1024, mod_row=bs)
    sample = Trunk(bs, ls, bs * ls, mod_row=0)

    depth = w_ada.shape[0]
    p = dict(rms_g_mix=rms_g_mix.reshape(depth, 1, D_MODEL), rms_g_mlp=rms_g_mlp.reshape(depth, 1, D_MODEL),
             ssm_d=ssm_d, ssm_w_glu=ssm_w_glu, ssm_b_glu=ssm_b_glu,
             conv_w_pw1=conv_w_pw1, conv_b_pw1=conv_b_pw1, conv_w_dw=conv_w_dw, conv_b_dw=conv_b_dw,
             conv_ln_g=conv_ln_g, conv_ln_b=conv_ln_b, conv_w_pw2=conv_w_pw2, conv_b_pw2=conv_b_pw2,
             mlp_w1=mlp_w1, mlp_w2=mlp_w2, final_g=final_g.reshape(1, D_MODEL))

    n_c = bp + bs
    pad_rows = -n_c % SUBLANES
    c_all = jnp.concatenate([c_sample, c_prompt, jnp.zeros((pad_rows, D_MODEL), F32)], axis=0)
    mod = _ada(c_all, w_ada, b_ada)

    tabs = _s5_prep(ssm_a_re[0], ssm_a_im[0], ssm_log_dt[0], ssm_b_re[0], ssm_b_im[0],
                    ssm_c_re[0], ssm_c_im[0])

    n_state = SSM_GROUPS * SSM_STATE
    h0 = (state_ssm_re.reshape(bs, n_state), state_ssm_im.reshape(bs, n_state))
    y_p, p_re, p_im, p_buf = _trunk(prompt, x_prompt, mod, None, None, tabs, p)
    y_s, s_re, s_im, s_buf = _trunk(sample, x_sample, mod, h0, cache_conv, tabs, p)
    return (y_p, y_s, p_re, p_im, p_buf, s_re, s_im, s_buf)
```

```python
import collections
import functools

import jax
import jax.numpy as jnp
from jax import lax
from jax.experimental import pallas as pl
from jax.experimental.pallas import tpu as pltpu

F32 = jnp.float32
BF16 = jnp.bfloat16

D_MODEL = 2048
D_FF = 4 * D_MODEL
SSM_GROUP = 16
SSM_GROUPS = D_MODEL // SSM_GROUP
SSM_STATE = 64
LOG2_GROUP = SSM_GROUP.bit_length() - 1
LOG2_STATE = SSM_STATE.bit_length() - 1
assert SSM_GROUP == 1 << LOG2_GROUP and SSM_STATE == 1 << LOG2_STATE
CONV_WIDTH = 31
CONV_HIST = CONV_WIDTH - 1
RMS_EPS = 1e-6
LN_EPS = 1e-5

LANES = 128
SUBLANES = 8
VMEM_LIMIT_BYTES = 56 * 1024 * 1024

GROUPS_PER_TILE = LANES // SSM_GROUP
STATE_TILE = GROUPS_PER_TILE * SSM_STATE
N_LANE_TILES = D_MODEL // LANES
HALO = 32
HIST_OFF = HALO - CONV_HIST

Trunk = collections.namedtuple("Trunk", "n_seq seq_len tm mod_row")


def _params(*sem):
    return pltpu.CompilerParams(dimension_semantics=sem, vmem_limit_bytes=VMEM_LIMIT_BYTES)


def _dot(a, b):
    return jnp.dot(a, b, preferred_element_type=F32)


def _norm_mod(x3, g, sc, sh):
    ms = jnp.mean(x3 * x3, axis=-1, keepdims=True)
    return (x3 * lax.rsqrt(ms + RMS_EPS) * g) * (1.0 + sc) + sh


def _mod_spec(tr, layer, part, tn, ti, tj):
    nblk = D_MODEL // tn
    if tr.seq_len >= tr.tm:
        per = tr.seq_len // tr.tm
        return pl.BlockSpec((None, 1, 1, tn),
                            lambda *g: (layer, tr.mod_row + ti(*g) // per, 0, part * nblk + tj(*g)))
    nbm = tr.tm // SUBLANES
    assert tr.seq_len == SUBLANES and tr.mod_row % nbm == 0
    return pl.BlockSpec((None, nbm, 1, tn),
                        lambda *g: (layer, tr.mod_row // nbm + ti(*g), 0, part * nblk + tj(*g)))


def _ada_kernel(c_ref, w_ref, b_ref, o_ref):
    ca = jax.nn.silu(c_ref[...]).astype(BF16)
    mod = _dot(ca, w_ref[...].astype(BF16)) + b_ref[...]
    for r in range(o_ref.shape[0]):
        o_ref[r] = mod[r:r + 1, :]


def _ada(c_all, w_ada, b_ada):
    depth, d, n = w_ada.shape
    rows = c_all.shape[0]
    tn = 1024
    return pl.pallas_call(
        _ada_kernel,
        grid=(depth, n // tn),
        in_specs=[pl.BlockSpec((rows, d), lambda l, j: (0, 0)),
                  pl.BlockSpec((None, d, tn), lambda l, j: (l, 0, j)),
                  pl.BlockSpec((None, 1, tn), lambda l, j: (l, 0, j))],
        out_specs=pl.BlockSpec((None, rows, 1, tn), lambda l, j: (l, 0, 0, j)),
        out_shape=jax.ShapeDtypeStruct((depth, rows, 1, n), F32),
        compiler_params=_params("parallel", "parallel"),
        name="ada",
    )(c_all, w_ada, b_ada.reshape(depth, 1, n))


def _prenorm_kernel(x_ref, g_ref, sc_ref, sh_ref, o_ref):
    h = _norm_mod(x_ref[...], g_ref[...], sc_ref[...], sh_ref[...])
    o_ref[...] = h.reshape(o_ref.shape).astype(o_ref.dtype)


def _prenorm(tr, x3, g, mod, layer, sublayer, dtype):
    tm = min(tr.tm, 512)
    trp = tr._replace(tm=tm)
    nb = tm // SUBLANES
    ti, tj = (lambda i: i), (lambda i: 0)
    return pl.pallas_call(
        _prenorm_kernel,
        grid=(x3.shape[0] // nb,),
        in_specs=[pl.BlockSpec((nb, SUBLANES, D_MODEL), lambda i: (i, 0, 0)),
                  pl.BlockSpec((None, 1, D_MODEL), lambda i: (layer, 0, 0)),
                  _mod_spec(trp, layer, 3 * sublayer + 1, D_MODEL, ti, tj),
                  _mod_spec(trp, layer, 3 * sublayer, D_MODEL, ti, tj)],
        out_specs=pl.BlockSpec((tm, D_MODEL), lambda i: (i, 0)),
        out_shape=jax.ShapeDtypeStruct((x3.shape[0] * SUBLANES, D_MODEL), dtype),
        compiler_params=_params("parallel"),
        name="prenorm",
    )(x3, g, mod, mod)


def _s5_prep_kernel(lre_ref, lim_ref, ldt_ref, bre_ref, bim_ref, cre_ref, cim_ref,
                    are_ref, aim_ref, bbre_ref, bbim_ref, ccre_ref, ccim_ref):
    lr, li = lre_ref[...], lim_ref[...]
    dt = jnp.exp(ldt_ref[...])
    mag = jnp.exp(lr * dt)
    are = mag * jnp.cos(li * dt)
    aim = mag * jnp.sin(li * dt)
    er, ei = are - 1.0, aim
    den = lr * lr + li * li
    qre = (er * lr + ei * li) / den
    qim = (ei * lr - er * li) / den
    are_ref[...] = are
    aim_ref[...] = aim

    br, bi = bre_ref[...], bim_ref[...]
    keep = (jnp.right_shift(lax.broadcasted_iota(jnp.int32, br.shape, 0), LOG2_GROUP)
            == jnp.right_shift(lax.broadcasted_iota(jnp.int32, br.shape, 1), LOG2_STATE))
    bbre_ref[...] = jnp.where(keep, qre * br - qim * bi, 0.0).astype(BF16)
    bbim_ref[...] = jnp.where(keep, qre * bi + qim * br, 0.0).astype(BF16)

    cr, ci = cre_ref[...], cim_ref[...]
    keep = (jnp.right_shift(lax.broadcasted_iota(jnp.int32, cr.shape, 0), LOG2_STATE)
            == jnp.right_shift(lax.broadcasted_iota(jnp.int32, cr.shape, 1), LOG2_GROUP))
    ccre_ref[...] = jnp.where(keep, cr, 0.0).astype(BF16)
    ccim_ref[...] = jnp.where(keep, -ci, 0.0).astype(BF16)


def _s5_prep(a_re, a_im, log_dt, b_re, b_im, c_re, c_im):
    nt = N_LANE_TILES
    tile3 = lambda a: a.reshape(nt, 1, STATE_TILE)
    ldt = jnp.broadcast_to(log_dt[:, None], (SSM_GROUPS, SSM_STATE))
    b_rows = lambda b: jnp.tile(b.transpose(0, 2, 1).reshape(D_MODEL, SSM_STATE), (1, GROUPS_PER_TILE))
    c_rows = lambda c: jnp.tile(c.transpose(0, 2, 1).reshape(SSM_GROUPS * SSM_STATE, SSM_GROUP),
                                (1, GROUPS_PER_TILE))
    vec = pl.BlockSpec((None, 1, STATE_TILE), lambda k: (k, 0, 0))
    bspec = pl.BlockSpec((LANES, STATE_TILE), lambda k: (k, 0))
    cspec = pl.BlockSpec((STATE_TILE, LANES), lambda k: (k, 0))
    return pl.pallas_call(
        _s5_prep_kernel,
        grid=(nt,),
        in_specs=[vec, vec, vec, bspec, bspec, cspec, cspec],
        out_specs=[vec, vec, bspec, bspec, cspec, cspec],
        out_shape=[jax.ShapeDtypeStruct((nt, 1, STATE_TILE), F32)] * 2
        + [jax.ShapeDtypeStruct((D_MODEL, STATE_TILE), BF16)] * 2
        + [jax.ShapeDtypeStruct((SSM_GROUPS * SSM_STATE, LANES), BF16)] * 2,
        compiler_params=_params("parallel"),
        name="s5_prep",
    )(tile3(a_re), tile3(a_im), tile3(ldt), b_rows(b_re), b_rows(b_im), c_rows(c_re), c_rows(c_im))


S5_CHUNK = 512


def _s5_kernel(*refs, n_blocks, seg_len, chained):
    u_ref, d_ref, bbre_ref, bbim_ref, ccre_ref, ccim_ref, are_ref, aim_ref = refs[:8]
    refs = refs[8:]
    if not chained:
        x0_ref, x1_ref = refs[:2]
        refs = refs[2:]
    g_ref, sre_ref, sim_ref, up, gp, gn = refs[:6]
    rows = n_blocks * seg_len * SUBLANES
    n_chunks = rows // S5_CHUNK
    hre, him = refs[6:6 + n_chunks], refs[6 + n_chunks:]

    def natural(lane):
        start = lane * seg_len
        return pl.ds(start if isinstance(lane, int) else pl.multiple_of(start, SUBLANES), seg_len)

    def regrouped(lane):
        return pl.ds((lane // SUBLANES) * (seg_len * SUBLANES) + lane % SUBLANES, seg_len, stride=SUBLANES)

    def for_each_lane(body):
        if n_blocks == 1:
            for lane in range(SUBLANES):
                body(lane, 0)
        else:
            lax.fori_loop(0, n_blocks * SUBLANES, body, 0, unroll=8)

    def regroup(lane, c):
        up[regrouped(lane), :] = u_ref[natural(lane), :]
        return c

    for_each_lane(regroup)

    ar = jnp.broadcast_to(are_ref[...], (SUBLANES, STATE_TILE))
    ai = jnp.broadcast_to(aim_ref[...], (SUBLANES, STATE_TILE))
    d = d_ref[...]
    groups_per_chunk = S5_CHUNK // SUBLANES

    def chunk_rows(q):
        return slice(q * S5_CHUNK, (q + 1) * S5_CHUNK)

    def local_rows(i):
        j = i % groups_per_chunk
        return slice(j * SUBLANES, (j + 1) * SUBLANES)

    def input_chunk(q):
        ub = up[chunk_rows(q), :].astype(BF16)
        hre[q][...] = _dot(ub, bbre_ref[...])
        him[q][...] = _dot(ub, bbim_ref[...])

    def scan_chunk(q, carry):
        for i in range(q * groups_per_chunk, (q + 1) * groups_per_chunk):
            nb, r = i // seg_len, local_rows(i)
            blk = slice(nb * SUBLANES, (nb + 1) * SUBLANES)
            if i % seg_len == 0:
                if chained:
                    carry = (jnp.zeros((SUBLANES, STATE_TILE), F32),) * 2
                else:
                    carry = (x0_ref[blk, :], x1_ref[blk, :])
            hr, hi = carry
            carry = (ar * hr - ai * hi + hre[q][r, :], ar * hi + ai * hr + him[q][r, :])
            hre[q][r, :], him[q][r, :] = carry
            if not chained and i % seg_len == seg_len - 1:
                sre_ref[blk, :], sim_ref[blk, :] = carry
        return carry

    def carry_chunk(q, f):
        for i in range(q * groups_per_chunk, (q + 1) * groups_per_chunk):
            r = local_rows(i)
            fr, fi = f
            hre[q][r, :] = hre[q][r, :] + fr
            him[q][r, :] = him[q][r, :] + fi
            f = (ar * fr - ai * fi, ar * fi + ai * fr)
        return f

    def output_chunk(q):
        y = _dot(hre[q][...].astype(BF16), ccre_ref[...]) + _dot(him[q][...].astype(BF16), ccim_ref[...])
        gp[chunk_rows(q), :] = jax.nn.gelu(y + d * up[chunk_rows(q), :])

    input_chunk(0)
    carry = None
    for q in range(n_chunks):
        if q + 1 < n_chunks:
            input_chunk(q + 1)
        carry = scan_chunk(q, carry)
        if not chained and q >= 1:
            output_chunk(q - 1)

    if chained:
        er, ei = carry
        pr, pi = are_ref[...], aim_ref[...]
        for _ in range(seg_len.bit_length() - 1):
            pr, pi = pr * pr - pi * pi, 2.0 * (pr * pi)
        row = lax.broadcasted_iota(jnp.int32, (SUBLANES, STATE_TILE), 0)
        xr = jnp.zeros((SUBLANES, STATE_TILE), F32)
        xi = xr
        for _ in range(SUBLANES - 1):
            yr = er + pr * xr - pi * xi
            yi = ei + pr * xi + pi * xr
            xr = jnp.where(row == 0, 0.0, pltpu.roll(yr, 1, 0))
            xi = jnp.where(row == 0, 0.0, pltpu.roll(yi, 1, 0))
        f = (ar * xr - ai * xi, ar * xi + ai * xr)
        for q in range(n_chunks):
            f = carry_chunk(q, f)
            if q >= 1:
                output_chunk(q - 1)
        last = n_chunks - 1
        sre_ref[...] = hre[last][S5_CHUNK - 1:S5_CHUNK, :]
        sim_ref[...] = him[last][S5_CHUNK - 1:S5_CHUNK, :]
    output_chunk(n_chunks - 1)

    def ungroup(lane, c):
        gn[natural(lane), :] = gp[regrouped(lane), :]
        return c

    for_each_lane(ungroup)
    g_ref[...] = gn[...].astype(BF16)


def _s5(tr, u2d, d_skip, tabs, h0):
    are, aim, bbre, bbim, ccre, ccim = tabs
    chained = h0 is None
    if chained:
        n_batch, rows = tr.n_seq, tr.seq_len
        n_blocks, seg_len = 1, tr.seq_len // SUBLANES
        assert seg_len == 1 << (seg_len.bit_length() - 1)
        x_specs, x_args = [], ()
        st_spec = pl.BlockSpec((None, 1, STATE_TILE), lambda k, b: (b, 0, k))
        st_shape = jax.ShapeDtypeStruct((tr.n_seq, 1, SSM_GROUPS * SSM_STATE), F32)
    else:
        n_batch, rows = 1, tr.n_seq * tr.seq_len
        n_blocks, seg_len = tr.n_seq // SUBLANES, tr.seq_len
        x_specs = [pl.BlockSpec((tr.n_seq, STATE_TILE), lambda k, b: (0, k))] * 2
        x_args = h0
        st_spec = pl.BlockSpec((tr.n_seq, STATE_TILE), lambda k, b: (0, k))
        st_shape = jax.ShapeDtypeStruct((tr.n_seq, SSM_GROUPS * SSM_STATE), F32)
    assert rows % S5_CHUNK == 0
    vec = pl.BlockSpec((None, 1, STATE_TILE), lambda k, b: (k, 0, 0))
    bspec = pl.BlockSpec((LANES, STATE_TILE), lambda k, b: (k, 0))
    cspec = pl.BlockSpec((STATE_TILE, LANES), lambda k, b: (k, 0))
    tok = pl.BlockSpec((rows, LANES), lambda k, b: (b, k))
    return pl.pallas_call(
        functools.partial(_s5_kernel, n_blocks=n_blocks, seg_len=seg_len, chained=chained),
        grid=(N_LANE_TILES, n_batch),
        in_specs=[tok, pl.BlockSpec((1, LANES), lambda k, b: (0, k)), bspec, bspec, cspec, cspec, vec, vec]
        + x_specs,
        out_specs=[tok, st_spec, st_spec],
        out_shape=[jax.ShapeDtypeStruct(u2d.shape, BF16), st_shape, st_shape],
        scratch_shapes=[pltpu.VMEM((rows, LANES), F32)] * 3
        + [pltpu.VMEM((S5_CHUNK, STATE_TILE), F32)] * (2 * (rows // S5_CHUNK)),
        compiler_params=_params("parallel", "parallel"),
        name="s5",
    )(u2d, d_skip, bbre, bbim, ccre, ccim, are, aim, *x_args)


def _glu_kernel(*refs, residual):
    if residual:
        a_ref, wa_ref, wb_ref, ba_ref, bb_ref, x_ref, gate_ref, o_ref = refs
    else:
        a_ref, wa_ref, wb_ref, ba_ref, bb_ref, o_ref = refs
    a = a_ref[...]
    za = _dot(a, wa_ref[...].astype(BF16)) + ba_ref[...]
    zb = _dot(a, wb_ref[...].astype(BF16)) + bb_ref[...]
    out = za * jax.nn.sigmoid(zb)
    if residual:
        out = x_ref[...] + gate_ref[...] * out.reshape(o_ref.shape)
    o_ref[...] = out


def _glu(tr, a2d, w, b, x3=None, mod=None, layer=None):
    tm, tn = tr.tm, 512
    nb, nj = tm // SUBLANES, D_MODEL // tn
    b3 = b.reshape(1, 1, 2 * D_MODEL)
    residual = x3 is not None
    in_specs = [pl.BlockSpec((tm, D_MODEL), lambda j, i: (i, 0)),
                pl.BlockSpec((None, D_MODEL, tn), lambda j, i: (0, 0, j)),
                pl.BlockSpec((None, D_MODEL, tn), lambda j, i: (0, 0, j + nj)),
                pl.BlockSpec((None, 1, tn), lambda j, i: (0, 0, j)),
                pl.BlockSpec((None, 1, tn), lambda j, i: (0, 0, j + nj))]
    args = (a2d, w, w, b3, b3)
    if residual:
        out_spec = pl.BlockSpec((nb, SUBLANES, tn), lambda j, i: (i, 0, j))
        out_shape = jax.ShapeDtypeStruct(x3.shape, F32)
        in_specs += [out_spec, _mod_spec(tr, layer, 2, tn, lambda j, i: i, lambda j, i: j)]
        args += (x3, mod)
    else:
        out_spec = pl.BlockSpec((tm, tn), lambda j, i: (i, j))
        out_shape = jax.ShapeDtypeStruct(a2d.shape, F32)
    return pl.pallas_call(
        functools.partial(_glu_kernel, residual=residual),
        grid=(nj, a2d.shape[0] // tm),
        in_specs=in_specs,
        out_specs=out_spec,
        out_shape=out_shape,
        compiler_params=_params("parallel", "parallel"),
        name="glu",
    )(*args)


CONV_CHUNK = SUBLANES * SUBLANES
CONV_ROWS = 256


def _conv_weights(w_ref, b_ref, ls):
    w = [jnp.broadcast_to(w_ref[k:k + 1, ls], (SUBLANES, LANES)) for k in range(CONV_WIDTH)]
    return w, jnp.broadcast_to(b_ref[:, ls], (SUBLANES, LANES))


def _conv_taps(win, w, bias):
    acc = [bias] * SUBLANES
    for o in range(CONV_WIDTH + SUBLANES - 1):
        x = win(o)
        for r in range(SUBLANES):
            if 0 <= o - r < CONV_WIDTH:
                acc[r] = acc[r] + w[o - r] * x
    return acc


def _ln_silu_store(cbuf, lg_ref, lb_ref, o_ref):
    n = cbuf.shape[0]
    inv_d = 1.0 / (n * LANES)
    tot = cbuf[0]
    for l in range(1, n):
        tot = tot + cbuf[l]
    mean = jnp.sum(tot, axis=-1, keepdims=True) * inv_d
    sq = jnp.zeros_like(tot)
    for l in range(n):
        xc = cbuf[l] - mean
        sq = sq + xc * xc
    rstd = lax.rsqrt(jnp.sum(sq, axis=-1, keepdims=True) * inv_d + LN_EPS)
    for l in range(n):
        ls = slice(l * LANES, (l + 1) * LANES)
        y = (cbuf[l] - mean) * rstd * lg_ref[:, ls] + lb_ref[:, ls]
        o_ref[:, ls] = jax.nn.silu(y).astype(BF16)


def _conv_ln_kernel(v_ref, halo_ref, w_ref, b_ref, lg_ref, lb_ref, o_ref, pad, cbuf, *, tiles_per_seq):
    rows = v_ref.shape[0]
    first = (pl.program_id(0) % tiles_per_seq) == 0
    for l in range(N_LANE_TILES):
        ls = slice(l * LANES, (l + 1) * LANES)
        pad[l, 0:HALO, :] = jnp.where(first, 0.0, halo_ref[:, ls])
        pad[l, HALO:, :] = v_ref[:, ls]
        w, bias = _conv_weights(w_ref, b_ref, ls)
        for base in range(0, rows, CONV_CHUNK):
            acc = _conv_taps(
                lambda o: pad[l, pl.ds(base + HIST_OFF + o, SUBLANES, stride=SUBLANES), :], w, bias)
            for r in range(SUBLANES):
                cbuf[l, pl.ds(base + r, SUBLANES, stride=SUBLANES), :] = acc[r]
    _ln_silu_store(cbuf, lg_ref, lb_ref, o_ref)


def _conv_ln_long(tr, v2d, w, b, ln_g, ln_b):
    rows = CONV_ROWS
    assert tr.seq_len % rows == 0 and rows % CONV_CHUNK == 0 and rows % HALO == 0
    hb = rows // HALO
    row = pl.BlockSpec((1, D_MODEL), lambda i: (0, 0))
    return pl.pallas_call(
        functools.partial(_conv_ln_kernel, tiles_per_seq=tr.seq_len // rows),
        grid=(v2d.shape[0] // rows,),
        in_specs=[pl.BlockSpec((rows, D_MODEL), lambda i: (i, 0)),
                  pl.BlockSpec((HALO, D_MODEL), lambda i: (jnp.maximum(i * hb - 1, 0), 0)),
                  pl.BlockSpec((None, CONV_WIDTH, D_MODEL), lambda i: (0, 0, 0)),
                  row, row, row],
        out_specs=pl.BlockSpec((rows, D_MODEL), lambda i: (i, 0)),
        out_shape=jax.ShapeDtypeStruct(v2d.shape, BF16),
        scratch_shapes=[pltpu.VMEM((N_LANE_TILES, HALO + rows, LANES), F32),
                        pltpu.VMEM((N_LANE_TILES, rows, LANES), F32)],
        compiler_params=_params("parallel"),
        name="conv_ln_long",
    )(v2d, v2d, w, b, ln_g, ln_b)


def _conv_ln_step_kernel(v_ref, cache_ref, w_ref, b_ref, lg_ref, lb_ref, o_ref, nc_ref, vs, cbuf):
    for l in range(N_LANE_TILES):
        ls = slice(l * LANES, (l + 1) * LANES)
        vs[l] = v_ref[:, ls]
        new = [vs[l, pl.ds(t, SUBLANES, stride=SUBLANES), :] for t in range(SUBLANES)]

        def padded(o, ls=ls, new=new):
            return cache_ref[o, :, ls] if o < CONV_HIST else new[o - CONV_HIST]

        acc = _conv_taps(padded, *_conv_weights(w_ref, b_ref, ls))
        for t in range(SUBLANES):
            cbuf[l, pl.ds(t, SUBLANES, stride=SUBLANES), :] = acc[t]
        for q in range(CONV_HIST):
            nc_ref[q, :, ls] = padded(q + SUBLANES)
    _ln_silu_store(cbuf, lg_ref, lb_ref, o_ref)


def _conv_ln_step(tr, v2d, cache_t, w, b, ln_g, ln_b):
    assert tr.seq_len == SUBLANES
    rows = SUBLANES * tr.seq_len
    row = pl.BlockSpec((1, D_MODEL), lambda s: (0, 0))
    cspec = pl.BlockSpec((CONV_HIST, SUBLANES, D_MODEL), lambda s: (0, s, 0))
    return pl.pallas_call(
        _conv_ln_step_kernel,
        grid=(tr.n_seq // SUBLANES,),
        in_specs=[pl.BlockSpec((rows, D_MODEL), lambda s: (s, 0)), cspec,
                  pl.BlockSpec((None, CONV_WIDTH, D_MODEL), lambda s: (0, 0, 0)),
                  row, row, row],
        out_specs=[pl.BlockSpec((rows, D_MODEL), lambda s: (s, 0)), cspec],
        out_shape=[jax.ShapeDtypeStruct(v2d.shape, BF16), jax.ShapeDtypeStruct(cache_t.shape, F32)],
        scratch_shapes=[pltpu.VMEM((N_LANE_TILES, rows, LANES), F32),
                        pltpu.VMEM((N_LANE_TILES, rows, LANES), F32)],
        compiler_params=_params("parallel"),
        name="conv_ln_step",
    )(v2d, cache_t, w, b, ln_g, ln_b)


MLP_TILE = 1024
MLP_TK = 2048


def _mlp_up_kernel(h_ref, w_ref, o_ref):
    a = jnp.maximum(_dot(h_ref[...], w_ref[...].astype(BF16)), 0.0)
    o_ref[...] = (a * a).astype(BF16)


def _mlp_up(tr, h2d, w1, layer):
    tm, tn = tr.tm, MLP_TILE
    return pl.pallas_call(
        _mlp_up_kernel,
        grid=(D_FF // tn, h2d.shape[0] // tm),
        in_specs=[pl.BlockSpec((tm, D_MODEL), lambda j, i: (i, 0)),
                  pl.BlockSpec((None, D_MODEL, tn), lambda j, i: (layer, 0, j))],
        out_specs=pl.BlockSpec((tm, tn), lambda j, i: (i, j)),
        out_shape=jax.ShapeDtypeStruct((h2d.shape[0], D_FF), BF16),
        compiler_params=_params("parallel", "parallel"),
        name="mlp_up",
    )(h2d, w1)


def _mm_res_kernel(*refs, has_bias, single_k):
    if has_bias:
        a_ref, w_ref, b_ref, x_ref, gate_ref, o_ref = refs
    else:
        a_ref, w_ref, x_ref, gate_ref, o_ref = refs
    def finish(out):
        if has_bias:
            out = out + b_ref[...]
        o_ref[...] = x_ref[...] + gate_ref[...] * out

    def product():
        return _dot(a_ref[...], w_ref[...].astype(BF16)).reshape(o_ref.shape)

    if single_k:
        finish(product())
        return
    k = pl.program_id(2)

    @pl.when(k == 0)
    def _():
        o_ref[...] = jnp.zeros(o_ref.shape, F32)

    o_ref[...] += product()

    @pl.when(k == pl.num_programs(2) - 1)
    def _():
        finish(o_ref[...])


def _mm_res(tr, a2d, w, w_idx, x3, mod, layer, part, bias=None):
    tm, tn = tr.tm, MLP_TILE
    kdim = a2d.shape[1]
    tk = min(kdim, MLP_TK)
    nb = tm // SUBLANES
    xspec = pl.BlockSpec((nb, SUBLANES, tn), lambda j, i, k: (i, 0, j))
    has_bias = bias is not None
    bias_specs = [pl.BlockSpec((1, tn), lambda j, i, k: (0, j))] if has_bias else []
    bias_args = (bias,) if has_bias else ()
    return pl.pallas_call(
        functools.partial(_mm_res_kernel, has_bias=has_bias, single_k=kdim == tk),
        grid=(D_MODEL // tn, a2d.shape[0] // tm, kdim // tk),
        in_specs=[pl.BlockSpec((tm, tk), lambda j, i, k: (i, k)),
                  pl.BlockSpec((None, tk, tn), lambda j, i, k: (w_idx, k, j))] + bias_specs
        + [xspec, _mod_spec(tr, layer, part, tn, lambda j, i, k: i, lambda j, i, k: j)],
        out_specs=xspec,
        out_shape=jax.ShapeDtypeStruct(x3.shape, F32),
        compiler_params=_params("parallel", "parallel", "arbitrary"),
        name="mm_res",
    )(a2d, w, *bias_args, x3, mod)


def _final_norm_kernel(x_ref, g_ref, o_ref):
    x = x_ref[...]
    ms = jnp.mean(x * x, axis=-1, keepdims=True)
    o_ref[...] = x * lax.rsqrt(ms + RMS_EPS) * g_ref[...]


def _final_norm(tr, x3, g):
    nb = min(tr.tm, 512) // SUBLANES
    spec = pl.BlockSpec((nb, SUBLANES, D_MODEL), lambda i: (i, 0, 0))
    return pl.pallas_call(
        _final_norm_kernel,
        grid=(x3.shape[0] // nb,),
        in_specs=[spec, pl.BlockSpec((1, D_MODEL), lambda i: (0, 0))],
        out_specs=spec,
        out_shape=jax.ShapeDtypeStruct(x3.shape, F32),
        compiler_params=_params("parallel"),
        name="final_norm",
    )(x3, g)


def _mlp(tr, x3, g, w1, w2, mod, layer):
    h2d = _prenorm(tr, x3, g, mod, layer, 1, BF16)
    return _mm_res(tr, _mlp_up(tr, h2d, w1, layer), w2, layer, x3, mod, layer, 5)


def _trunk(tr, x, mod, h0, cache, tabs, p):
    tokens = tr.n_seq * tr.seq_len
    x3 = x.reshape(tokens // SUBLANES, SUBLANES, D_MODEL)

    u2d = _prenorm(tr, x3, p["rms_g_mix"], mod, 0, 0, F32)
    g2d, s_re, s_im = _s5(tr, u2d, p["ssm_d"], tabs, h0)
    x3 = _glu(tr, g2d, p["ssm_w_glu"], p["ssm_b_glu"], x3, mod, 0)
    x3 = _mlp(tr, x3, p["rms_g_mlp"], p["mlp_w1"], p["mlp_w2"], mod, 0)

    h2d = _prenorm(tr, x3, p["rms_g_mix"], mod, 1, 0, BF16)
    v2d = _glu(tr, h2d, p["conv_w_pw1"], p["conv_b_pw1"])
    conv_args = (p["conv_w_dw"], p["conv_b_dw"], p["conv_ln_g"], p["conv_ln_b"])
    if cache is None:
        hc2d = _conv_ln_long(tr, v2d, *conv_args)
        new_cache = v2d.reshape(1, tr.n_seq, tr.seq_len, D_MODEL)[:, :, tr.seq_len - CONV_HIST:]
    else:
        hc2d, cache_t = _conv_ln_step(tr, v2d, jnp.transpose(cache[0], (1, 0, 2)), *conv_args)
        new_cache = jnp.transpose(cache_t, (1, 0, 2))[None]
    x3 = _mm_res(tr, hc2d, p["conv_w_pw2"], 0, x3, mod, 1, 2, bias=p["conv_b_pw2"])
    x3 = _mlp(tr, x3, p["rms_g_mlp"], p["mlp_w1"], p["mlp_w2"], mod, 1)
    y3 = _final_norm(tr, x3, p["final_g"])

    state_shape = (1, tr.n_seq, SSM_GROUPS, SSM_STATE)
    return (y3.reshape(tr.n_seq, tr.seq_len, D_MODEL), s_re.reshape(state_shape), s_im.reshape(state_shape),
            new_cache)


def kernel(x_prompt, x_sample, state_ssm_re, state_ssm_im, cache_conv, c_prompt, c_sample, rms_g_mix, rms_g_mlp, w_ada, b_ada, ssm_a_re, ssm_a_im, ssm_log_dt, ssm_b_re, ssm_b_im, ssm_c_re, ssm_c_im, ssm_d, ssm_w_glu, ssm_b_glu, conv_w_pw1, conv_b_pw1, conv_w_dw, conv_b_dw, conv_ln_g, conv_ln_b, conv_w_pw2, conv_b_pw2, mlp_w1, mlp_w2, final_g):
    bp, lp, _ = x_prompt.shape
    bs, ls, _ = x_sample.shape
    assert w_ada.shape[0] == 2 and ssm_a_re.shape[0] == 1 and conv_w_dw.shape[0] == 1
    prompt = Trunk(bp, lp, 1024, mod_row=bs)
    sample = Trunk(bs, ls, bs * ls, mod_row=0)

    depth = w_ada.shape[0]
    p = dict(rms_g_mix=rms_g_mix.reshape(depth, 1, D_MODEL), rms_g_mlp=rms_g_mlp.reshape(depth, 1, D_MODEL),
             ssm_d=ssm_d, ssm_w_glu=ssm_w_glu, ssm_b_glu=ssm_b_glu,
             conv_w_pw1=conv_w_pw1, conv_b_pw1=conv_b_pw1, conv_w_dw=conv_w_dw, conv_b_dw=conv_b_dw,
             conv_ln_g=conv_ln_g, conv_ln_b=conv_ln_b, conv_w_pw2=conv_w_pw2, conv_b_pw2=conv_b_pw2,
             mlp_w1=mlp_w1, mlp_w2=mlp_w2, final_g=final_g.reshape(1, D_MODEL))

    n_c = bp + bs
    pad_rows = -n_c % SUBLANES
    c_all = jnp.concatenate([c_sample, c_prompt, jnp.zeros((pad_rows, D_MODEL), F32)], axis=0)
    mod = _ada(c_all, w_ada, b_ada)

    tabs = _s5_prep(ssm_a_re[0], ssm_a_im[0], ssm_log_dt[0], ssm_b_re[0], ssm_b_im[0],
                    ssm_c_re[0], ssm_c_im[0])

    n_state = SSM_GROUPS * SSM_STATE
    h0 = (state_ssm_re.reshape(bs, n_state), state_ssm_im.reshape(bs, n_state))
    y_p, p_re, p_im, p_buf = _trunk(prompt, x_prompt, mod, None, None, tabs, p)
    y_s, s_re, s_im, s_buf = _trunk(sample, x_sample, mod, h0, cache_conv, tabs, p)
    return (y_p, y_s, p_re, p_im, p_buf, s_re, s_im, s_buf)
```

```python
import collections
import functools

import jax
import jax.numpy as jnp
from jax import lax
from jax.experimental import pallas as pl
from jax.experimental.pallas import tpu as pltpu

F32 = jnp.float32
BF16 = jnp.bfloat16

D_MODEL = 2048
D_FF = 4 * D_MODEL
SSM_GROUP = 16
SSM_GROUPS = D_MODEL // SSM_GROUP
SSM_STATE = 64
LOG2_GROUP = SSM_GROUP.bit_length() - 1
LOG2_STATE = SSM_STATE.bit_length() - 1
assert SSM_GROUP == 1 << LOG2_GROUP and SSM_STATE == 1 << LOG2_STATE
CONV_WIDTH = 31
CONV_HIST = CONV_WIDTH - 1
RMS_EPS = 1e-6
LN_EPS = 1e-5

LANES = 128
SUBLANES = 8
VMEM_LIMIT_BYTES = 56 * 1024 * 1024

GROUPS_PER_TILE = LANES // SSM_GROUP
STATE_TILE = GROUPS_PER_TILE * SSM_STATE
N_LANE_TILES = D_MODEL // LANES
HALO = 32
HIST_OFF = HALO - CONV_HIST

Trunk = collections.namedtuple("Trunk", "n_seq seq_len tm mod_row")


def _params(*sem):
    return pltpu.CompilerParams(dimension_semantics=sem, vmem_limit_bytes=VMEM_LIMIT_BYTES)


def _dot(a, b):
    return jnp.dot(a, b, preferred_element_type=F32)


def _norm_mod(x3, g, sc, sh):
    ms = jnp.mean(x3 * x3, axis=-1, keepdims=True)
    return x3 * lax.rsqrt(ms + RMS_EPS) * (g * (1.0 + sc)) + sh


def _mod_spec(tr, layer, part, tn, ti, tj):
    nblk = D_MODEL // tn
    if tr.seq_len >= tr.tm:
        per = tr.seq_len // tr.tm
        return pl.BlockSpec((None, 1, 1, tn),
                            lambda *g: (layer, tr.mod_row + ti(*g) // per, 0, part * nblk + tj(*g)))
    nbm = tr.tm // SUBLANES
    assert tr.seq_len == SUBLANES and tr.mod_row % nbm == 0
    return pl.BlockSpec((None, nbm, 1, tn),
                        lambda *g: (layer, tr.mod_row // nbm + ti(*g), 0, part * nblk + tj(*g)))


def _ada_kernel(c_ref, w_ref, b_ref, o_ref):
    ca = jax.nn.silu(c_ref[...]).astype(BF16)
    mod = _dot(ca, w_ref[...].astype(BF16)) + b_ref[...]
    for r in range(o_ref.shape[0]):
        o_ref[r] = mod[r:r + 1, :]


def _ada(c_all, w_ada, b_ada):
    depth, d, n = w_ada.shape
    rows = c_all.shape[0]
    tn = 1024
    return pl.pallas_call(
        _ada_kernel,
        grid=(depth, n // tn),
        in_specs=[pl.BlockSpec((rows, d), lambda l, j: (0, 0)),
                  pl.BlockSpec((None, d, tn), lambda l, j: (l, 0, j)),
                  pl.BlockSpec((None, 1, tn), lambda l, j: (l, 0, j))],
        out_specs=pl.BlockSpec((None, rows, 1, tn), lambda l, j: (l, 0, 0, j)),
        out_shape=jax.ShapeDtypeStruct((depth, rows, 1, n), F32),
        compiler_params=_params("parallel", "parallel"),
        name="ada",
    )(c_all, w_ada, b_ada.reshape(depth, 1, n))


def _norm_rows(x3):
    tokens = x3.shape[0] * SUBLANES
    return min(512, tokens // SUBLANES)


def _prenorm_kernel(x_ref, g_ref, sc_ref, sh_ref, o_ref):
    h = _norm_mod(x_ref[...], g_ref[...], sc_ref[...], sh_ref[...])
    o_ref[...] = h.reshape(o_ref.shape).astype(o_ref.dtype)


def _prenorm(tr, x3, g, mod, layer, sublayer, dtype):
    tm = _norm_rows(x3)
    trp = tr._replace(tm=tm)
    nb = tm // SUBLANES
    ti, tj = (lambda i: i), (lambda i: 0)
    return pl.pallas_call(
        _prenorm_kernel,
        grid=(x3.shape[0] // nb,),
        in_specs=[pl.BlockSpec((nb, SUBLANES, D_MODEL), lambda i: (i, 0, 0)),
                  pl.BlockSpec((None, 1, D_MODEL), lambda i: (layer, 0, 0)),
                  _mod_spec(trp, layer, 3 * sublayer + 1, D_MODEL, ti, tj),
                  _mod_spec(trp, layer, 3 * sublayer, D_MODEL, ti, tj)],
        out_specs=pl.BlockSpec((tm, D_MODEL), lambda i: (i, 0)),
        out_shape=jax.ShapeDtypeStruct((x3.shape[0] * SUBLANES, D_MODEL), dtype),
        compiler_params=_params("parallel"),
        name="prenorm",
    )(x3, g, mod, mod)


def _s5_prep_kernel(lre_ref, lim_ref, ldt_ref, bre_ref, bim_ref, cre_ref, cim_ref,
                    are_ref, aim_ref, bbre_ref, bbim_ref, ccre_ref, ccim_ref):
    lr, li = lre_ref[...], lim_ref[...]
    dt = jnp.exp(ldt_ref[...])
    mag = jnp.exp(lr * dt)
    are = mag * jnp.cos(li * dt)
    aim = mag * jnp.sin(li * dt)
    er, ei = are - 1.0, aim
    den = lr * lr + li * li
    qre = (er * lr + ei * li) / den
    qim = (ei * lr - er * li) / den
    are_ref[...] = are
    aim_ref[...] = aim

    br, bi = bre_ref[...], bim_ref[...]
    keep = (jnp.right_shift(lax.broadcasted_iota(jnp.int32, br.shape, 0), LOG2_GROUP)
            == jnp.right_shift(lax.broadcasted_iota(jnp.int32, br.shape, 1), LOG2_STATE))
    bbre_ref[...] = jnp.where(keep, qre * br - qim * bi, 0.0).astype(BF16)
    bbim_ref[...] = jnp.where(keep, qre * bi + qim * br, 0.0).astype(BF16)

    cr, ci = cre_ref[...], cim_ref[...]
    keep = (jnp.right_shift(lax.broadcasted_iota(jnp.int32, cr.shape, 0), LOG2_STATE)
            == jnp.right_shift(lax.broadcasted_iota(jnp.int32, cr.shape, 1), LOG2_GROUP))
    ccre_ref[...] = jnp.where(keep, cr, 0.0).astype(BF16)
    ccim_ref[...] = jnp.where(keep, -ci, 0.0).astype(BF16)


def _s5_prep(a_re, a_im, log_dt, b_re, b_im, c_re, c_im):
    nt = N_LANE_TILES
    tile3 = lambda a: a.reshape(nt, 1, STATE_TILE)
    ldt = jnp.broadcast_to(log_dt[:, None], (SSM_GROUPS, SSM_STATE))
    b_rows = lambda b: jnp.tile(b.transpose(0, 2, 1).reshape(D_MODEL, SSM_STATE), (1, GROUPS_PER_TILE))
    c_rows = lambda c: jnp.tile(c.transpose(0, 2, 1).reshape(SSM_GROUPS * SSM_STATE, SSM_GROUP),
                                (1, GROUPS_PER_TILE))
    vec = pl.BlockSpec((None, 1, STATE_TILE), lambda k: (k, 0, 0))
    bspec = pl.BlockSpec((LANES, STATE_TILE), lambda k: (k, 0))
    cspec = pl.BlockSpec((STATE_TILE, LANES), lambda k: (k, 0))
    return pl.pallas_call(
        _s5_prep_kernel,
        grid=(nt,),
        in_specs=[vec, vec, vec, bspec, bspec, cspec, cspec],
        out_specs=[vec, vec, bspec, bspec, cspec, cspec],
        out_shape=[jax.ShapeDtypeStruct((nt, 1, STATE_TILE), F32)] * 2
        + [jax.ShapeDtypeStruct((D_MODEL, STATE_TILE), BF16)] * 2
        + [jax.ShapeDtypeStruct((SSM_GROUPS * SSM_STATE, LANES), BF16)] * 2,
        compiler_params=_params("parallel"),
        name="s5_prep",
    )(tile3(a_re), tile3(a_im), tile3(ldt), b_rows(b_re), b_rows(b_im), c_rows(c_re), c_rows(c_im))


S5_CHUNK = 512


def _s5_kernel(*refs, n_blocks, seg_len, chained):
    u_ref, d_ref, bbre_ref, bbim_ref, ccre_ref, ccim_ref, are_ref, aim_ref = refs[:8]
    refs = refs[8:]
    if not chained:
        x0_ref, x1_ref = refs[:2]
        refs = refs[2:]
    g_ref, sre_ref, sim_ref, up, gp, gn = refs[:6]
    rows = n_blocks * seg_len * SUBLANES
    n_chunks = rows // S5_CHUNK
    hre, him = refs[6:6 + n_chunks], refs[6 + n_chunks:]

    def natural(lane):
        start = lane * seg_len
        return pl.ds(start if isinstance(lane, int) else pl.multiple_of(start, SUBLANES), seg_len)

    def regrouped(lane):
        return pl.ds((lane // SUBLANES) * (seg_len * SUBLANES) + lane % SUBLANES, seg_len, stride=SUBLANES)

    def for_each_lane(body):
        if n_blocks == 1:
            for lane in range(SUBLANES):
                body(lane, 0)
        else:
            lax.fori_loop(0, n_blocks * SUBLANES, body, 0, unroll=8)

    def regroup(lane, c):
        up[regrouped(lane), :] = u_ref[natural(lane), :]
        return c

    for_each_lane(regroup)

    ar = jnp.broadcast_to(are_ref[...], (SUBLANES, STATE_TILE))
    ai = jnp.broadcast_to(aim_ref[...], (SUBLANES, STATE_TILE))
    d = d_ref[...]
    groups_per_chunk = S5_CHUNK // SUBLANES

    def chunk_rows(q):
        return slice(q * S5_CHUNK, (q + 1) * S5_CHUNK)

    def local_rows(i):
        j = i % groups_per_chunk
        return slice(j * SUBLANES, (j + 1) * SUBLANES)

    def input_chunk(q):
        ub = up[chunk_rows(q), :].astype(BF16)
        hre[q][...] = _dot(ub, bbre_ref[...])
        him[q][...] = _dot(ub, bbim_ref[...])

    def scan_chunk(q, carry):
        for i in range(q * groups_per_chunk, (q + 1) * groups_per_chunk):
            nb, r = i // seg_len, local_rows(i)
            blk = slice(nb * SUBLANES, (nb + 1) * SUBLANES)
            if i % seg_len == 0:
                if chained:
                    carry = (jnp.zeros((SUBLANES, STATE_TILE), F32),) * 2
                else:
                    carry = (x0_ref[blk, :], x1_ref[blk, :])
            hr, hi = carry
            carry = (ar * hr - ai * hi + hre[q][r, :], ar * hi + ai * hr + him[q][r, :])
            hre[q][r, :], him[q][r, :] = carry
            if not chained and i % seg_len == seg_len - 1:
                sre_ref[blk, :], sim_ref[blk, :] = carry
        return carry

    def carry_chunk(q, f):
        for i in range(q * groups_per_chunk, (q + 1) * groups_per_chunk):
            r = local_rows(i)
            fr, fi = f
            hre[q][r, :] = hre[q][r, :] + fr
            him[q][r, :] = him[q][r, :] + fi
            f = (ar * fr - ai * fi, ar * fi + ai * fr)
        return f

    def output_chunk(q):
        y = _dot(hre[q][...].astype(BF16), ccre_ref[...]) + _dot(him[q][...].astype(BF16), ccim_ref[...])
        gp[chunk_rows(q), :] = jax.nn.gelu(y + d * up[chunk_rows(q), :])

    input_chunk(0)
    carry = None
    for q in range(n_chunks):
        if q + 1 < n_chunks:
            input_chunk(q + 1)
        carry = scan_chunk(q, carry)
        if not chained and q >= 1:
            output_chunk(q - 1)

    if chained:
        er, ei = carry
        pr, pi = are_ref[...], aim_ref[...]
        for _ in range(seg_len.bit_length() - 1):
            pr, pi = pr * pr - pi * pi, 2.0 * (pr * pi)
        row = lax.broadcasted_iota(jnp.int32, (SUBLANES, STATE_TILE), 0)
        xr = jnp.zeros((SUBLANES, STATE_TILE), F32)
        xi = xr
        for _ in range(SUBLANES - 1):
            yr = er + pr * xr - pi * xi
            yi = ei + pr * xi + pi * xr
            xr = jnp.where(row == 0, 0.0, pltpu.roll(yr, 1, 0))
            xi = jnp.where(row == 0, 0.0, pltpu.roll(yi, 1, 0))
        f = (ar * xr - ai * xi, ar * xi + ai * xr)
        for q in range(n_chunks):
            f = carry_chunk(q, f)
            if q >= 1:
                output_chunk(q - 1)
        last = n_chunks - 1
        sre_ref[...] = hre[last][S5_CHUNK - 1:S5_CHUNK, :]
        sim_ref[...] = him[last][S5_CHUNK - 1:S5_CHUNK, :]
    output_chunk(n_chunks - 1)

    def ungroup(lane, c):
        gn[natural(lane), :] = gp[regrouped(lane), :]
        return c

    for_each_lane(ungroup)
    g_ref[...] = gn[...].astype(BF16)


def _s5(tr, u2d, d_skip, tabs, h0):
    are, aim, bbre, bbim, ccre, ccim = tabs
    chained = h0 is None
    if chained:
        n_batch, rows = tr.n_seq, tr.seq_len
        n_blocks, seg_len = 1, tr.seq_len // SUBLANES
        assert seg_len == 1 << (seg_len.bit_length() - 1)
        x_specs, x_args = [], ()
        st_spec = pl.BlockSpec((None, 1, STATE_TILE), lambda k, b: (b, 0, k))
        st_shape = jax.ShapeDtypeStruct((tr.n_seq, 1, SSM_GROUPS * SSM_STATE), F32)
    else:
        n_batch, rows = 1, tr.n_seq * tr.seq_len
        n_blocks, seg_len = tr.n_seq // SUBLANES, tr.seq_len
        x_specs = [pl.BlockSpec((tr.n_seq, STATE_TILE), lambda k, b: (0, k))] * 2
        x_args = h0
        st_spec = pl.BlockSpec((tr.n_seq, STATE_TILE), lambda k, b: (0, k))
        st_shape = jax.ShapeDtypeStruct((tr.n_seq, SSM_GROUPS * SSM_STATE), F32)
    assert rows % S5_CHUNK == 0
    vec = pl.BlockSpec((None, 1, STATE_TILE), lambda k, b: (k, 0, 0))
    bspec = pl.BlockSpec((LANES, STATE_TILE), lambda k, b: (k, 0))
    cspec = pl.BlockSpec((STATE_TILE, LANES), lambda k, b: (k, 0))
    tok = pl.BlockSpec((rows, LANES), lambda k, b: (b, k))
    return pl.pallas_call(
        functools.partial(_s5_kernel, n_blocks=n_blocks, seg_len=seg_len, chained=chained),
        grid=(N_LANE_TILES, n_batch),
        in_specs=[tok, pl.BlockSpec((1, LANES), lambda k, b: (0, k)), bspec, bspec, cspec, cspec, vec, vec]
        + x_specs,
        out_specs=[tok, st_spec, st_spec],
        out_shape=[jax.ShapeDtypeStruct(u2d.shape, BF16), st_shape, st_shape],
        scratch_shapes=[pltpu.VMEM((rows, LANES), F32)] * 3
        + [pltpu.VMEM((S5_CHUNK, STATE_TILE), F32)] * (2 * (rows // S5_CHUNK)),
        compiler_params=_params("parallel", "parallel"),
        name="s5",
    )(u2d, d_skip, bbre, bbim, ccre, ccim, are, aim, *x_args)


GLU_COLS = 256


def _glu_kernel(*refs, residual):
    if residual:
        a_ref, wa_ref, wb_ref, ba_ref, bb_ref, x_ref, gate_ref, o_ref = refs
    else:
        a_ref, wa_ref, wb_ref, ba_ref, bb_ref, o_ref = refs
    a = a_ref[...]
    tn = wa_ref.shape[1]
    for c in range(0, tn, GLU_COLS):
        cs = slice(c, c + GLU_COLS)
        za = _dot(a, wa_ref[:, cs].astype(BF16)) + ba_ref[:, cs]
        zb = _dot(a, wb_ref[:, cs].astype(BF16)) + bb_ref[:, cs]
        out = za * jax.nn.sigmoid(zb)
        if residual:
            o_ref[:, :, cs] = x_ref[:, :, cs] + gate_ref[:, :, cs] * out.reshape(o_ref.shape[:2] + (GLU_COLS,))
        else:
            o_ref[:, cs] = out


def _glu(tr, a2d, w, b, x3=None, mod=None, layer=None):
    tm, tn = tr.tm, 512
    nb, nj = tm // SUBLANES, D_MODEL // tn
    b3 = b.reshape(1, 1, 2 * D_MODEL)
    residual = x3 is not None
    in_specs = [pl.BlockSpec((tm, D_MODEL), lambda j, i: (i, 0)),
                pl.BlockSpec((None, D_MODEL, tn), lambda j, i: (0, 0, j)),
                pl.BlockSpec((None, D_MODEL, tn), lambda j, i: (0, 0, j + nj)),
                pl.BlockSpec((None, 1, tn), lambda j, i: (0, 0, j)),
                pl.BlockSpec((None, 1, tn), lambda j, i: (0, 0, j + nj))]
    args = (a2d, w, w, b3, b3)
    if residual:
        out_spec = pl.BlockSpec((nb, SUBLANES, tn), lambda j, i: (i, 0, j))
        out_shape = jax.ShapeDtypeStruct(x3.shape, F32)
        in_specs += [out_spec, _mod_spec(tr, layer, 2, tn, lambda j, i: i, lambda j, i: j)]
        args += (x3, mod)
    else:
        out_spec = pl.BlockSpec((tm, tn), lambda j, i: (i, j))
        out_shape = jax.ShapeDtypeStruct(a2d.shape, F32)
    return pl.pallas_call(
        functools.partial(_glu_kernel, residual=residual),
        grid=(nj, a2d.shape[0] // tm),
        in_specs=in_specs,
        out_specs=out_spec,
        out_shape=out_shape,
        compiler_params=_params("parallel", "parallel"),
        name="glu",
    )(*args)


CONV_CHUNK = SUBLANES * SUBLANES
CONV_ROWS = 256


def _conv_weights(w_ref, b_ref, ls):
    w = [jnp.broadcast_to(w_ref[k:k + 1, ls], (SUBLANES, LANES)) for k in range(CONV_WIDTH)]
    return w, jnp.broadcast_to(b_ref[:, ls], (SUBLANES, LANES))


def _conv_taps(win, w, bias):
    acc = [bias] * SUBLANES
    for o in range(CONV_WIDTH + SUBLANES - 1):
        x = win(o)
        for r in range(SUBLANES):
            if 0 <= o - r < CONV_WIDTH:
                acc[r] = acc[r] + w[o - r] * x
    return acc


def _ln_silu_store(cbuf, lg_ref, lb_ref, o_ref):
    n = cbuf.shape[0]
    inv_d = 1.0 / (n * LANES)
    tot = cbuf[0]
    for l in range(1, n):
        tot = tot + cbuf[l]
    mean = jnp.sum(tot, axis=-1, keepdims=True) * inv_d
    sq = jnp.zeros_like(tot)
    for l in range(n):
        xc = cbuf[l] - mean
        sq = sq + xc * xc
    rstd = lax.rsqrt(jnp.sum(sq, axis=-1, keepdims=True) * inv_d + LN_EPS)
    for l in range(n):
        ls = slice(l * LANES, (l + 1) * LANES)
        y = (cbuf[l] - mean) * rstd * lg_ref[:, ls] + lb_ref[:, ls]
        o_ref[:, ls] = jax.nn.silu(y).astype(BF16)


def _conv_ln_kernel(v_ref, halo_ref, w_ref, b_ref, lg_ref, lb_ref, o_ref, pad, cbuf, *, tiles_per_seq):
    rows = v_ref.shape[0]
    first = (pl.program_id(0) % tiles_per_seq) == 0
    for l in range(N_LANE_TILES):
        ls = slice(l * LANES, (l + 1) * LANES)
        pad[l, 0:HALO, :] = jnp.where(first, 0.0, halo_ref[:, ls])
        pad[l, HALO:, :] = v_ref[:, ls]
        w, bias = _conv_weights(w_ref, b_ref, ls)
        for base in range(0, rows, CONV_CHUNK):
            acc = _conv_taps(
                lambda o: pad[l, pl.ds(base + HIST_OFF + o, SUBLANES, stride=SUBLANES), :], w, bias)
            for r in range(SUBLANES):
                cbuf[l, pl.ds(base + r, SUBLANES, stride=SUBLANES), :] = acc[r]
    _ln_silu_store(cbuf, lg_ref, lb_ref, o_ref)


def _conv_ln_long(tr, v2d, w, b, ln_g, ln_b):
    rows = CONV_ROWS
    assert tr.seq_len % rows == 0 and rows % CONV_CHUNK == 0 and rows % HALO == 0
    hb = rows // HALO
    row = pl.BlockSpec((1, D_MODEL), lambda i: (0, 0))
    return pl.pallas_call(
        functools.partial(_conv_ln_kernel, tiles_per_seq=tr.seq_len // rows),
        grid=(v2d.shape[0] // rows,),
        in_specs=[pl.BlockSpec((rows, D_MODEL), lambda i: (i, 0)),
                  pl.BlockSpec((HALO, D_MODEL), lambda i: (jnp.maximum(i * hb - 1, 0), 0)),
                  pl.BlockSpec((None, CONV_WIDTH, D_MODEL), lambda i: (0, 0, 0)),
                  row, row, row],
        out_specs=pl.BlockSpec((rows, D_MODEL), lambda i: (i, 0)),
        out_shape=jax.ShapeDtypeStruct(v2d.shape, BF16),
        scratch_shapes=[pltpu.VMEM((N_LANE_TILES, HALO + rows, LANES), F32),
                        pltpu.VMEM((N_LANE_TILES, rows, LANES), F32)],
        compiler_params=_params("parallel"),
        name="conv_ln_long",
    )(v2d, v2d, w, b, ln_g, ln_b)


def _conv_ln_step_kernel(v_ref, cache_ref, w_ref, b_ref, lg_ref, lb_ref, o_ref, nc_ref, vs, cbuf):
    for l in range(N_LANE_TILES):
        ls = slice(l * LANES, (l + 1) * LANES)
        vs[l] = v_ref[:, ls]
        new = [vs[l, pl.ds(t, SUBLANES, stride=SUBLANES), :] for t in range(SUBLANES)]

        def padded(o, ls=ls, new=new):
            return cache_ref[o, :, ls] if o < CONV_HIST else new[o - CONV_HIST]

        acc = _conv_taps(padded, *_conv_weights(w_ref, b_ref, ls))
        for t in range(SUBLANES):
            cbuf[l, pl.ds(t, SUBLANES, stride=SUBLANES), :] = acc[t]
        for q in range(CONV_HIST):
            nc_ref[q, :, ls] = padded(q + SUBLANES)
    _ln_silu_store(cbuf, lg_ref, lb_ref, o_ref)


def _conv_ln_step(tr, v2d, cache_t, w, b, ln_g, ln_b):
    assert tr.seq_len == SUBLANES
    rows = SUBLANES * tr.seq_len
    row = pl.BlockSpec((1, D_MODEL), lambda s: (0, 0))
    cspec = pl.BlockSpec((CONV_HIST, SUBLANES, D_MODEL), lambda s: (0, s, 0))
    return pl.pallas_call(
        _conv_ln_step_kernel,
        grid=(tr.n_seq // SUBLANES,),
        in_specs=[pl.BlockSpec((rows, D_MODEL), lambda s: (s, 0)), cspec,
                  pl.BlockSpec((None, CONV_WIDTH, D_MODEL), lambda s: (0, 0, 0)),
                  row, row, row],
        out_specs=[pl.BlockSpec((rows, D_MODEL), lambda s: (s, 0)), cspec],
        out_shape=[jax.ShapeDtypeStruct(v2d.shape, BF16), jax.ShapeDtypeStruct(cache_t.shape, F32)],
        scratch_shapes=[pltpu.VMEM((N_LANE_TILES, rows, LANES), F32),
                        pltpu.VMEM((N_LANE_TILES, rows, LANES), F32)],
        compiler_params=_params("parallel"),
        name="conv_ln_step",
    )(v2d, cache_t, w, b, ln_g, ln_b)


MLP_TILE = 1024
MLP_TK = 2048


def _mlp_up_kernel(h_ref, w_ref, o_ref):
    a = jnp.maximum(_dot(h_ref[...], w_ref[...].astype(BF16)), 0.0)
    o_ref[...] = (a * a).astype(BF16)


def _mlp_up(tr, h2d, w1, layer):
    tm, tn = tr.tm, MLP_TILE
    return pl.pallas_call(
        _mlp_up_kernel,
        grid=(D_FF // tn, h2d.shape[0] // tm),
        in_specs=[pl.BlockSpec((tm, D_MODEL), lambda j, i: (i, 0)),
                  pl.BlockSpec((None, D_MODEL, tn), lambda j, i: (layer, 0, j))],
        out_specs=pl.BlockSpec((tm, tn), lambda j, i: (i, j)),
        out_shape=jax.ShapeDtypeStruct((h2d.shape[0], D_FF), BF16),
        compiler_params=_params("parallel", "parallel"),
        name="mlp_up",
    )(h2d, w1)


def _mm_res_kernel(*refs, has_bias, single_k):
    if has_bias:
        a_ref, w_ref, b_ref, x_ref, gate_ref, o_ref = refs
    else:
        a_ref, w_ref, x_ref, gate_ref, o_ref = refs
    def finish(out):
        if has_bias:
            out = out + b_ref[...]
        o_ref[...] = x_ref[...] + gate_ref[...] * out

    def product():
        return _dot(a_ref[...], w_ref[...].astype(BF16)).reshape(o_ref.shape)

    if single_k:
        finish(product())
        return
    k = pl.program_id(2)

    @pl.when(k == 0)
    def _():
        o_ref[...] = jnp.zeros(o_ref.shape, F32)

    o_ref[...] += product()

    @pl.when(k == pl.num_programs(2) - 1)
    def _():
        finish(o_ref[...])


def _mm_res(tr, a2d, w, w_idx, x3, mod, layer, part, bias=None):
    tm, tn = tr.tm, MLP_TILE
    kdim = a2d.shape[1]
    tk = min(kdim, MLP_TK)
    nb = tm // SUBLANES
    xspec = pl.BlockSpec((nb, SUBLANES, tn), lambda j, i, k: (i, 0, j))
    has_bias = bias is not None
    bias_specs = [pl.BlockSpec((1, tn), lambda j, i, k: (0, j))] if has_bias else []
    bias_args = (bias,) if has_bias else ()
    return pl.pallas_call(
        functools.partial(_mm_res_kernel, has_bias=has_bias, single_k=kdim == tk),
        grid=(D_MODEL // tn, a2d.shape[0] // tm, kdim // tk),
        in_specs=[pl.BlockSpec((tm, tk), lambda j, i, k: (i, k)),
                  pl.BlockSpec((None, tk, tn), lambda j, i, k: (w_idx, k, j))] + bias_specs
        + [xspec, _mod_spec(tr, layer, part, tn, lambda j, i, k: i, lambda j, i, k: j)],
        out_specs=xspec,
        out_shape=jax.ShapeDtypeStruct(x3.shape, F32),
        compiler_params=_params("parallel", "parallel", "arbitrary"),
        name="mm_res",
    )(a2d, w, *bias_args, x3, mod)


def _final_norm_kernel(x_ref, g_ref, o_ref):
    x = x_ref[...]
    ms = jnp.mean(x * x, axis=-1, keepdims=True)
    o_ref[...] = x * lax.rsqrt(ms + RMS_EPS) * g_ref[...]


def _final_norm(tr, x3, g):
    nb = _norm_rows(x3) // SUBLANES
    spec = pl.BlockSpec((nb, SUBLANES, D_MODEL), lambda i: (i, 0, 0))
    return pl.pallas_call(
        _final_norm_kernel,
        grid=(x3.shape[0] // nb,),
        in_specs=[spec, pl.BlockSpec((1, D_MODEL), lambda i: (0, 0))],
        out_specs=spec,
        out_shape=jax.ShapeDtypeStruct(x3.shape, F32),
        compiler_params=_params("parallel"),
        name="final_norm",
    )(x3, g)


def _mlp(tr, x3, g, w1, w2, mod, layer):
    h2d = _prenorm(tr, x3, g, mod, layer, 1, BF16)
    return _mm_res(tr, _mlp_up(tr, h2d, w1, layer), w2, layer, x3, mod, layer, 5)


def _trunk(tr, x, mod, h0, cache, tabs, p):
    tokens = tr.n_seq * tr.seq_len
    x3 = x.reshape(tokens // SUBLANES, SUBLANES, D_MODEL)

    u2d = _prenorm(tr, x3, p["rms_g_mix"], mod, 0, 0, F32)
    g2d, s_re, s_im = _s5(tr, u2d, p["ssm_d"], tabs, h0)
    x3 = _glu(tr, g2d, p["ssm_w_glu"], p["ssm_b_glu"], x3, mod, 0)
    x3 = _mlp(tr, x3, p["rms_g_mlp"], p["mlp_w1"], p["mlp_w2"], mod, 0)

    h2d = _prenorm(tr, x3, p["rms_g_mix"], mod, 1, 0, BF16)
    v2d = _glu(tr, h2d, p["conv_w_pw1"], p["conv_b_pw1"])
    conv_args = (p["conv_w_dw"], p["conv_b_dw"], p["conv_ln_g"], p["conv_ln_b"])
    if cache is None:
        hc2d = _conv_ln_long(tr, v2d, *conv_args)
        new_cache = v2d.reshape(1, tr.n_seq, tr.seq_len, D_MODEL)[:, :, tr.seq_len - CONV_HIST:]
    else:
        hc2d, cache_t = _conv_ln_step(tr, v2d, jnp.transpose(cache[0], (1, 0, 2)), *conv_args)
        new_cache = jnp.transpose(cache_t, (1, 0, 2))[None]
    x3 = _mm_res(tr, hc2d, p["conv_w_pw2"], 0, x3, mod, 1, 2, bias=p["conv_b_pw2"])
    x3 = _mlp(tr, x3, p["rms_g_mlp"], p["mlp_w1"], p["mlp_w2"], mod, 1)
    y3 = _final_norm(tr, x3, p["final_g"])

    state_shape = (1, tr.n_seq, SSM_GROUPS, SSM_STATE)
    return (y3.reshape(tr.n_seq, tr.seq_len, D_MODEL), s_re.reshape(state_shape), s_im.reshape(state_shape),
            new_cache)


def kernel(x_prompt, x_sample, state_ssm_re, state_ssm_im, cache_conv, c_prompt, c_sample, rms_g_mix, rms_g_mlp, w_ada, b_ada, ssm_a_re, ssm_a_im, ssm_log_dt, ssm_b_re, ssm_b_im, ssm_c_re, ssm_c_im, ssm_d, ssm_w_glu, ssm_b_glu, conv_w_pw1, conv_b_pw1, conv_w_dw, conv_b_dw, conv_ln_g, conv_ln_b, conv_w_pw2, conv_b_pw2, mlp_w1, mlp_w2, final_g):
    bp, lp, _ = x_prompt.shape
    bs, ls, _ = x_sample.shape
    assert w_ada.shape[0] == 2 and ssm_a_re.shape[0] == 1 and conv_w_dw.shape[0] == 1
    prompt = Trunk(bp, lp, 1024, mod_row=bs)
    sample = Trunk(bs, ls, bs * ls, mod_row=0)

    depth = w_ada.shape[0]
    p = dict(rms_g_mix=rms_g_mix.reshape(depth, 1, D_MODEL), rms_g_mlp=rms_g_mlp.reshape(depth, 1, D_MODEL),
             ssm_d=ssm_d, ssm_w_glu=ssm_w_glu, ssm_b_glu=ssm_b_glu,
             conv_w_pw1=conv_w_pw1, conv_b_pw1=conv_b_pw1, conv_w_dw=conv_w_dw, conv_b_dw=conv_b_dw,
             conv_ln_g=conv_ln_g, conv_ln_b=conv_ln_b, conv_w_pw2=conv_w_pw2, conv_b_pw2=conv_b_pw2,
             mlp_w1=mlp_w1, mlp_w2=mlp_w2, final_g=final_g.reshape(1, D_MODEL))

    n_c = bp + bs
    pad_rows = -n_c % SUBLANES
    c_all = jnp.concatenate([c_sample, c_prompt, jnp.zeros((pad_rows, D_MODEL), F32)], axis=0)
    mod = _ada(c_all, w_ada, b_ada)

    tabs = _s5_prep(ssm_a_re[0], ssm_a_im[0], ssm_log_dt[0], ssm_b_re[0], ssm_b_im[0],
                    ssm_c_re[0], ssm_c_im[0])

    n_state = SSM_GROUPS * SSM_STATE
    h0 = (state_ssm_re.reshape(bs, n_state), state_ssm_im.reshape(bs, n_state))
    y_p, p_re, p_im, p_buf = _trunk(prompt, x_prompt, mod, None, None, tabs, p)
    y_s, s_re, s_im, s_buf = _trunk(sample, x_sample, mod, h0, cache_conv, tabs, p)
    return (y_p, y_s, p_re, p_im, p_buf, s_re, s_im, s_buf)
```

```python
import collections
import functools

import jax
import jax.numpy as jnp
from jax import lax
from jax.experimental import pallas as pl
from jax.experimental.pallas import tpu as pltpu

F32 = jnp.float32
BF16 = jnp.bfloat16

D_MODEL = 2048
D_FF = 4 * D_MODEL
SSM_GROUP = 16
SSM_GROUPS = D_MODEL // SSM_GROUP
SSM_STATE = 64
LOG2_GROUP = SSM_GROUP.bit_length() - 1
LOG2_STATE = SSM_STATE.bit_length() - 1
assert SSM_GROUP == 1 << LOG2_GROUP and SSM_STATE == 1 << LOG2_STATE
CONV_WIDTH = 31
CONV_HIST = CONV_WIDTH - 1
RMS_EPS = 1e-6
LN_EPS = 1e-5

LANES = 128
SUBLANES = 8
VMEM_LIMIT_BYTES = 56 * 1024 * 1024

GROUPS_PER_TILE = LANES // SSM_GROUP
STATE_TILE = GROUPS_PER_TILE * SSM_STATE
N_LANE_TILES = D_MODEL // LANES
HALO = 32
HIST_OFF = HALO - CONV_HIST

Trunk = collections.namedtuple("Trunk", "n_seq seq_len tm mod_row")


def _params(*sem):
    return pltpu.CompilerParams(dimension_semantics=sem, vmem_limit_bytes=VMEM_LIMIT_BYTES)


def _dot(a, b):
    return jnp.dot(a, b, preferred_element_type=F32)


def _norm_mod(x3, g, sc, sh):
    ms = jnp.mean(x3 * x3, axis=-1, keepdims=True)
    return x3 * lax.rsqrt(ms + RMS_EPS) * (g * (1.0 + sc)) + sh


def _mod_spec(tr, layer, part, tn, ti, tj):
    nblk = D_MODEL // tn
    if tr.seq_len >= tr.tm:
        per = tr.seq_len // tr.tm
        return pl.BlockSpec((None, 1, 1, tn),
                            lambda *g: (layer, tr.mod_row + ti(*g) // per, 0, part * nblk + tj(*g)))
    nbm = tr.tm // SUBLANES
    assert tr.seq_len == SUBLANES and tr.mod_row % nbm == 0
    return pl.BlockSpec((None, nbm, 1, tn),
                        lambda *g: (layer, tr.mod_row // nbm + ti(*g), 0, part * nblk + tj(*g)))


def _ada_kernel(c_ref, w_ref, b_ref, o_ref):
    ca = jax.nn.silu(c_ref[...]).astype(BF16)
    mod = _dot(ca, w_ref[...].astype(BF16)) + b_ref[...]
    for r in range(o_ref.shape[0]):
        o_ref[r] = mod[r:r + 1, :]


def _ada(c_all, w_ada, b_ada):
    depth, d, n = w_ada.shape
    rows = c_all.shape[0]
    tn = 1024
    return pl.pallas_call(
        _ada_kernel,
        grid=(depth, n // tn),
        in_specs=[pl.BlockSpec((rows, d), lambda l, j: (0, 0)),
                  pl.BlockSpec((None, d, tn), lambda l, j: (l, 0, j)),
                  pl.BlockSpec((None, 1, tn), lambda l, j: (l, 0, j))],
        out_specs=pl.BlockSpec((None, rows, 1, tn), lambda l, j: (l, 0, 0, j)),
        out_shape=jax.ShapeDtypeStruct((depth, rows, 1, n), F32),
        compiler_params=_params("parallel", "parallel"),
        name="ada",
    )(c_all, w_ada, b_ada.reshape(depth, 1, n))


def _norm_rows(x3):
    tokens = x3.shape[0] * SUBLANES
    return min(512, tokens // SUBLANES)


def _prenorm_kernel(x_ref, g_ref, sc_ref, sh_ref, o_ref):
    h = _norm_mod(x_ref[...], g_ref[...], sc_ref[...], sh_ref[...])
    o_ref[...] = h.reshape(o_ref.shape).astype(o_ref.dtype)


def _prenorm(tr, x3, g, mod, layer, sublayer, dtype):
    tm = _norm_rows(x3)
    trp = tr._replace(tm=tm)
    nb = tm // SUBLANES
    ti, tj = (lambda i: i), (lambda i: 0)
    return pl.pallas_call(
        _prenorm_kernel,
        grid=(x3.shape[0] // nb,),
        in_specs=[pl.BlockSpec((nb, SUBLANES, D_MODEL), lambda i: (i, 0, 0)),
                  pl.BlockSpec((None, 1, D_MODEL), lambda i: (layer, 0, 0)),
                  _mod_spec(trp, layer, 3 * sublayer + 1, D_MODEL, ti, tj),
                  _mod_spec(trp, layer, 3 * sublayer, D_MODEL, ti, tj)],
        out_specs=pl.BlockSpec((tm, D_MODEL), lambda i: (i, 0)),
        out_shape=jax.ShapeDtypeStruct((x3.shape[0] * SUBLANES, D_MODEL), dtype),
        compiler_params=_params("parallel"),
        name="prenorm",
    )(x3, g, mod, mod)


def _s5_prep_kernel(lre_ref, lim_ref, ldt_ref, bre_ref, bim_ref, cre_ref, cim_ref,
                    are_ref, aim_ref, bbre_ref, bbim_ref, ccre_ref, ccim_ref):
    lr, li = lre_ref[...], lim_ref[...]
    dt = jnp.exp(ldt_ref[...])
    mag = jnp.exp(lr * dt)
    are = mag * jnp.cos(li * dt)
    aim = mag * jnp.sin(li * dt)
    er, ei = are - 1.0, aim
    den = lr * lr + li * li
    qre = (er * lr + ei * li) / den
    qim = (ei * lr - er * li) / den
    are_ref[...] = are
    aim_ref[...] = aim

    br, bi = bre_ref[...], bim_ref[...]
    keep = (jnp.right_shift(lax.broadcasted_iota(jnp.int32, br.shape, 0), LOG2_GROUP)
            == jnp.right_shift(lax.broadcasted_iota(jnp.int32, br.shape, 1), LOG2_STATE))
    bbre_ref[...] = jnp.where(keep, qre * br - qim * bi, 0.0).astype(BF16)
    bbim_ref[...] = jnp.where(keep, qre * bi + qim * br, 0.0).astype(BF16)

    cr, ci = cre_ref[...], cim_ref[...]
    keep = (jnp.right_shift(lax.broadcasted_iota(jnp.int32, cr.shape, 0), LOG2_STATE)
            == jnp.right_shift(lax.broadcasted_iota(jnp.int32, cr.shape, 1), LOG2_GROUP))
    ccre_ref[...] = jnp.where(keep, cr, 0.0).astype(BF16)
    ccim_ref[...] = jnp.where(keep, -ci, 0.0).astype(BF16)


def _s5_prep(a_re, a_im, log_dt, b_re, b_im, c_re, c_im):
    nt = N_LANE_TILES
    tile3 = lambda a: a.reshape(nt, 1, STATE_TILE)
    ldt = jnp.broadcast_to(log_dt[:, None], (SSM_GROUPS, SSM_STATE))
    b_rows = lambda b: jnp.tile(b.transpose(0, 2, 1).reshape(D_MODEL, SSM_STATE), (1, GROUPS_PER_TILE))
    c_rows = lambda c: jnp.tile(c.transpose(0, 2, 1).reshape(SSM_GROUPS * SSM_STATE, SSM_GROUP),
                                (1, GROUPS_PER_TILE))
    vec = pl.BlockSpec((None, 1, STATE_TILE), lambda k: (k, 0, 0))
    bspec = pl.BlockSpec((LANES, STATE_TILE), lambda k: (k, 0))
    cspec = pl.BlockSpec((STATE_TILE, LANES), lambda k: (k, 0))
    return pl.pallas_call(
        _s5_prep_kernel,
        grid=(nt,),
        in_specs=[vec, vec, vec, bspec, bspec, cspec, cspec],
        out_specs=[vec, vec, bspec, bspec, cspec, cspec],
        out_shape=[jax.ShapeDtypeStruct((nt, 1, STATE_TILE), F32)] * 2
        + [jax.ShapeDtypeStruct((D_MODEL, STATE_TILE), BF16)] * 2
        + [jax.ShapeDtypeStruct((SSM_GROUPS * SSM_STATE, LANES), BF16)] * 2,
        compiler_params=_params("parallel"),
        name="s5_prep",
    )(tile3(a_re), tile3(a_im), tile3(ldt), b_rows(b_re), b_rows(b_im), c_rows(c_re), c_rows(c_im))


S5_CHUNK = 256


def _s5_step_kernel(u_ref, d_ref, bbre_ref, bbim_ref, ccre_ref, ccim_ref, are_ref, aim_ref, x0_ref, x1_ref,
                    g_ref, sre_ref, sim_ref, up, gp, gn, *chunks, n_blocks, seg_len):
    n_chunks = len(chunks) // 2
    hre, him = chunks[:n_chunks], chunks[n_chunks:]

    def natural(lane):
        return pl.ds(pl.multiple_of(lane * seg_len, SUBLANES), seg_len)

    def regrouped(lane):
        return pl.ds((lane // SUBLANES) * (seg_len * SUBLANES) + lane % SUBLANES, seg_len, stride=SUBLANES)

    def for_each_lane(body):
        lax.fori_loop(0, n_blocks * SUBLANES, body, 0, unroll=8)

    def regroup(lane, c):
        up[regrouped(lane), :] = u_ref[natural(lane), :]
        return c

    for_each_lane(regroup)

    ar = jnp.broadcast_to(are_ref[...], (SUBLANES, STATE_TILE))
    ai = jnp.broadcast_to(aim_ref[...], (SUBLANES, STATE_TILE))
    d = d_ref[...]
    groups_per_chunk = S5_CHUNK // SUBLANES

    def chunk_rows(q):
        return slice(q * S5_CHUNK, (q + 1) * S5_CHUNK)

    def local_rows(i):
        j = i % groups_per_chunk
        return slice(j * SUBLANES, (j + 1) * SUBLANES)

    def input_chunk(q):
        ub = up[chunk_rows(q), :].astype(BF16)
        hre[q][...] = _dot(ub, bbre_ref[...])
        him[q][...] = _dot(ub, bbim_ref[...])

    def scan_chunk(q, carry):
        for i in range(q * groups_per_chunk, (q + 1) * groups_per_chunk):
            nb, r = i // seg_len, local_rows(i)
            blk = slice(nb * SUBLANES, (nb + 1) * SUBLANES)
            if i % seg_len == 0:
                carry = (x0_ref[blk, :], x1_ref[blk, :])
            hr, hi = carry
            carry = (ar * hr - ai * hi + hre[q][r, :], ar * hi + ai * hr + him[q][r, :])
            hre[q][r, :], him[q][r, :] = carry
            if i % seg_len == seg_len - 1:
                sre_ref[blk, :], sim_ref[blk, :] = carry
        return carry

    def output_chunk(q):
        y = _dot(hre[q][...].astype(BF16), ccre_ref[...]) + _dot(him[q][...].astype(BF16), ccim_ref[...])
        gp[chunk_rows(q), :] = jax.nn.gelu(y + d * up[chunk_rows(q), :])

    input_chunk(0)
    carry = None
    for q in range(n_chunks):
        if q + 1 < n_chunks:
            input_chunk(q + 1)
        carry = scan_chunk(q, carry)
        if q >= 1:
            output_chunk(q - 1)
    output_chunk(n_chunks - 1)

    def ungroup(lane, c):
        gn[natural(lane), :] = gp[regrouped(lane), :]
        return c

    for_each_lane(ungroup)
    g_ref[...] = gn[...].astype(BF16)


def _s5_step(tr, u2d, d_skip, tabs, h0):
    are, aim, bbre, bbim, ccre, ccim = tabs
    rows = tr.n_seq * tr.seq_len
    assert rows % S5_CHUNK == 0 and tr.n_seq % SUBLANES == 0 and tr.seq_len % SUBLANES == 0
    vec = pl.BlockSpec((None, 1, STATE_TILE), lambda k: (k, 0, 0))
    bspec = pl.BlockSpec((LANES, STATE_TILE), lambda k: (k, 0))
    cspec = pl.BlockSpec((STATE_TILE, LANES), lambda k: (k, 0))
    tok = pl.BlockSpec((rows, LANES), lambda k: (0, k))
    st_spec = pl.BlockSpec((tr.n_seq, STATE_TILE), lambda k: (0, k))
    st_shape = jax.ShapeDtypeStruct((tr.n_seq, SSM_GROUPS * SSM_STATE), F32)
    return pl.pallas_call(
        functools.partial(_s5_step_kernel, n_blocks=tr.n_seq // SUBLANES, seg_len=tr.seq_len),
        grid=(N_LANE_TILES,),
        in_specs=[tok, pl.BlockSpec((1, LANES), lambda k: (0, k)), bspec, bspec, cspec, cspec, vec, vec,
                  st_spec, st_spec],
        out_specs=[tok, st_spec, st_spec],
        out_shape=[jax.ShapeDtypeStruct(u2d.shape, BF16), st_shape, st_shape],
        scratch_shapes=[pltpu.VMEM((rows, LANES), F32)] * 3
        + [pltpu.VMEM((S5_CHUNK, STATE_TILE), F32)] * (2 * (rows // S5_CHUNK)),
        compiler_params=_params("parallel"),
        name="s5_step",
    )(u2d, d_skip, bbre, bbim, ccre, ccim, are, aim, *h0)


S5_STEPS = 256


def _s5_long_kernel(u_ref, d_ref, bbre_ref, bbim_ref, ccre_ref, ccim_ref, are_ref, aim_ref,
                    g_ref, sre_ref, sim_ref, up, ys, cre, cim, *chunks, n_seq, n_tiles):
    n_chunks = len(chunks) // 2
    hre, him = chunks[:n_chunks], chunks[n_chunks:]
    rows = S5_STEPS * SUBLANES
    c = pl.program_id(1)

    @pl.when(c == 0)
    def _():
        cre[...] = jnp.zeros(cre.shape, F32)
        cim[...] = jnp.zeros(cim.shape, F32)

    def lane_rows(b, kk):
        return pl.ds(b * n_tiles + kk, S5_STEPS, stride=SUBLANES)

    def tile_lanes(kk):
        return slice(kk * LANES, (kk + 1) * LANES)

    for kk in range(n_tiles):
        up[kk] = jnp.zeros((rows, LANES), F32)
        for b in range(n_seq):
            up[kk, lane_rows(b, kk), :] = u_ref[b, :, tile_lanes(kk)]

    tile_of_row = jnp.bitwise_and(lax.broadcasted_iota(jnp.int32, (SUBLANES, STATE_TILE), 0), n_tiles - 1)
    ar = jnp.broadcast_to(are_ref[0], (SUBLANES, STATE_TILE))
    ai = jnp.broadcast_to(aim_ref[0], (SUBLANES, STATE_TILE))
    for kk in range(1, n_tiles):
        ar = jnp.where(tile_of_row == kk, are_ref[kk], ar)
        ai = jnp.where(tile_of_row == kk, aim_ref[kk], ai)
    state_rows = lambda kk: slice(kk * STATE_TILE, (kk + 1) * STATE_TILE)
    c_re = jnp.concatenate([ccre_ref[state_rows(kk), :] for kk in range(n_tiles)], axis=-1)
    c_im = jnp.concatenate([ccim_ref[state_rows(kk), :] for kk in range(n_tiles)], axis=-1)
    groups_per_chunk = S5_CHUNK // SUBLANES

    def chunk_rows(q):
        return slice(q * S5_CHUNK, (q + 1) * S5_CHUNK)

    def input_chunk(q):
        ub = jnp.concatenate([up[kk, chunk_rows(q), :] for kk in range(n_tiles)], axis=-1).astype(BF16)
        hre[q][...] = _dot(ub, bbre_ref[...])
        him[q][...] = _dot(ub, bbim_ref[...])

    def scan_chunk(q, carry):
        for j in range(groups_per_chunk):
            r = slice(j * SUBLANES, (j + 1) * SUBLANES)
            hr, hi = carry
            carry = (ar * hr - ai * hi + hre[q][r, :], ar * hi + ai * hr + him[q][r, :])
            hre[q][r, :], him[q][r, :] = carry
        return carry

    def output_chunk(q):
        y = _dot(hre[q][...].astype(BF16), c_re) + _dot(him[q][...].astype(BF16), c_im)
        for kk in range(n_tiles):
            ys[kk, chunk_rows(q), :] = y[:, tile_lanes(kk)]

    input_chunk(0)
    carry = (cre[...], cim[...])
    for q in range(n_chunks):
        if q + 1 < n_chunks:
            input_chunk(q + 1)
        carry = scan_chunk(q, carry)
        if q >= 1:
            output_chunk(q - 1)
    output_chunk(n_chunks - 1)
    cre[...], cim[...] = carry

    d = d_ref[...]
    for kk in range(n_tiles):
        for b in range(n_seq):
            u = u_ref[b, :, tile_lanes(kk)]
            y = ys[kk, lane_rows(b, kk), :]
            g_ref[b, :, tile_lanes(kk)] = jax.nn.gelu(y + d[:, tile_lanes(kk)] * u).astype(BF16)

    @pl.when(c == pl.num_programs(1) - 1)
    def _():
        for kk in range(n_tiles):
            for b in range(n_seq):
                lane = b * n_tiles + kk
                sre_ref[b, :, state_rows(kk)] = cre[lane:lane + 1, :]
                sim_ref[b, :, state_rows(kk)] = cim[lane:lane + 1, :]


def _s5_long(tr, u2d, d_skip, tabs):
    are, aim, bbre, bbim, ccre, ccim = tabs
    n_seq, seq_len = tr.n_seq, tr.seq_len
    assert SUBLANES % n_seq == 0 and seq_len % S5_STEPS == 0
    n_tiles = SUBLANES // n_seq
    assert N_LANE_TILES % n_tiles == 0 and n_tiles == 1 << (n_tiles.bit_length() - 1)
    cw = n_tiles * LANES
    rows = S5_STEPS * SUBLANES
    n_chunks = rows // S5_CHUNK
    n_state = SSM_GROUPS * SSM_STATE
    tok = pl.BlockSpec((n_seq, S5_STEPS, cw), lambda k, c: (0, c, k))
    vec = pl.BlockSpec((n_tiles, 1, STATE_TILE), lambda k, c: (k, 0, 0))
    bspec = pl.BlockSpec((cw, STATE_TILE), lambda k, c: (k, 0))
    cspec = pl.BlockSpec((n_tiles * STATE_TILE, LANES), lambda k, c: (k, 0))
    st_spec = pl.BlockSpec((n_seq, 1, n_tiles * STATE_TILE), lambda k, c: (0, 0, k))
    st_shape = jax.ShapeDtypeStruct((n_seq, 1, n_state), F32)
    g3, s_re, s_im = pl.pallas_call(
        functools.partial(_s5_long_kernel, n_seq=n_seq, n_tiles=n_tiles),
        grid=(N_LANE_TILES // n_tiles, seq_len // S5_STEPS),
        in_specs=[tok, pl.BlockSpec((1, cw), lambda k, c: (0, k)), bspec, bspec, cspec, cspec, vec, vec],
        out_specs=[tok, st_spec, st_spec],
        out_shape=[jax.ShapeDtypeStruct((n_seq, seq_len, D_MODEL), BF16), st_shape, st_shape],
        scratch_shapes=[pltpu.VMEM((n_tiles, rows, LANES), F32)] * 2
        + [pltpu.VMEM((SUBLANES, STATE_TILE), F32)] * 2
        + [pltpu.VMEM((S5_CHUNK, STATE_TILE), F32)] * (2 * n_chunks),
        compiler_params=_params("parallel", "arbitrary"),
        name="s5_long",
    )(u2d.reshape(n_seq, seq_len, D_MODEL), d_skip, bbre, bbim, ccre, ccim, are, aim)
    return g3.reshape(n_seq * seq_len, D_MODEL), s_re, s_im


GLU_COLS = 256


def _glu_kernel(*refs, residual):
    if residual:
        a_ref, wa_ref, wb_ref, ba_ref, bb_ref, x_ref, gate_ref, o_ref = refs
    else:
        a_ref, wa_ref, wb_ref, ba_ref, bb_ref, o_ref = refs
    a = a_ref[...]
    tn = wa_ref.shape[1]
    for c in range(0, tn, GLU_COLS):
        cs = slice(c, c + GLU_COLS)
        za = _dot(a, wa_ref[:, cs].astype(BF16)) + ba_ref[:, cs]
        zb = _dot(a, wb_ref[:, cs].astype(BF16)) + bb_ref[:, cs]
        out = za * jax.nn.sigmoid(zb)
        if residual:
            o_ref[:, :, cs] = x_ref[:, :, cs] + gate_ref[:, :, cs] * out.reshape(o_ref.shape[:2] + (GLU_COLS,))
        else:
            o_ref[:, cs] = out


def _glu(tr, a2d, w, b, x3=None, mod=None, layer=None):
    tm, tn = tr.tm, 512
    nb, nj = tm // SUBLANES, D_MODEL // tn
    b3 = b.reshape(1, 1, 2 * D_MODEL)
    residual = x3 is not None
    in_specs = [pl.BlockSpec((tm, D_MODEL), lambda j, i: (i, 0)),
                pl.BlockSpec((None, D_MODEL, tn), lambda j, i: (0, 0, j)),
                pl.BlockSpec((None, D_MODEL, tn), lambda j, i: (0, 0, j + nj)),
                pl.BlockSpec((None, 1, tn), lambda j, i: (0, 0, j)),
                pl.BlockSpec((None, 1, tn), lambda j, i: (0, 0, j + nj))]
    args = (a2d, w, w, b3, b3)
    if residual:
        out_spec = pl.BlockSpec((nb, SUBLANES, tn), lambda j, i: (i, 0, j))
        out_shape = jax.ShapeDtypeStruct(x3.shape, F32)
        in_specs += [out_spec, _mod_spec(tr, layer, 2, tn, lambda j, i: i, lambda j, i: j)]
        args += (x3, mod)
    else:
        out_spec = pl.BlockSpec((tm, tn), lambda j, i: (i, j))
        out_shape = jax.ShapeDtypeStruct(a2d.shape, F32)
    return pl.pallas_call(
        functools.partial(_glu_kernel, residual=residual),
        grid=(nj, a2d.shape[0] // tm),
        in_specs=in_specs,
        out_specs=out_spec,
        out_shape=out_shape,
        compiler_params=_params("parallel", "parallel"),
        name="glu",
    )(*args)


CONV_CHUNK = SUBLANES * SUBLANES
CONV_ROWS = 256


def _conv_weights(w_ref, b_ref, ls):
    w = [jnp.broadcast_to(w_ref[k:k + 1, ls], (SUBLANES, LANES)) for k in range(CONV_WIDTH)]
    return w, jnp.broadcast_to(b_ref[:, ls], (SUBLANES, LANES))


def _conv_taps(win, w, bias):
    acc = [bias] * SUBLANES
    for o in range(CONV_WIDTH + SUBLANES - 1):
        x = win(o)
        for r in range(SUBLANES):
            if 0 <= o - r < CONV_WIDTH:
                acc[r] = acc[r] + w[o - r] * x
    return acc


def _ln_silu_store(cbuf, lg_ref, lb_ref, o_ref):
    n = cbuf.shape[0]
    inv_d = 1.0 / (n * LANES)
    tot = cbuf[0]
    for l in range(1, n):
        tot = tot + cbuf[l]
    mean = jnp.sum(tot, axis=-1, keepdims=True) * inv_d
    sq = jnp.zeros_like(tot)
    for l in range(n):
        xc = cbuf[l] - mean
        sq = sq + xc * xc
    rstd = lax.rsqrt(jnp.sum(sq, axis=-1, keepdims=True) * inv_d + LN_EPS)
    for l in range(n):
        ls = slice(l * LANES, (l + 1) * LANES)
        y = (cbuf[l] - mean) * rstd * lg_ref[:, ls] + lb_ref[:, ls]
        o_ref[:, ls] = jax.nn.silu(y).astype(BF16)


def _conv_ln_kernel(v_ref, halo_ref, w_ref, b_ref, lg_ref, lb_ref, o_ref, pad, cbuf, *, tiles_per_seq):
    rows = v_ref.shape[0]
    first = (pl.program_id(0) % tiles_per_seq) == 0
    for l in range(N_LANE_TILES):
        ls = slice(l * LANES, (l + 1) * LANES)
        pad[l, 0:HALO, :] = jnp.where(first, 0.0, halo_ref[:, ls])
        pad[l, HALO:, :] = v_ref[:, ls]
        w, bias = _conv_weights(w_ref, b_ref, ls)
        for base in range(0, rows, CONV_CHUNK):
            acc = _conv_taps(
                lambda o: pad[l, pl.ds(base + HIST_OFF + o, SUBLANES, stride=SUBLANES), :], w, bias)
            for r in range(SUBLANES):
                cbuf[l, pl.ds(base + r, SUBLANES, stride=SUBLANES), :] = acc[r]
    _ln_silu_store(cbuf, lg_ref, lb_ref, o_ref)


def _conv_ln_long(tr, v2d, w, b, ln_g, ln_b):
    rows = CONV_ROWS
    assert tr.seq_len % rows == 0 and rows % CONV_CHUNK == 0 and rows % HALO == 0
    hb = rows // HALO
    row = pl.BlockSpec((1, D_MODEL), lambda i: (0, 0))
    return pl.pallas_call(
        functools.partial(_conv_ln_kernel, tiles_per_seq=tr.seq_len // rows),
        grid=(v2d.shape[0] // rows,),
        in_specs=[pl.BlockSpec((rows, D_MODEL), lambda i: (i, 0)),
                  pl.BlockSpec((HALO, D_MODEL), lambda i: (jnp.maximum(i * hb - 1, 0), 0)),
                  pl.BlockSpec((None, CONV_WIDTH, D_MODEL), lambda i: (0, 0, 0)),
                  row, row, row],
        out_specs=pl.BlockSpec((rows, D_MODEL), lambda i: (i, 0)),
        out_shape=jax.ShapeDtypeStruct(v2d.shape, BF16),
        scratch_shapes=[pltpu.VMEM((N_LANE_TILES, HALO + rows, LANES), F32),
                        pltpu.VMEM((N_LANE_TILES, rows, LANES), F32)],
        compiler_params=_params("parallel"),
        name="conv_ln_long",
    )(v2d, v2d, w, b, ln_g, ln_b)


def _conv_ln_step_kernel(v_ref, cache_ref, w_ref, b_ref, lg_ref, lb_ref, o_ref, nc_ref, vs, cbuf):
    for l in range(N_LANE_TILES):
        ls = slice(l * LANES, (l + 1) * LANES)
        vs[l] = v_ref[:, ls]
        new = [vs[l, pl.ds(t, SUBLANES, stride=SUBLANES), :] for t in range(SUBLANES)]

        def padded(o, ls=ls, new=new):
            return cache_ref[o, :, ls] if o < CONV_HIST else new[o - CONV_HIST]

        acc = _conv_taps(padded, *_conv_weights(w_ref, b_ref, ls))
        for t in range(SUBLANES):
            cbuf[l, pl.ds(t, SUBLANES, stride=SUBLANES), :] = acc[t]
        for q in range(CONV_HIST):
            nc_ref[q, :, ls] = padded(q + SUBLANES)
    _ln_silu_store(cbuf, lg_ref, lb_ref, o_ref)


def _conv_ln_step(tr, v2d, cache_t, w, b, ln_g, ln_b):
    assert tr.seq_len == SUBLANES
    rows = SUBLANES * tr.seq_len
    row = pl.BlockSpec((1, D_MODEL), lambda s: (0, 0))
    cspec = pl.BlockSpec((CONV_HIST, SUBLANES, D_MODEL), lambda s: (0, s, 0))
    return pl.pallas_call(
        _conv_ln_step_kernel,
        grid=(tr.n_seq // SUBLANES,),
        in_specs=[pl.BlockSpec((rows, D_MODEL), lambda s: (s, 0)), cspec,
                  pl.BlockSpec((None, CONV_WIDTH, D_MODEL), lambda s: (0, 0, 0)),
                  row, row, row],
        out_specs=[pl.BlockSpec((rows, D_MODEL), lambda s: (s, 0)), cspec],
        out_shape=[jax.ShapeDtypeStruct(v2d.shape, BF16), jax.ShapeDtypeStruct(cache_t.shape, F32)],
        scratch_shapes=[pltpu.VMEM((N_LANE_TILES, rows, LANES), F32),
                        pltpu.VMEM((N_LANE_TILES, rows, LANES), F32)],
        compiler_params=_params("parallel"),
        name="conv_ln_step",
    )(v2d, cache_t, w, b, ln_g, ln_b)


MLP_TILE = 1024
MLP_TK = 2048


def _mlp_up_kernel(h_ref, w_ref, o_ref):
    a = jnp.maximum(_dot(h_ref[...], w_ref[...].astype(BF16)), 0.0)
    o_ref[...] = (a * a).astype(BF16)


def _mlp_up(tr, h2d, w1, layer):
    tm, tn = tr.tm, MLP_TILE
    return pl.pallas_call(
        _mlp_up_kernel,
        grid=(D_FF // tn, h2d.shape[0] // tm),
        in_specs=[pl.BlockSpec((tm, D_MODEL), lambda j, i: (i, 0)),
                  pl.BlockSpec((None, D_MODEL, tn), lambda j, i: (layer, 0, j))],
        out_specs=pl.BlockSpec((tm, tn), lambda j, i: (i, j)),
        out_shape=jax.ShapeDtypeStruct((h2d.shape[0], D_FF), BF16),
        compiler_params=_params("parallel", "parallel"),
        name="mlp_up",
    )(h2d, w1)


def _mm_res_kernel(*refs, has_bias, single_k):
    if has_bias:
        a_ref, w_ref, b_ref, x_ref, gate_ref, o_ref = refs
    else:
        a_ref, w_ref, x_ref, gate_ref, o_ref = refs
    def finish(out):
        if has_bias:
            out = out + b_ref[...]
        o_ref[...] = x_ref[...] + gate_ref[...] * out

    def product():
        return _dot(a_ref[...], w_ref[...].astype(BF16)).reshape(o_ref.shape)

    if single_k:
        finish(product())
        return
    k = pl.program_id(2)

    @pl.when(k == 0)
    def _():
        o_ref[...] = jnp.zeros(o_ref.shape, F32)

    o_ref[...] += product()

    @pl.when(k == pl.num_programs(2) - 1)
    def _():
        finish(o_ref[...])


def _mm_res(tr, a2d, w, w_idx, x3, mod, layer, part, bias=None):
    tm, tn = tr.tm, MLP_TILE
    kdim = a2d.shape[1]
    tk = min(kdim, MLP_TK)
    nb = tm // SUBLANES
    xspec = pl.BlockSpec((nb, SUBLANES, tn), lambda j, i, k: (i, 0, j))
    has_bias = bias is not None
    bias_specs = [pl.BlockSpec((1, tn), lambda j, i, k: (0, j))] if has_bias else []
    bias_args = (bias,) if has_bias else ()
    return pl.pallas_call(
        functools.partial(_mm_res_kernel, has_bias=has_bias, single_k=kdim == tk),
        grid=(D_MODEL // tn, a2d.shape[0] // tm, kdim // tk),
        in_specs=[pl.BlockSpec((tm, tk), lambda j, i, k: (i, k)),
                  pl.BlockSpec((None, tk, tn), lambda j, i, k: (w_idx, k, j))] + bias_specs
        + [xspec, _mod_spec(tr, layer, part, tn, lambda j, i, k: i, lambda j, i, k: j)],
        out_specs=xspec,
        out_shape=jax.ShapeDtypeStruct(x3.shape, F32),
        compiler_params=_params("parallel", "parallel", "arbitrary"),
        name="mm_res",
    )(a2d, w, *bias_args, x3, mod)


def _final_norm_kernel(x_ref, g_ref, o_ref):
    x = x_ref[...]
    ms = jnp.mean(x * x, axis=-1, keepdims=True)
    o_ref[...] = x * lax.rsqrt(ms + RMS_EPS) * g_ref[...]


def _final_norm(tr, x3, g):
    nb = _norm_rows(x3) // SUBLANES
    spec = pl.BlockSpec((nb, SUBLANES, D_MODEL), lambda i: (i, 0, 0))
    return pl.pallas_call(
        _final_norm_kernel,
        grid=(x3.shape[0] // nb,),
        in_specs=[spec, pl.BlockSpec((1, D_MODEL), lambda i: (0, 0))],
        out_specs=spec,
        out_shape=jax.ShapeDtypeStruct(x3.shape, F32),
        compiler_params=_params("parallel"),
        name="final_norm",
    )(x3, g)


def _mlp(tr, x3, g, w1, w2, mod, layer):
    h2d = _prenorm(tr, x3, g, mod, layer, 1, BF16)
    return _mm_res(tr, _mlp_up(tr, h2d, w1, layer), w2, layer, x3, mod, layer, 5)


def _trunk(tr, x, mod, h0, cache, tabs, p):
    tokens = tr.n_seq * tr.seq_len
    x3 = x.reshape(tokens // SUBLANES, SUBLANES, D_MODEL)

    u2d = _prenorm(tr, x3, p["rms_g_mix"], mod, 0, 0, F32)
    if h0 is None:
        g2d, s_re, s_im = _s5_long(tr, u2d, p["ssm_d"], tabs)
    else:
        g2d, s_re, s_im = _s5_step(tr, u2d, p["ssm_d"], tabs, h0)
    x3 = _glu(tr, g2d, p["ssm_w_glu"], p["ssm_b_glu"], x3, mod, 0)
    x3 = _mlp(tr, x3, p["rms_g_mlp"], p["mlp_w1"], p["mlp_w2"], mod, 0)

    h2d = _prenorm(tr, x3, p["rms_g_mix"], mod, 1, 0, BF16)
    v2d = _glu(tr, h2d, p["conv_w_pw1"], p["conv_b_pw1"])
    conv_args = (p["conv_w_dw"], p["conv_b_dw"], p["conv_ln_g"], p["conv_ln_b"])
    if cache is None:
        hc2d = _conv_ln_long(tr, v2d, *conv_args)
        new_cache = v2d.reshape(1, tr.n_seq, tr.seq_len, D_MODEL)[:, :, tr.seq_len - CONV_HIST:]
    else:
        hc2d, cache_t = _conv_ln_step(tr, v2d, jnp.transpose(cache[0], (1, 0, 2)), *conv_args)
        new_cache = jnp.transpose(cache_t, (1, 0, 2))[None]
    x3 = _mm_res(tr, hc2d, p["conv_w_pw2"], 0, x3, mod, 1, 2, bias=p["conv_b_pw2"])
    x3 = _mlp(tr, x3, p["rms_g_mlp"], p["mlp_w1"], p["mlp_w2"], mod, 1)
    y3 = _final_norm(tr, x3, p["final_g"])

    state_shape = (1, tr.n_seq, SSM_GROUPS, SSM_STATE)
    return (y3.reshape(tr.n_seq, tr.seq_len, D_MODEL), s_re.reshape(state_shape), s_im.reshape(state_shape),
            new_cache)


def kernel(x_prompt, x_sample, state_ssm_re, state_ssm_im, cache_conv, c_prompt, c_sample, rms_g_mix, rms_g_mlp, w_ada, b_ada, ssm_a_re, ssm_a_im, ssm_log_dt, ssm_b_re, ssm_b_im, ssm_c_re, ssm_c_im, ssm_d, ssm_w_glu, ssm_b_glu, conv_w_pw1, conv_b_pw1, conv_w_dw, conv_b_dw, conv_ln_g, conv_ln_b, conv_w_pw2, conv_b_pw2, mlp_w1, mlp_w2, final_g):
    bp, lp, _ = x_prompt.shape
    bs, ls, _ = x_sample.shape
    assert w_ada.shape[0] == 2 and ssm_a_re.shape[0] == 1 and conv_w_dw.shape[0] == 1
    prompt = Trunk(bp, lp, 1024, mod_row=bs)
    sample = Trunk(bs, ls, bs * ls, mod_row=0)

    depth = w_ada.shape[0]
    p = dict(rms_g_mix=rms_g_mix.reshape(depth, 1, D_MODEL), rms_g_mlp=rms_g_mlp.reshape(depth, 1, D_MODEL),
             ssm_d=ssm_d, ssm_w_glu=ssm_w_glu, ssm_b_glu=ssm_b_glu,
             conv_w_pw1=conv_w_pw1, conv_b_pw1=conv_b_pw1, conv_w_dw=conv_w_dw, conv_b_dw=conv_b_dw,
             conv_ln_g=conv_ln_g, conv_ln_b=conv_ln_b, conv_w_pw2=conv_w_pw2, conv_b_pw2=conv_b_pw2,
             mlp_w1=mlp_w1, mlp_w2=mlp_w2, final_g=final_g.reshape(1, D_MODEL))

    n_c = bp + bs
    pad_rows = -n_c % SUBLANES
    c_all = jnp.concatenate([c_sample, c_prompt, jnp.zeros((pad_rows, D_MODEL), F32)], axis=0)
    mod = _ada(c_all, w_ada, b_ada)

    tabs = _s5_prep(ssm_a_re[0], ssm_a_im[0], ssm_log_dt[0], ssm_b_re[0], ssm_b_im[0],
                    ssm_c_re[0], ssm_c_im[0])

    n_state = SSM_GROUPS * SSM_STATE
    h0 = (state_ssm_re.reshape(bs, n_state), state_ssm_im.reshape(bs, n_state))
    y_p, p_re, p_im, p_buf = _trunk(prompt, x_prompt, mod, None, None, tabs, p)
    y_s, s_re, s_im, s_buf = _trunk(sample, x_sample, mod, h0, cache_conv, tabs, p)
    return (y_p, y_s, p_re, p_im, p_buf, s_re, s_im, s_buf)
```

```python
import collections
import functools

import jax
import jax.numpy as jnp
from jax import lax
from jax.experimental import pallas as pl
from jax.experimental.pallas import tpu as pltpu

F32 = jnp.float32
BF16 = jnp.bfloat16

D_MODEL = 2048
D_FF = 4 * D_MODEL
SSM_GROUP = 16
SSM_GROUPS = D_MODEL // SSM_GROUP
SSM_STATE = 64
LOG2_GROUP = SSM_GROUP.bit_length() - 1
LOG2_STATE = SSM_STATE.bit_length() - 1
assert SSM_GROUP == 1 << LOG2_GROUP and SSM_STATE == 1 << LOG2_STATE
CONV_WIDTH = 31
CONV_HIST = CONV_WIDTH - 1
RMS_EPS = 1e-6
LN_EPS = 1e-5

LANES = 128
SUBLANES = 8
VMEM_LIMIT_BYTES = 56 * 1024 * 1024

GROUPS_PER_TILE = LANES // SSM_GROUP
STATE_TILE = GROUPS_PER_TILE * SSM_STATE
N_LANE_TILES = D_MODEL // LANES
HALO = 32
HIST_OFF = HALO - CONV_HIST

Trunk = collections.namedtuple("Trunk", "n_seq seq_len tm mod_row")


def _params(*sem):
    return pltpu.CompilerParams(dimension_semantics=sem, vmem_limit_bytes=VMEM_LIMIT_BYTES)


def _dot(a, b):
    return jnp.dot(a, b, preferred_element_type=F32)


def _norm_mod(x3, g, sc, sh):
    ms = jnp.mean(x3 * x3, axis=-1, keepdims=True)
    return x3 * lax.rsqrt(ms + RMS_EPS) * (g * (1.0 + sc)) + sh


def _mod_spec(tr, layer, part, tn, ti, tj):
    nblk = D_MODEL // tn
    if tr.seq_len >= tr.tm:
        per = tr.seq_len // tr.tm
        return pl.BlockSpec((None, 1, 1, tn),
                            lambda *g: (layer, tr.mod_row + ti(*g) // per, 0, part * nblk + tj(*g)))
    nbm = tr.tm // SUBLANES
    assert tr.seq_len == SUBLANES and tr.mod_row % nbm == 0
    return pl.BlockSpec((None, nbm, 1, tn),
                        lambda *g: (layer, tr.mod_row // nbm + ti(*g), 0, part * nblk + tj(*g)))


def _ada_kernel(c_ref, w_ref, b_ref, o_ref):
    ca = jax.nn.silu(c_ref[...]).astype(BF16)
    mod = _dot(ca, w_ref[...].astype(BF16)) + b_ref[...]
    for r in range(o_ref.shape[0]):
        o_ref[r] = mod[r:r + 1, :]


def _ada(c_all, w_ada, b_ada):
    depth, d, n = w_ada.shape
    rows = c_all.shape[0]
    tn = 1024
    return pl.pallas_call(
        _ada_kernel,
        grid=(depth, n // tn),
        in_specs=[pl.BlockSpec((rows, d), lambda l, j: (0, 0)),
                  pl.BlockSpec((None, d, tn), lambda l, j: (l, 0, j)),
                  pl.BlockSpec((None, 1, tn), lambda l, j: (l, 0, j))],
        out_specs=pl.BlockSpec((None, rows, 1, tn), lambda l, j: (l, 0, 0, j)),
        out_shape=jax.ShapeDtypeStruct((depth, rows, 1, n), F32),
        compiler_params=_params("parallel", "parallel"),
        name="ada",
    )(c_all, w_ada, b_ada.reshape(depth, 1, n))


def _norm_rows(x3):
    tokens = x3.shape[0] * SUBLANES
    return min(512, tokens // SUBLANES)


def _prenorm_kernel(x_ref, g_ref, sc_ref, sh_ref, o_ref):
    h = _norm_mod(x_ref[...], g_ref[...], sc_ref[...], sh_ref[...])
    o_ref[...] = h.reshape(o_ref.shape).astype(o_ref.dtype)


def _prenorm(tr, x3, g, mod, layer, sublayer, dtype):
    tm = _norm_rows(x3)
    trp = tr._replace(tm=tm)
    nb = tm // SUBLANES
    ti, tj = (lambda i: i), (lambda i: 0)
    return pl.pallas_call(
        _prenorm_kernel,
        grid=(x3.shape[0] // nb,),
        in_specs=[pl.BlockSpec((nb, SUBLANES, D_MODEL), lambda i: (i, 0, 0)),
                  pl.BlockSpec((None, 1, D_MODEL), lambda i: (layer, 0, 0)),
                  _mod_spec(trp, layer, 3 * sublayer + 1, D_MODEL, ti, tj),
                  _mod_spec(trp, layer, 3 * sublayer, D_MODEL, ti, tj)],
        out_specs=pl.BlockSpec((tm, D_MODEL), lambda i: (i, 0)),
        out_shape=jax.ShapeDtypeStruct((x3.shape[0] * SUBLANES, D_MODEL), dtype),
        compiler_params=_params("parallel"),
        name="prenorm",
    )(x3, g, mod, mod)


def _s5_prep_kernel(lre_ref, lim_ref, ldt_ref, bre_ref, bim_ref, cre_ref, cim_ref,
                    are_ref, aim_ref, bbre_ref, bbim_ref, ccre_ref, ccim_ref):
    lr, li = lre_ref[...], lim_ref[...]
    dt = jnp.exp(ldt_ref[...])
    mag = jnp.exp(lr * dt)
    are = mag * jnp.cos(li * dt)
    aim = mag * jnp.sin(li * dt)
    er, ei = are - 1.0, aim
    den = lr * lr + li * li
    qre = (er * lr + ei * li) / den
    qim = (ei * lr - er * li) / den
    are_ref[...] = are
    aim_ref[...] = aim

    br, bi = bre_ref[...], bim_ref[...]
    keep = (jnp.right_shift(lax.broadcasted_iota(jnp.int32, br.shape, 0), LOG2_GROUP)
            == jnp.right_shift(lax.broadcasted_iota(jnp.int32, br.shape, 1), LOG2_STATE))
    bbre_ref[...] = jnp.where(keep, qre * br - qim * bi, 0.0).astype(BF16)
    bbim_ref[...] = jnp.where(keep, qre * bi + qim * br, 0.0).astype(BF16)

    cr, ci = cre_ref[...], cim_ref[...]
    keep = (jnp.right_shift(lax.broadcasted_iota(jnp.int32, cr.shape, 0), LOG2_STATE)
            == jnp.right_shift(lax.broadcasted_iota(jnp.int32, cr.shape, 1), LOG2_GROUP))
    ccre_ref[...] = jnp.where(keep, cr, 0.0).astype(BF16)
    ccim_ref[...] = jnp.where(keep, -ci, 0.0).astype(BF16)


def _s5_prep(a_re, a_im, log_dt, b_re, b_im, c_re, c_im):
    nt = N_LANE_TILES
    tile3 = lambda a: a.reshape(nt, 1, STATE_TILE)
    ldt = jnp.broadcast_to(log_dt[:, None], (SSM_GROUPS, SSM_STATE))
    b_rows = lambda b: jnp.tile(b.transpose(0, 2, 1).reshape(D_MODEL, SSM_STATE), (1, GROUPS_PER_TILE))
    c_rows = lambda c: jnp.tile(c.transpose(0, 2, 1).reshape(SSM_GROUPS * SSM_STATE, SSM_GROUP),
                                (1, GROUPS_PER_TILE))
    vec = pl.BlockSpec((None, 1, STATE_TILE), lambda k: (k, 0, 0))
    bspec = pl.BlockSpec((LANES, STATE_TILE), lambda k: (k, 0))
    cspec = pl.BlockSpec((STATE_TILE, LANES), lambda k: (k, 0))
    return pl.pallas_call(
        _s5_prep_kernel,
        grid=(nt,),
        in_specs=[vec, vec, vec, bspec, bspec, cspec, cspec],
        out_specs=[vec, vec, bspec, bspec, cspec, cspec],
        out_shape=[jax.ShapeDtypeStruct((nt, 1, STATE_TILE), F32)] * 2
        + [jax.ShapeDtypeStruct((D_MODEL, STATE_TILE), BF16)] * 2
        + [jax.ShapeDtypeStruct((SSM_GROUPS * SSM_STATE, LANES), BF16)] * 2,
        compiler_params=_params("parallel"),
        name="s5_prep",
    )(tile3(a_re), tile3(a_im), tile3(ldt), b_rows(b_re), b_rows(b_im), c_rows(c_re), c_rows(c_im))


S5_CHUNK = 256


def _s5_step_kernel(u_ref, d_ref, bbre_ref, bbim_ref, ccre_ref, ccim_ref, are_ref, aim_ref, x0_ref, x1_ref,
                    g_ref, sre_ref, sim_ref, up, gp, gn, *chunks, n_blocks, seg_len):
    n_chunks = len(chunks) // 2
    hre, him = chunks[:n_chunks], chunks[n_chunks:]

    def natural(lane):
        return pl.ds(pl.multiple_of(lane * seg_len, SUBLANES), seg_len)

    def regrouped(lane):
        return pl.ds((lane // SUBLANES) * (seg_len * SUBLANES) + lane % SUBLANES, seg_len, stride=SUBLANES)

    def for_each_lane(body):
        lax.fori_loop(0, n_blocks * SUBLANES, body, 0, unroll=8)

    def regroup(lane, c):
        up[regrouped(lane), :] = u_ref[natural(lane), :]
        return c

    for_each_lane(regroup)

    ar = jnp.broadcast_to(are_ref[...], (SUBLANES, STATE_TILE))
    ai = jnp.broadcast_to(aim_ref[...], (SUBLANES, STATE_TILE))
    d = d_ref[...]
    groups_per_chunk = S5_CHUNK // SUBLANES

    def chunk_rows(q):
        return slice(q * S5_CHUNK, (q + 1) * S5_CHUNK)

    def local_rows(i):
        j = i % groups_per_chunk
        return slice(j * SUBLANES, (j + 1) * SUBLANES)

    def input_chunk(q):
        ub = up[chunk_rows(q), :].astype(BF16)
        hre[q][...] = _dot(ub, bbre_ref[...])
        him[q][...] = _dot(ub, bbim_ref[...])

    def scan_chunk(q, carry):
        for i in range(q * groups_per_chunk, (q + 1) * groups_per_chunk):
            nb, r = i // seg_len, local_rows(i)
            blk = slice(nb * SUBLANES, (nb + 1) * SUBLANES)
            if i % seg_len == 0:
                carry = (x0_ref[blk, :], x1_ref[blk, :])
            hr, hi = carry
            carry = (ar * hr - ai * hi + hre[q][r, :], ar * hi + ai * hr + him[q][r, :])
            hre[q][r, :], him[q][r, :] = carry
            if i % seg_len == seg_len - 1:
                sre_ref[blk, :], sim_ref[blk, :] = carry
        return carry

    def output_chunk(q):
        y = _dot(hre[q][...].astype(BF16), ccre_ref[...]) + _dot(him[q][...].astype(BF16), ccim_ref[...])
        gp[chunk_rows(q), :] = jax.nn.gelu(y + d * up[chunk_rows(q), :])

    input_chunk(0)
    carry = None
    for q in range(n_chunks):
        if q + 1 < n_chunks:
            input_chunk(q + 1)
        carry = scan_chunk(q, carry)
        if q >= 1:
            output_chunk(q - 1)
    output_chunk(n_chunks - 1)

    def ungroup(lane, c):
        gn[natural(lane), :] = gp[regrouped(lane), :]
        return c

    for_each_lane(ungroup)
    g_ref[...] = gn[...].astype(BF16)


def _s5_step(tr, u2d, d_skip, tabs, h0):
    are, aim, bbre, bbim, ccre, ccim = tabs
    rows = tr.n_seq * tr.seq_len
    assert rows % S5_CHUNK == 0 and tr.n_seq % SUBLANES == 0 and tr.seq_len % SUBLANES == 0
    vec = pl.BlockSpec((None, 1, STATE_TILE), lambda k: (k, 0, 0))
    bspec = pl.BlockSpec((LANES, STATE_TILE), lambda k: (k, 0))
    cspec = pl.BlockSpec((STATE_TILE, LANES), lambda k: (k, 0))
    tok = pl.BlockSpec((rows, LANES), lambda k: (0, k))
    st_spec = pl.BlockSpec((tr.n_seq, STATE_TILE), lambda k: (0, k))
    st_shape = jax.ShapeDtypeStruct((tr.n_seq, SSM_GROUPS * SSM_STATE), F32)
    return pl.pallas_call(
        functools.partial(_s5_step_kernel, n_blocks=tr.n_seq // SUBLANES, seg_len=tr.seq_len),
        grid=(N_LANE_TILES,),
        in_specs=[tok, pl.BlockSpec((1, LANES), lambda k: (0, k)), bspec, bspec, cspec, cspec, vec, vec,
                  st_spec, st_spec],
        out_specs=[tok, st_spec, st_spec],
        out_shape=[jax.ShapeDtypeStruct(u2d.shape, BF16), st_shape, st_shape],
        scratch_shapes=[pltpu.VMEM((rows, LANES), F32)] * 3
        + [pltpu.VMEM((S5_CHUNK, STATE_TILE), F32)] * (2 * (rows // S5_CHUNK)),
        compiler_params=_params("parallel"),
        name="s5_step",
    )(u2d, d_skip, bbre, bbim, ccre, ccim, are, aim, *h0)


S5_STEPS = 512


def _s5_long_kernel(u_ref, d_ref, bbre_ref, bbim_ref, ccre_ref, ccim_ref, are_ref, aim_ref,
                    g_ref, sre_ref, sim_ref, up, ys, cre, cim, *chunks, n_seq, n_tiles):
    n_chunks = len(chunks) // 2
    hre, him = chunks[:n_chunks], chunks[n_chunks:]
    rows = S5_STEPS * SUBLANES
    c = pl.program_id(1)

    @pl.when(c == 0)
    def _():
        cre[...] = jnp.zeros(cre.shape, F32)
        cim[...] = jnp.zeros(cim.shape, F32)

    def lane_rows(b, kk):
        return pl.ds(b * n_tiles + kk, S5_STEPS, stride=SUBLANES)

    def tile_lanes(kk):
        return slice(kk * LANES, (kk + 1) * LANES)

    for kk in range(n_tiles):
        up[kk] = jnp.zeros((rows, LANES), F32)
        for b in range(n_seq):
            up[kk, lane_rows(b, kk), :] = u_ref[b, :, tile_lanes(kk)]

    tile_of_row = jnp.bitwise_and(lax.broadcasted_iota(jnp.int32, (SUBLANES, STATE_TILE), 0), n_tiles - 1)
    ar = jnp.broadcast_to(are_ref[0], (SUBLANES, STATE_TILE))
    ai = jnp.broadcast_to(aim_ref[0], (SUBLANES, STATE_TILE))
    for kk in range(1, n_tiles):
        ar = jnp.where(tile_of_row == kk, are_ref[kk], ar)
        ai = jnp.where(tile_of_row == kk, aim_ref[kk], ai)
    state_rows = lambda kk: slice(kk * STATE_TILE, (kk + 1) * STATE_TILE)
    c_re = jnp.concatenate([ccre_ref[state_rows(kk), :] for kk in range(n_tiles)], axis=-1)
    c_im = jnp.concatenate([ccim_ref[state_rows(kk), :] for kk in range(n_tiles)], axis=-1)
    groups_per_chunk = S5_CHUNK // SUBLANES

    def chunk_rows(q):
        return slice(q * S5_CHUNK, (q + 1) * S5_CHUNK)

    def input_chunk(q):
        ub = jnp.concatenate([up[kk, chunk_rows(q), :] for kk in range(n_tiles)], axis=-1).astype(BF16)
        hre[q][...] = _dot(ub, bbre_ref[...])
        him[q][...] = _dot(ub, bbim_ref[...])

    def scan_chunk(q, carry):
        for j in range(groups_per_chunk):
            r = slice(j * SUBLANES, (j + 1) * SUBLANES)
            hr, hi = carry
            carry = (ar * hr - ai * hi + hre[q][r, :], ar * hi + ai * hr + him[q][r, :])
            hre[q][r, :], him[q][r, :] = carry
        return carry

    def output_chunk(q):
        y = _dot(hre[q][...].astype(BF16), c_re) + _dot(him[q][...].astype(BF16), c_im)
        for kk in range(n_tiles):
            ys[kk, chunk_rows(q), :] = y[:, tile_lanes(kk)]

    input_chunk(0)
    carry = (cre[...], cim[...])
    for q in range(n_chunks):
        if q + 1 < n_chunks:
            input_chunk(q + 1)
        carry = scan_chunk(q, carry)
        if q >= 1:
            output_chunk(q - 1)
    output_chunk(n_chunks - 1)
    cre[...], cim[...] = carry

    d = d_ref[...]
    for kk in range(n_tiles):
        for b in range(n_seq):
            u = u_ref[b, :, tile_lanes(kk)]
            y = ys[kk, lane_rows(b, kk), :]
            g_ref[b, :, tile_lanes(kk)] = jax.nn.gelu(y + d[:, tile_lanes(kk)] * u).astype(BF16)

    @pl.when(c == pl.num_programs(1) - 1)
    def _():
        for kk in range(n_tiles):
            for b in range(n_seq):
                lane = b * n_tiles + kk
                sre_ref[b, :, state_rows(kk)] = cre[lane:lane + 1, :]
                sim_ref[b, :, state_rows(kk)] = cim[lane:lane + 1, :]


def _s5_long(tr, u2d, d_skip, tabs):
    are, aim, bbre, bbim, ccre, ccim = tabs
    n_seq, seq_len = tr.n_seq, tr.seq_len
    assert SUBLANES % n_seq == 0 and seq_len % S5_STEPS == 0
    n_tiles = SUBLANES // n_seq
    assert N_LANE_TILES % n_tiles == 0 and n_tiles == 1 << (n_tiles.bit_length() - 1)
    cw = n_tiles * LANES
    rows = S5_STEPS * SUBLANES
    n_chunks = rows // S5_CHUNK
    n_state = SSM_GROUPS * SSM_STATE
    tok = pl.BlockSpec((n_seq, S5_STEPS, cw), lambda k, c: (0, c, k))
    vec = pl.BlockSpec((n_tiles, 1, STATE_TILE), lambda k, c: (k, 0, 0))
    bspec = pl.BlockSpec((cw, STATE_TILE), lambda k, c: (k, 0))
    cspec = pl.BlockSpec((n_tiles * STATE_TILE, LANES), lambda k, c: (k, 0))
    st_spec = pl.BlockSpec((n_seq, 1, n_tiles * STATE_TILE), lambda k, c: (0, 0, k))
    st_shape = jax.ShapeDtypeStruct((n_seq, 1, n_state), F32)
    g3, s_re, s_im = pl.pallas_call(
        functools.partial(_s5_long_kernel, n_seq=n_seq, n_tiles=n_tiles),
        grid=(N_LANE_TILES // n_tiles, seq_len // S5_STEPS),
        in_specs=[tok, pl.BlockSpec((1, cw), lambda k, c: (0, k)), bspec, bspec, cspec, cspec, vec, vec],
        out_specs=[tok, st_spec, st_spec],
        out_shape=[jax.ShapeDtypeStruct((n_seq, seq_len, D_MODEL), BF16), st_shape, st_shape],
        scratch_shapes=[pltpu.VMEM((n_tiles, rows, LANES), F32)] * 2
        + [pltpu.VMEM((SUBLANES, STATE_TILE), F32)] * 2
        + [pltpu.VMEM((S5_CHUNK, STATE_TILE), F32)] * (2 * n_chunks),
        compiler_params=_params("parallel", "arbitrary"),
        name="s5_long",
    )(u2d.reshape(n_seq, seq_len, D_MODEL), d_skip, bbre, bbim, ccre, ccim, are, aim)
    return g3.reshape(n_seq * seq_len, D_MODEL), s_re, s_im


GLU_COLS = 256


def _glu_kernel(*refs, residual):
    if residual:
        a_ref, wa_ref, wb_ref, ba_ref, bb_ref, x_ref, gate_ref, o_ref = refs
    else:
        a_ref, wa_ref, wb_ref, ba_ref, bb_ref, o_ref = refs
    a = a_ref[...]
    tn = wa_ref.shape[1]
    for c in range(0, tn, GLU_COLS):
        cs = slice(c, c + GLU_COLS)
        za = _dot(a, wa_ref[:, cs].astype(BF16)) + ba_ref[:, cs]
        zb = _dot(a, wb_ref[:, cs].astype(BF16)) + bb_ref[:, cs]
        out = za * jax.nn.sigmoid(zb)
        if residual:
            o_ref[:, :, cs] = x_ref[:, :, cs] + gate_ref[:, :, cs] * out.reshape(o_ref.shape[:2] + (GLU_COLS,))
        else:
            o_ref[:, cs] = out


def _glu(tr, a2d, w, b, x3=None, mod=None, layer=None):
    tm, tn = tr.tm, 512
    nb, nj = tm // SUBLANES, D_MODEL // tn
    b3 = b.reshape(1, 1, 2 * D_MODEL)
    residual = x3 is not None
    in_specs = [pl.BlockSpec((tm, D_MODEL), lambda j, i: (i, 0)),
                pl.BlockSpec((None, D_MODEL, tn), lambda j, i: (0, 0, j)),
                pl.BlockSpec((None, D_MODEL, tn), lambda j, i: (0, 0, j + nj)),
                pl.BlockSpec((None, 1, tn), lambda j, i: (0, 0, j)),
                pl.BlockSpec((None, 1, tn), lambda j, i: (0, 0, j + nj))]
    args = (a2d, w, w, b3, b3)
    if residual:
        out_spec = pl.BlockSpec((nb, SUBLANES, tn), lambda j, i: (i, 0, j))
        out_shape = jax.ShapeDtypeStruct(x3.shape, F32)
        in_specs += [out_spec, _mod_spec(tr, layer, 2, tn, lambda j, i: i, lambda j, i: j)]
        args += (x3, mod)
    else:
        out_spec = pl.BlockSpec((tm, tn), lambda j, i: (i, j))
        out_shape = jax.ShapeDtypeStruct(a2d.shape, F32)
    return pl.pallas_call(
        functools.partial(_glu_kernel, residual=residual),
        grid=(nj, a2d.shape[0] // tm),
        in_specs=in_specs,
        out_specs=out_spec,
        out_shape=out_shape,
        compiler_params=_params("parallel", "parallel"),
        name="glu",
    )(*args)


CONV_CHUNK = SUBLANES * SUBLANES
CONV_ROWS = 256


def _conv_weights(w_ref, b_ref, ls):
    w = [jnp.broadcast_to(w_ref[k:k + 1, ls], (SUBLANES, LANES)) for k in range(CONV_WIDTH)]
    return w, jnp.broadcast_to(b_ref[:, ls], (SUBLANES, LANES))


def _conv_taps(win, w, bias):
    acc = [bias] * SUBLANES
    for o in range(CONV_WIDTH + SUBLANES - 1):
        x = win(o)
        for r in range(SUBLANES):
            if 0 <= o - r < CONV_WIDTH:
                acc[r] = acc[r] + w[o - r] * x
    return acc


def _ln_silu_store(cbuf, lg_ref, lb_ref, o_ref):
    n = cbuf.shape[0]
    inv_d = 1.0 / (n * LANES)
    tot = cbuf[0]
    for l in range(1, n):
        tot = tot + cbuf[l]
    mean = jnp.sum(tot, axis=-1, keepdims=True) * inv_d
    sq = jnp.zeros_like(tot)
    for l in range(n):
        xc = cbuf[l] - mean
        sq = sq + xc * xc
    rstd = lax.rsqrt(jnp.sum(sq, axis=-1, keepdims=True) * inv_d + LN_EPS)
    for l in range(n):
        ls = slice(l * LANES, (l + 1) * LANES)
        y = (cbuf[l] - mean) * rstd * lg_ref[:, ls] + lb_ref[:, ls]
        o_ref[:, ls] = jax.nn.silu(y).astype(BF16)


def _conv_ln_kernel(v_ref, halo_ref, w_ref, b_ref, lg_ref, lb_ref, o_ref, pad, cbuf, *, tiles_per_seq):
    rows = v_ref.shape[0]
    first = (pl.program_id(0) % tiles_per_seq) == 0
    for l in range(N_LANE_TILES):
        ls = slice(l * LANES, (l + 1) * LANES)
        pad[l, 0:HALO, :] = jnp.where(first, 0.0, halo_ref[:, ls])
        pad[l, HALO:, :] = v_ref[:, ls]
        w, bias = _conv_weights(w_ref, b_ref, ls)
        for base in range(0, rows, CONV_CHUNK):
            acc = _conv_taps(
                lambda o: pad[l, pl.ds(base + HIST_OFF + o, SUBLANES, stride=SUBLANES), :], w, bias)
            for r in range(SUBLANES):
                cbuf[l, pl.ds(base + r, SUBLANES, stride=SUBLANES), :] = acc[r]
    _ln_silu_store(cbuf, lg_ref, lb_ref, o_ref)


def _conv_ln_long(tr, v2d, w, b, ln_g, ln_b):
    rows = CONV_ROWS
    assert tr.seq_len % rows == 0 and rows % CONV_CHUNK == 0 and rows % HALO == 0
    hb = rows // HALO
    row = pl.BlockSpec((1, D_MODEL), lambda i: (0, 0))
    return pl.pallas_call(
        functools.partial(_conv_ln_kernel, tiles_per_seq=tr.seq_len // rows),
        grid=(v2d.shape[0] // rows,),
        in_specs=[pl.BlockSpec((rows, D_MODEL), lambda i: (i, 0)),
                  pl.BlockSpec((HALO, D_MODEL), lambda i: (jnp.maximum(i * hb - 1, 0), 0)),
                  pl.BlockSpec((None, CONV_WIDTH, D_MODEL), lambda i: (0, 0, 0)),
                  row, row, row],
        out_specs=pl.BlockSpec((rows, D_MODEL), lambda i: (i, 0)),
        out_shape=jax.ShapeDtypeStruct(v2d.shape, BF16),
        scratch_shapes=[pltpu.VMEM((N_LANE_TILES, HALO + rows, LANES), F32),
                        pltpu.VMEM((N_LANE_TILES, rows, LANES), F32)],
        compiler_params=_params("parallel"),
        name="conv_ln_long",
    )(v2d, v2d, w, b, ln_g, ln_b)


def _conv_ln_step_kernel(v_ref, cache_ref, w_ref, b_ref, lg_ref, lb_ref, o_ref, nc_ref, vs, cbuf):
    for l in range(N_LANE_TILES):
        ls = slice(l * LANES, (l + 1) * LANES)
        vs[l] = v_ref[:, ls]
        new = [vs[l, pl.ds(t, SUBLANES, stride=SUBLANES), :] for t in range(SUBLANES)]

        def padded(o, ls=ls, new=new):
            return cache_ref[o, :, ls] if o < CONV_HIST else new[o - CONV_HIST]

        acc = _conv_taps(padded, *_conv_weights(w_ref, b_ref, ls))
        for t in range(SUBLANES):
            cbuf[l, pl.ds(t, SUBLANES, stride=SUBLANES), :] = acc[t]
        for q in range(CONV_HIST):
            nc_ref[q, :, ls] = padded(q + SUBLANES)
    _ln_silu_store(cbuf, lg_ref, lb_ref, o_ref)


def _conv_ln_step(tr, v2d, cache_t, w, b, ln_g, ln_b):
    assert tr.seq_len == SUBLANES
    rows = SUBLANES * tr.seq_len
    row = pl.BlockSpec((1, D_MODEL), lambda s: (0, 0))
    cspec = pl.BlockSpec((CONV_HIST, SUBLANES, D_MODEL), lambda s: (0, s, 0))
    return pl.pallas_call(
        _conv_ln_step_kernel,
        grid=(tr.n_seq // SUBLANES,),
        in_specs=[pl.BlockSpec((rows, D_MODEL), lambda s: (s, 0)), cspec,
                  pl.BlockSpec((None, CONV_WIDTH, D_MODEL), lambda s: (0, 0, 0)),
                  row, row, row],
        out_specs=[pl.BlockSpec((rows, D_MODEL), lambda s: (s, 0)), cspec],
        out_shape=[jax.ShapeDtypeStruct(v2d.shape, BF16), jax.ShapeDtypeStruct(cache_t.shape, F32)],
        scratch_shapes=[pltpu.VMEM((N_LANE_TILES, rows, LANES), F32),
                        pltpu.VMEM((N_LANE_TILES, rows, LANES), F32)],
        compiler_params=_params("parallel"),
        name="conv_ln_step",
    )(v2d, cache_t, w, b, ln_g, ln_b)


MLP_TILE = 1024
MLP_TK = 2048


def _mlp_up_kernel(h_ref, w_ref, o_ref):
    a = jnp.maximum(_dot(h_ref[...], w_ref[...].astype(BF16)), 0.0)
    o_ref[...] = (a * a).astype(BF16)


def _mlp_up(tr, h2d, w1, layer):
    tm, tn = tr.tm, MLP_TILE
    return pl.pallas_call(
        _mlp_up_kernel,
        grid=(D_FF // tn, h2d.shape[0] // tm),
        in_specs=[pl.BlockSpec((tm, D_MODEL), lambda j, i: (i, 0)),
                  pl.BlockSpec((None, D_MODEL, tn), lambda j, i: (layer, 0, j))],
        out_specs=pl.BlockSpec((tm, tn), lambda j, i: (i, j)),
        out_shape=jax.ShapeDtypeStruct((h2d.shape[0], D_FF), BF16),
        compiler_params=_params("parallel", "parallel"),
        name="mlp_up",
    )(h2d, w1)


def _mm_res_kernel(*refs, has_bias, single_k):
    if has_bias:
        a_ref, w_ref, b_ref, x_ref, gate_ref, o_ref = refs
    else:
        a_ref, w_ref, x_ref, gate_ref, o_ref = refs
    def finish(out):
        if has_bias:
            out = out + b_ref[...]
        o_ref[...] = x_ref[...] + gate_ref[...] * out

    def product():
        return _dot(a_ref[...], w_ref[...].astype(BF16)).reshape(o_ref.shape)

    if single_k:
        finish(product())
        return
    assert not has_bias

    @pl.when(pl.program_id(2) == 0)
    def _():
        o_ref[...] = x_ref[...]

    o_ref[...] += gate_ref[...] * product()


def _mm_res(tr, a2d, w, w_idx, x3, mod, layer, part, bias=None):
    tm, tn = tr.tm, MLP_TILE
    kdim = a2d.shape[1]
    tk = min(kdim, MLP_TK)
    nb = tm // SUBLANES
    xspec = pl.BlockSpec((nb, SUBLANES, tn), lambda j, i, k: (i, 0, j))
    has_bias = bias is not None
    bias_specs = [pl.BlockSpec((1, tn), lambda j, i, k: (0, j))] if has_bias else []
    bias_args = (bias,) if has_bias else ()
    return pl.pallas_call(
        functools.partial(_mm_res_kernel, has_bias=has_bias, single_k=kdim == tk),
        grid=(D_MODEL // tn, a2d.shape[0] // tm, kdim // tk),
        in_specs=[pl.BlockSpec((tm, tk), lambda j, i, k: (i, k)),
                  pl.BlockSpec((None, tk, tn), lambda j, i, k: (w_idx, k, j))] + bias_specs
        + [xspec, _mod_spec(tr, layer, part, tn, lambda j, i, k: i, lambda j, i, k: j)],
        out_specs=xspec,
        out_shape=jax.ShapeDtypeStruct(x3.shape, F32),
        compiler_params=_params("parallel", "parallel", "arbitrary"),
        name="mm_res",
    )(a2d, w, *bias_args, x3, mod)


def _final_norm_kernel(x_ref, g_ref, o_ref):
    x = x_ref[...]
    ms = jnp.mean(x * x, axis=-1, keepdims=True)
    o_ref[...] = x * lax.rsqrt(ms + RMS_EPS) * g_ref[...]


def _final_norm(tr, x3, g):
    nb = _norm_rows(x3) // SUBLANES
    spec = pl.BlockSpec((nb, SUBLANES, D_MODEL), lambda i: (i, 0, 0))
    return pl.pallas_call(
        _final_norm_kernel,
        grid=(x3.shape[0] // nb,),
        in_specs=[spec, pl.BlockSpec((1, D_MODEL), lambda i: (0, 0))],
        out_specs=spec,
        out_shape=jax.ShapeDtypeStruct(x3.shape, F32),
        compiler_params=_params("parallel"),
        name="final_norm",
    )(x3, g)


def _mlp(tr, x3, g, w1, w2, mod, layer):
    h2d = _prenorm(tr, x3, g, mod, layer, 1, BF16)
    return _mm_res(tr, _mlp_up(tr, h2d, w1, layer), w2, layer, x3, mod, layer, 5)


def _trunk(tr, x, mod, h0, cache, tabs, p):
    tokens = tr.n_seq * tr.seq_len
    x3 = x.reshape(tokens // SUBLANES, SUBLANES, D_MODEL)

    u2d = _prenorm(tr, x3, p["rms_g_mix"], mod, 0, 0, F32)
    if h0 is None:
        g2d, s_re, s_im = _s5_long(tr, u2d, p["ssm_d"], tabs)
    else:
        g2d, s_re, s_im = _s5_step(tr, u2d, p["ssm_d"], tabs, h0)
    x3 = _glu(tr, g2d, p["ssm_w_glu"], p["ssm_b_glu"], x3, mod, 0)
    x3 = _mlp(tr, x3, p["rms_g_mlp"], p["mlp_w1"], p["mlp_w2"], mod, 0)

    h2d = _prenorm(tr, x3, p["rms_g_mix"], mod, 1, 0, BF16)
    v2d = _glu(tr, h2d, p["conv_w_pw1"], p["conv_b_pw1"])
    conv_args = (p["conv_w_dw"], p["conv_b_dw"], p["conv_ln_g"], p["conv_ln_b"])
    if cache is None:
        hc2d = _conv_ln_long(tr, v2d, *conv_args)
        new_cache = v2d.reshape(1, tr.n_seq, tr.seq_len, D_MODEL)[:, :, tr.seq_len - CONV_HIST:]
    else:
        hc2d, cache_t = _conv_ln_step(tr, v2d, jnp.transpose(cache[0], (1, 0, 2)), *conv_args)
        new_cache = jnp.transpose(cache_t, (1, 0, 2))[None]
    x3 = _mm_res(tr, hc2d, p["conv_w_pw2"], 0, x3, mod, 1, 2, bias=p["conv_b_pw2"])
    x3 = _mlp(tr, x3, p["rms_g_mlp"], p["mlp_w1"], p["mlp_w2"], mod, 1)
    y3 = _final_norm(tr, x3, p["final_g"])

    state_shape = (1, tr.n_seq, SSM_GROUPS, SSM_STATE)
    return (y3.reshape(tr.n_seq, tr.seq_len, D_MODEL), s_re.reshape(state_shape), s_im.reshape(state_shape),
            new_cache)


def kernel(x_prompt, x_sample, state_ssm_re, state_ssm_im, cache_conv, c_prompt, c_sample, rms_g_mix, rms_g_mlp, w_ada, b_ada, ssm_a_re, ssm_a_im, ssm_log_dt, ssm_b_re, ssm_b_im, ssm_c_re, ssm_c_im, ssm_d, ssm_w_glu, ssm_b_glu, conv_w_pw1, conv_b_pw1, conv_w_dw, conv_b_dw, conv_ln_g, conv_ln_b, conv_w_pw2, conv_b_pw2, mlp_w1, mlp_w2, final_g):
    bp, lp, _ = x_prompt.shape
    bs, ls, _ = x_sample.shape
    assert w_ada.shape[0] == 2 and ssm_a_re.shape[0] == 1 and conv_w_dw.shape[0] == 1
    prompt = Trunk(bp, lp, 1024, mod_row=bs)
    sample = Trunk(bs, ls, bs * ls, mod_row=0)

    depth = w_ada.shape[0]
    p = dict(rms_g_mix=rms_g_mix.reshape(depth, 1, D_MODEL), rms_g_mlp=rms_g_mlp.reshape(depth, 1, D_MODEL),
             ssm_d=ssm_d, ssm_w_glu=ssm_w_glu, ssm_b_glu=ssm_b_glu,
             conv_w_pw1=conv_w_pw1, conv_b_pw1=conv_b_pw1, conv_w_dw=conv_w_dw, conv_b_dw=conv_b_dw,
             conv_ln_g=conv_ln_g, conv_ln_b=conv_ln_b, conv_w_pw2=conv_w_pw2, conv_b_pw2=conv_b_pw2,
             mlp_w1=mlp_w1, mlp_w2=mlp_w2, final_g=final_g.reshape(1, D_MODEL))

    n_c = bp + bs
    pad_rows = -n_c % SUBLANES
    c_all = jnp.concatenate([c_sample, c_prompt, jnp.zeros((pad_rows, D_MODEL), F32)], axis=0)
    mod = _ada(c_all, w_ada, b_ada)

    tabs = _s5_prep(ssm_a_re[0], ssm_a_im[0], ssm_log_dt[0], ssm_b_re[0], ssm_b_im[0],
                    ssm_c_re[0], ssm_c_im[0])

    n_state = SSM_GROUPS * SSM_STATE
    h0 = (state_ssm_re.reshape(bs, n_state), state_ssm_im.reshape(bs, n_state))
    y_p, p_re, p_im, p_buf = _trunk(prompt, x_prompt, mod, None, None, tabs, p)
    y_s, s_re, s_im, s_buf = _trunk(sample, x_sample, mod, h0, cache_conv, tabs, p)
    return (y_p, y_s, p_re, p_im, p_buf, s_re, s_im, s_buf)
```

```python
import collections
import functools

import jax
import jax.numpy as jnp
from jax import lax
from jax.experimental import pallas as pl
from jax.experimental.pallas import tpu as pltpu

F32 = jnp.float32
BF16 = jnp.bfloat16

D_MODEL = 2048
D_FF = 4 * D_MODEL
SSM_GROUP = 16
SSM_GROUPS = D_MODEL // SSM_GROUP
SSM_STATE = 64
LOG2_GROUP = SSM_GROUP.bit_length() - 1
LOG2_STATE = SSM_STATE.bit_length() - 1
assert SSM_GROUP == 1 << LOG2_GROUP and SSM_STATE == 1 << LOG2_STATE
CONV_WIDTH = 31
CONV_HIST = CONV_WIDTH - 1
RMS_EPS = 1e-6
LN_EPS = 1e-5

LANES = 128
SUBLANES = 8
VMEM_LIMIT_BYTES = 56 * 1024 * 1024

GROUPS_PER_TILE = LANES // SSM_GROUP
STATE_TILE = GROUPS_PER_TILE * SSM_STATE
N_LANE_TILES = D_MODEL // LANES
HALO = 32
HIST_OFF = HALO - CONV_HIST

Trunk = collections.namedtuple("Trunk", "n_seq seq_len tm mod_row")


def _params(*sem):
    return pltpu.CompilerParams(dimension_semantics=sem, vmem_limit_bytes=VMEM_LIMIT_BYTES)


def _dot(a, b):
    return jnp.dot(a, b, preferred_element_type=F32)


def _norm_mod(x3, g, sc, sh):
    ms = jnp.mean(x3 * x3, axis=-1, keepdims=True)
    return x3 * lax.rsqrt(ms + RMS_EPS) * (g * (1.0 + sc)) + sh


def _mod_spec(tr, layer, part, tn, ti, tj):
    nblk = D_MODEL // tn
    if tr.seq_len >= tr.tm:
        per = tr.seq_len // tr.tm
        return pl.BlockSpec((None, 1, 1, tn),
                            lambda *g: (layer, tr.mod_row + ti(*g) // per, 0, part * nblk + tj(*g)))
    nbm = tr.tm // SUBLANES
    assert tr.seq_len == SUBLANES and tr.mod_row % nbm == 0
    return pl.BlockSpec((None, nbm, 1, tn),
                        lambda *g: (layer, tr.mod_row // nbm + ti(*g), 0, part * nblk + tj(*g)))


def _ada_kernel(c_ref, w_ref, b_ref, o_ref):
    ca = jax.nn.silu(c_ref[...]).astype(BF16)
    mod = _dot(ca, w_ref[...].astype(BF16)) + b_ref[...]
    for r in range(o_ref.shape[0]):
        o_ref[r] = mod[r:r + 1, :]


def _ada(c_all, w_ada, b_ada):
    depth, d, n = w_ada.shape
    rows = c_all.shape[0]
    tn = 1024
    return pl.pallas_call(
        _ada_kernel,
        grid=(depth, n // tn),
        in_specs=[pl.BlockSpec((rows, d), lambda l, j: (0, 0)),
                  pl.BlockSpec((None, d, tn), lambda l, j: (l, 0, j)),
                  pl.BlockSpec((None, 1, tn), lambda l, j: (l, 0, j))],
        out_specs=pl.BlockSpec((None, rows, 1, tn), lambda l, j: (l, 0, 0, j)),
        out_shape=jax.ShapeDtypeStruct((depth, rows, 1, n), F32),
        compiler_params=_params("parallel", "parallel"),
        name="ada",
    )(c_all, w_ada, b_ada.reshape(depth, 1, n))


def _norm_rows(x3):
    tokens = x3.shape[0] * SUBLANES
    return min(1024, tokens // SUBLANES)


def _prenorm_kernel(x_ref, g_ref, sc_ref, sh_ref, o_ref):
    h = _norm_mod(x_ref[...], g_ref[...], sc_ref[...], sh_ref[...])
    o_ref[...] = h.reshape(o_ref.shape).astype(o_ref.dtype)


def _prenorm(tr, x3, g, mod, layer, sublayer, dtype):
    tm = _norm_rows(x3)
    trp = tr._replace(tm=tm)
    nb = tm // SUBLANES
    ti, tj = (lambda i: i), (lambda i: 0)
    return pl.pallas_call(
        _prenorm_kernel,
        grid=(x3.shape[0] // nb,),
        in_specs=[pl.BlockSpec((nb, SUBLANES, D_MODEL), lambda i: (i, 0, 0)),
                  pl.BlockSpec((None, 1, D_MODEL), lambda i: (layer, 0, 0)),
                  _mod_spec(trp, layer, 3 * sublayer + 1, D_MODEL, ti, tj),
                  _mod_spec(trp, layer, 3 * sublayer, D_MODEL, ti, tj)],
        out_specs=pl.BlockSpec((tm, D_MODEL), lambda i: (i, 0)),
        out_shape=jax.ShapeDtypeStruct((x3.shape[0] * SUBLANES, D_MODEL), dtype),
        compiler_params=_params("parallel"),
        name="prenorm",
    )(x3, g, mod, mod)


def _s5_prep_kernel(lre_ref, lim_ref, ldt_ref, bre_ref, bim_ref, cre_ref, cim_ref,
                    are_ref, aim_ref, bbre_ref, bbim_ref, ccre_ref, ccim_ref):
    lr, li = lre_ref[...], lim_ref[...]
    dt = jnp.exp(ldt_ref[...])
    mag = jnp.exp(lr * dt)
    are = mag * jnp.cos(li * dt)
    aim = mag * jnp.sin(li * dt)
    er, ei = are - 1.0, aim
    den = lr * lr + li * li
    qre = (er * lr + ei * li) / den
    qim = (ei * lr - er * li) / den
    are_ref[...] = are
    aim_ref[...] = aim

    br, bi = bre_ref[...], bim_ref[...]
    keep = (jnp.right_shift(lax.broadcasted_iota(jnp.int32, br.shape, 0), LOG2_GROUP)
            == jnp.right_shift(lax.broadcasted_iota(jnp.int32, br.shape, 1), LOG2_STATE))
    bbre_ref[...] = jnp.where(keep, qre * br - qim * bi, 0.0).astype(BF16)
    bbim_ref[...] = jnp.where(keep, qre * bi + qim * br, 0.0).astype(BF16)

    cr, ci = cre_ref[...], cim_ref[...]
    keep = (jnp.right_shift(lax.broadcasted_iota(jnp.int32, cr.shape, 0), LOG2_STATE)
            == jnp.right_shift(lax.broadcasted_iota(jnp.int32, cr.shape, 1), LOG2_GROUP))
    ccre_ref[...] = jnp.where(keep, cr, 0.0).astype(BF16)
    ccim_ref[...] = jnp.where(keep, -ci, 0.0).astype(BF16)


def _s5_prep(a_re, a_im, log_dt, b_re, b_im, c_re, c_im):
    nt = N_LANE_TILES
    tile3 = lambda a: a.reshape(nt, 1, STATE_TILE)
    ldt = jnp.broadcast_to(log_dt[:, None], (SSM_GROUPS, SSM_STATE))
    b_rows = lambda b: jnp.tile(b.transpose(0, 2, 1).reshape(D_MODEL, SSM_STATE), (1, GROUPS_PER_TILE))
    c_rows = lambda c: jnp.tile(c.transpose(0, 2, 1).reshape(SSM_GROUPS * SSM_STATE, SSM_GROUP),
                                (1, GROUPS_PER_TILE))
    vec = pl.BlockSpec((None, 1, STATE_TILE), lambda k: (k, 0, 0))
    bspec = pl.BlockSpec((LANES, STATE_TILE), lambda k: (k, 0))
    cspec = pl.BlockSpec((STATE_TILE, LANES), lambda k: (k, 0))
    return pl.pallas_call(
        _s5_prep_kernel,
        grid=(nt,),
        in_specs=[vec, vec, vec, bspec, bspec, cspec, cspec],
        out_specs=[vec, vec, bspec, bspec, cspec, cspec],
        out_shape=[jax.ShapeDtypeStruct((nt, 1, STATE_TILE), F32)] * 2
        + [jax.ShapeDtypeStruct((D_MODEL, STATE_TILE), BF16)] * 2
        + [jax.ShapeDtypeStruct((SSM_GROUPS * SSM_STATE, LANES), BF16)] * 2,
        compiler_params=_params("parallel"),
        name="s5_prep",
    )(tile3(a_re), tile3(a_im), tile3(ldt), b_rows(b_re), b_rows(b_im), c_rows(c_re), c_rows(c_im))


S5_CHUNK = 256


def _s5_step_kernel(u_ref, d_ref, bbre_ref, bbim_ref, ccre_ref, ccim_ref, are_ref, aim_ref, x0_ref, x1_ref,
                    g_ref, sre_ref, sim_ref, up, gp, gn, *chunks, n_blocks, seg_len):
    n_chunks = len(chunks) // 2
    hre, him = chunks[:n_chunks], chunks[n_chunks:]

    def natural(lane):
        return pl.ds(pl.multiple_of(lane * seg_len, SUBLANES), seg_len)

    def regrouped(lane):
        return pl.ds((lane // SUBLANES) * (seg_len * SUBLANES) + lane % SUBLANES, seg_len, stride=SUBLANES)

    def for_each_lane(body):
        lax.fori_loop(0, n_blocks * SUBLANES, body, 0, unroll=8)

    def regroup(lane, c):
        up[regrouped(lane), :] = u_ref[natural(lane), :]
        return c

    for_each_lane(regroup)

    ar = jnp.broadcast_to(are_ref[...], (SUBLANES, STATE_TILE))
    ai = jnp.broadcast_to(aim_ref[...], (SUBLANES, STATE_TILE))
    d = d_ref[...]
    groups_per_chunk = S5_CHUNK // SUBLANES

    def chunk_rows(q):
        return slice(q * S5_CHUNK, (q + 1) * S5_CHUNK)

    def local_rows(i):
        j = i % groups_per_chunk
        return slice(j * SUBLANES, (j + 1) * SUBLANES)

    def input_chunk(q):
        ub = up[chunk_rows(q), :].astype(BF16)
        hre[q][...] = _dot(ub, bbre_ref[...])
        him[q][...] = _dot(ub, bbim_ref[...])

    def scan_chunk(q, carry):
        for i in range(q * groups_per_chunk, (q + 1) * groups_per_chunk):
            nb, r = i // seg_len, local_rows(i)
            blk = slice(nb * SUBLANES, (nb + 1) * SUBLANES)
            if i % seg_len == 0:
                carry = (x0_ref[blk, :], x1_ref[blk, :])
            hr, hi = carry
            carry = (ar * hr - ai * hi + hre[q][r, :], ar * hi + ai * hr + him[q][r, :])
            hre[q][r, :], him[q][r, :] = carry
            if i % seg_len == seg_len - 1:
                sre_ref[blk, :], sim_ref[blk, :] = carry
        return carry

    def output_chunk(q):
        y = _dot(hre[q][...].astype(BF16), ccre_ref[...]) + _dot(him[q][...].astype(BF16), ccim_ref[...])
        gp[chunk_rows(q), :] = jax.nn.gelu(y + d * up[chunk_rows(q), :])

    input_chunk(0)
    carry = None
    for q in range(n_chunks):
        if q + 1 < n_chunks:
            input_chunk(q + 1)
        carry = scan_chunk(q, carry)
        if q >= 1:
            output_chunk(q - 1)
    output_chunk(n_chunks - 1)

    def ungroup(lane, c):
        gn[natural(lane), :] = gp[regrouped(lane), :]
        return c

    for_each_lane(ungroup)
    g_ref[...] = gn[...].astype(BF16)


def _s5_step(tr, u2d, d_skip, tabs, h0):
    are, aim, bbre, bbim, ccre, ccim = tabs
    rows = tr.n_seq * tr.seq_len
    assert rows % S5_CHUNK == 0 and tr.n_seq % SUBLANES == 0 and tr.seq_len % SUBLANES == 0
    vec = pl.BlockSpec((None, 1, STATE_TILE), lambda k: (k, 0, 0))
    bspec = pl.BlockSpec((LANES, STATE_TILE), lambda k: (k, 0))
    cspec = pl.BlockSpec((STATE_TILE, LANES), lambda k: (k, 0))
    tok = pl.BlockSpec((rows, LANES), lambda k: (0, k))
    st_spec = pl.BlockSpec((tr.n_seq, STATE_TILE), lambda k: (0, k))
    st_shape = jax.ShapeDtypeStruct((tr.n_seq, SSM_GROUPS * SSM_STATE), F32)
    return pl.pallas_call(
        functools.partial(_s5_step_kernel, n_blocks=tr.n_seq // SUBLANES, seg_len=tr.seq_len),
        grid=(N_LANE_TILES,),
        in_specs=[tok, pl.BlockSpec((1, LANES), lambda k: (0, k)), bspec, bspec, cspec, cspec, vec, vec,
                  st_spec, st_spec],
        out_specs=[tok, st_spec, st_spec],
        out_shape=[jax.ShapeDtypeStruct(u2d.shape, BF16), st_shape, st_shape],
        scratch_shapes=[pltpu.VMEM((rows, LANES), F32)] * 3
        + [pltpu.VMEM((S5_CHUNK, STATE_TILE), F32)] * (2 * (rows // S5_CHUNK)),
        compiler_params=_params("parallel"),
        name="s5_step",
    )(u2d, d_skip, bbre, bbim, ccre, ccim, are, aim, *h0)


S5_STEPS = 512


def _s5_long_kernel(u_ref, d_ref, bbre_ref, bbim_ref, ccre_ref, ccim_ref, are_ref, aim_ref,
                    g_ref, sre_ref, sim_ref, up, ys, cre, cim, *chunks, n_seq, n_tiles):
    n_chunks = len(chunks) // 2
    hre, him = chunks[:n_chunks], chunks[n_chunks:]
    rows = S5_STEPS * SUBLANES
    c = pl.program_id(1)

    @pl.when(c == 0)
    def _():
        cre[...] = jnp.zeros(cre.shape, F32)
        cim[...] = jnp.zeros(cim.shape, F32)

    def lane_rows(b, kk):
        return pl.ds(b * n_tiles + kk, S5_STEPS, stride=SUBLANES)

    def tile_lanes(kk):
        return slice(kk * LANES, (kk + 1) * LANES)

    for kk in range(n_tiles):
        up[kk] = jnp.zeros((rows, LANES), F32)
        for b in range(n_seq):
            up[kk, lane_rows(b, kk), :] = u_ref[b, :, tile_lanes(kk)]

    tile_of_row = jnp.bitwise_and(lax.broadcasted_iota(jnp.int32, (SUBLANES, STATE_TILE), 0), n_tiles - 1)
    ar = jnp.broadcast_to(are_ref[0], (SUBLANES, STATE_TILE))
    ai = jnp.broadcast_to(aim_ref[0], (SUBLANES, STATE_TILE))
    for kk in range(1, n_tiles):
        ar = jnp.where(tile_of_row == kk, are_ref[kk], ar)
        ai = jnp.where(tile_of_row == kk, aim_ref[kk], ai)
    state_rows = lambda kk: slice(kk * STATE_TILE, (kk + 1) * STATE_TILE)
    c_re = jnp.concatenate([ccre_ref[state_rows(kk), :] for kk in range(n_tiles)], axis=-1)
    c_im = jnp.concatenate([ccim_ref[state_rows(kk), :] for kk in range(n_tiles)], axis=-1)
    groups_per_chunk = S5_CHUNK // SUBLANES

    def chunk_rows(q):
        return slice(q * S5_CHUNK, (q + 1) * S5_CHUNK)

    def input_chunk(q):
        ub = jnp.concatenate([up[kk, chunk_rows(q), :] for kk in range(n_tiles)], axis=-1).astype(BF16)
        hre[q][...] = _dot(ub, bbre_ref[...])
        him[q][...] = _dot(ub, bbim_ref[...])

    def scan_chunk(q, carry):
        for j in range(groups_per_chunk):
            r = slice(j * SUBLANES, (j + 1) * SUBLANES)
            hr, hi = carry
            carry = (ar * hr - ai * hi + hre[q][r, :], ar * hi + ai * hr + him[q][r, :])
            hre[q][r, :], him[q][r, :] = carry
        return carry

    def output_chunk(q):
        y = _dot(hre[q][...].astype(BF16), c_re) + _dot(him[q][...].astype(BF16), c_im)
        for kk in range(n_tiles):
            ys[kk, chunk_rows(q), :] = y[:, tile_lanes(kk)]

    input_chunk(0)
    carry = (cre[...], cim[...])
    for q in range(n_chunks):
        if q + 1 < n_chunks:
            input_chunk(q + 1)
        carry = scan_chunk(q, carry)
        if q >= 1:
            output_chunk(q - 1)
    output_chunk(n_chunks - 1)
    cre[...], cim[...] = carry

    d = d_ref[...]
    for kk in range(n_tiles):
        for b in range(n_seq):
            u = u_ref[b, :, tile_lanes(kk)]
            y = ys[kk, lane_rows(b, kk), :]
            g_ref[b, :, tile_lanes(kk)] = jax.nn.gelu(y + d[:, tile_lanes(kk)] * u).astype(BF16)

    @pl.when(c == pl.num_programs(1) - 1)
    def _():
        for kk in range(n_tiles):
            for b in range(n_seq):
                lane = b * n_tiles + kk
                sre_ref[b, :, state_rows(kk)] = cre[lane:lane + 1, :]
                sim_ref[b, :, state_rows(kk)] = cim[lane:lane + 1, :]


def _s5_long(tr, u2d, d_skip, tabs):
    are, aim, bbre, bbim, ccre, ccim = tabs
    n_seq, seq_len = tr.n_seq, tr.seq_len
    assert SUBLANES % n_seq == 0 and seq_len % S5_STEPS == 0
    n_tiles = SUBLANES // n_seq
    assert N_LANE_TILES % n_tiles == 0 and n_tiles == 1 << (n_tiles.bit_length() - 1)
    cw = n_tiles * LANES
    rows = S5_STEPS * SUBLANES
    n_chunks = rows // S5_CHUNK
    n_state = SSM_GROUPS * SSM_STATE
    tok = pl.BlockSpec((n_seq, S5_STEPS, cw), lambda k, c: (0, c, k))
    vec = pl.BlockSpec((n_tiles, 1, STATE_TILE), lambda k, c: (k, 0, 0))
    bspec = pl.BlockSpec((cw, STATE_TILE), lambda k, c: (k, 0))
    cspec = pl.BlockSpec((n_tiles * STATE_TILE, LANES), lambda k, c: (k, 0))
    st_spec = pl.BlockSpec((n_seq, 1, n_tiles * STATE_TILE), lambda k, c: (0, 0, k))
    st_shape = jax.ShapeDtypeStruct((n_seq, 1, n_state), F32)
    g3, s_re, s_im = pl.pallas_call(
        functools.partial(_s5_long_kernel, n_seq=n_seq, n_tiles=n_tiles),
        grid=(N_LANE_TILES // n_tiles, seq_len // S5_STEPS),
        in_specs=[tok, pl.BlockSpec((1, cw), lambda k, c: (0, k)), bspec, bspec, cspec, cspec, vec, vec],
        out_specs=[tok, st_spec, st_spec],
        out_shape=[jax.ShapeDtypeStruct((n_seq, seq_len, D_MODEL), BF16), st_shape, st_shape],
        scratch_shapes=[pltpu.VMEM((n_tiles, rows, LANES), F32)] * 2
        + [pltpu.VMEM((SUBLANES, STATE_TILE), F32)] * 2
        + [pltpu.VMEM((S5_CHUNK, STATE_TILE), F32)] * (2 * n_chunks),
        compiler_params=_params("parallel", "arbitrary"),
        name="s5_long",
    )(u2d.reshape(n_seq, seq_len, D_MODEL), d_skip, bbre, bbim, ccre, ccim, are, aim)
    return g3.reshape(n_seq * seq_len, D_MODEL), s_re, s_im


GLU_COLS = 256


def _glu_kernel(*refs, residual):
    if residual:
        a_ref, wa_ref, wb_ref, ba_ref, bb_ref, x_ref, gate_ref, o_ref = refs
    else:
        a_ref, wa_ref, wb_ref, ba_ref, bb_ref, o_ref = refs
    a = a_ref[...]
    tn = wa_ref.shape[1]
    for c in range(0, tn, GLU_COLS):
        cs = slice(c, c + GLU_COLS)
        za = _dot(a, wa_ref[:, cs].astype(BF16)) + ba_ref[:, cs]
        zb = _dot(a, wb_ref[:, cs].astype(BF16)) + bb_ref[:, cs]
        out = za * jax.nn.sigmoid(zb)
        if residual:
            o_ref[:, :, cs] = x_ref[:, :, cs] + gate_ref[:, :, cs] * out.reshape(o_ref.shape[:2] + (GLU_COLS,))
        else:
            o_ref[:, cs] = out


def _glu(tr, a2d, w, b, x3=None, mod=None, layer=None):
    tm, tn = tr.tm, 512
    nb, nj = tm // SUBLANES, D_MODEL // tn
    b3 = b.reshape(1, 1, 2 * D_MODEL)
    residual = x3 is not None
    in_specs = [pl.BlockSpec((tm, D_MODEL), lambda j, i: (i, 0)),
                pl.BlockSpec((None, D_MODEL, tn), lambda j, i: (0, 0, j)),
                pl.BlockSpec((None, D_MODEL, tn), lambda j, i: (0, 0, j + nj)),
                pl.BlockSpec((None, 1, tn), lambda j, i: (0, 0, j)),
                pl.BlockSpec((None, 1, tn), lambda j, i: (0, 0, j + nj))]
    args = (a2d, w, w, b3, b3)
    if residual:
        out_spec = pl.BlockSpec((nb, SUBLANES, tn), lambda j, i: (i, 0, j))
        out_shape = jax.ShapeDtypeStruct(x3.shape, F32)
        in_specs += [out_spec, _mod_spec(tr, layer, 2, tn, lambda j, i: i, lambda j, i: j)]
        args += (x3, mod)
    else:
        out_spec = pl.BlockSpec((tm, tn), lambda j, i: (i, j))
        out_shape = jax.ShapeDtypeStruct(a2d.shape, F32)
    return pl.pallas_call(
        functools.partial(_glu_kernel, residual=residual),
        grid=(nj, a2d.shape[0] // tm),
        in_specs=in_specs,
        out_specs=out_spec,
        out_shape=out_shape,
        compiler_params=_params("parallel", "parallel"),
        name="glu",
    )(*args)


CONV_CHUNK = SUBLANES * SUBLANES
CONV_ROWS = 256


def _conv_weights(w_ref, b_ref, ls):
    w = [jnp.broadcast_to(w_ref[k:k + 1, ls], (SUBLANES, LANES)) for k in range(CONV_WIDTH)]
    return w, jnp.broadcast_to(b_ref[:, ls], (SUBLANES, LANES))


def _conv_taps(win, w, bias):
    acc = [bias] * SUBLANES
    for o in range(CONV_WIDTH + SUBLANES - 1):
        x = win(o)
        for r in range(SUBLANES):
            if 0 <= o - r < CONV_WIDTH:
                acc[r] = acc[r] + w[o - r] * x
    return acc


def _ln_silu_store(cbuf, lg_ref, lb_ref, o_ref):
    n = cbuf.shape[0]
    inv_d = 1.0 / (n * LANES)
    tot = cbuf[0]
    for l in range(1, n):
        tot = tot + cbuf[l]
    mean = jnp.sum(tot, axis=-1, keepdims=True) * inv_d
    sq = jnp.zeros_like(tot)
    for l in range(n):
        xc = cbuf[l] - mean
        sq = sq + xc * xc
    rstd = lax.rsqrt(jnp.sum(sq, axis=-1, keepdims=True) * inv_d + LN_EPS)
    for l in range(n):
        ls = slice(l * LANES, (l + 1) * LANES)
        y = (cbuf[l] - mean) * rstd * lg_ref[:, ls] + lb_ref[:, ls]
        o_ref[:, ls] = jax.nn.silu(y).astype(BF16)


def _conv_ln_kernel(v_ref, halo_ref, w_ref, b_ref, lg_ref, lb_ref, o_ref, pad, cbuf, *, tiles_per_seq):
    rows = v_ref.shape[0]
    first = (pl.program_id(0) % tiles_per_seq) == 0
    for l in range(N_LANE_TILES):
        ls = slice(l * LANES, (l + 1) * LANES)
        pad[l, 0:HALO, :] = jnp.where(first, 0.0, halo_ref[:, ls])
        pad[l, HALO:, :] = v_ref[:, ls]
        w, bias = _conv_weights(w_ref, b_ref, ls)
        for base in range(0, rows, CONV_CHUNK):
            acc = _conv_taps(
                lambda o: pad[l, pl.ds(base + HIST_OFF + o, SUBLANES, stride=SUBLANES), :], w, bias)
            for r in range(SUBLANES):
                cbuf[l, pl.ds(base + r, SUBLANES, stride=SUBLANES), :] = acc[r]
    _ln_silu_store(cbuf, lg_ref, lb_ref, o_ref)


def _conv_ln_long(tr, v2d, w, b, ln_g, ln_b):
    rows = CONV_ROWS
    assert tr.seq_len % rows == 0 and rows % CONV_CHUNK == 0 and rows % HALO == 0
    hb = rows // HALO
    row = pl.BlockSpec((1, D_MODEL), lambda i: (0, 0))
    return pl.pallas_call(
        functools.partial(_conv_ln_kernel, tiles_per_seq=tr.seq_len // rows),
        grid=(v2d.shape[0] // rows,),
        in_specs=[pl.BlockSpec((rows, D_MODEL), lambda i: (i, 0)),
                  pl.BlockSpec((HALO, D_MODEL), lambda i: (jnp.maximum(i * hb - 1, 0), 0)),
                  pl.BlockSpec((None, CONV_WIDTH, D_MODEL), lambda i: (0, 0, 0)),
                  row, row, row],
        out_specs=pl.BlockSpec((rows, D_MODEL), lambda i: (i, 0)),
        out_shape=jax.ShapeDtypeStruct(v2d.shape, BF16),
        scratch_shapes=[pltpu.VMEM((N_LANE_TILES, HALO + rows, LANES), F32),
                        pltpu.VMEM((N_LANE_TILES, rows, LANES), F32)],
        compiler_params=_params("parallel"),
        name="conv_ln_long",
    )(v2d, v2d, w, b, ln_g, ln_b)


def _conv_ln_step_kernel(v_ref, cache_ref, w_ref, b_ref, lg_ref, lb_ref, o_ref, nc_ref, vs, cbuf):
    for l in range(N_LANE_TILES):
        ls = slice(l * LANES, (l + 1) * LANES)
        vs[l] = v_ref[:, ls]
        new = [vs[l, pl.ds(t, SUBLANES, stride=SUBLANES), :] for t in range(SUBLANES)]

        def padded(o, ls=ls, new=new):
            return cache_ref[o, :, ls] if o < CONV_HIST else new[o - CONV_HIST]

        acc = _conv_taps(padded, *_conv_weights(w_ref, b_ref, ls))
        for t in range(SUBLANES):
            cbuf[l, pl.ds(t, SUBLANES, stride=SUBLANES), :] = acc[t]
        for q in range(CONV_HIST):
            nc_ref[q, :, ls] = padded(q + SUBLANES)
    _ln_silu_store(cbuf, lg_ref, lb_ref, o_ref)


def _conv_ln_step(tr, v2d, cache_t, w, b, ln_g, ln_b):
    assert tr.seq_len == SUBLANES
    rows = SUBLANES * tr.seq_len
    row = pl.BlockSpec((1, D_MODEL), lambda s: (0, 0))
    cspec = pl.BlockSpec((CONV_HIST, SUBLANES, D_MODEL), lambda s: (0, s, 0))
    return pl.pallas_call(
        _conv_ln_step_kernel,
        grid=(tr.n_seq // SUBLANES,),
        in_specs=[pl.BlockSpec((rows, D_MODEL), lambda s: (s, 0)), cspec,
                  pl.BlockSpec((None, CONV_WIDTH, D_MODEL), lambda s: (0, 0, 0)),
                  row, row, row],
        out_specs=[pl.BlockSpec((rows, D_MODEL), lambda s: (s, 0)), cspec],
        out_shape=[jax.ShapeDtypeStruct(v2d.shape, BF16), jax.ShapeDtypeStruct(cache_t.shape, F32)],
        scratch_shapes=[pltpu.VMEM((N_LANE_TILES, rows, LANES), F32),
                        pltpu.VMEM((N_LANE_TILES, rows, LANES), F32)],
        compiler_params=_params("parallel"),
        name="conv_ln_step",
    )(v2d, cache_t, w, b, ln_g, ln_b)


MLP_TILE = 1024
MLP_TK = 2048


def _mlp_up_kernel(h_ref, w_ref, o_ref):
    a = jnp.maximum(_dot(h_ref[...], w_ref[...].astype(BF16)), 0.0)
    o_ref[...] = (a * a).astype(BF16)


def _mlp_up(tr, h2d, w1, layer):
    tm, tn = tr.tm, MLP_TILE
    return pl.pallas_call(
        _mlp_up_kernel,
        grid=(D_FF // tn, h2d.shape[0] // tm),
        in_specs=[pl.BlockSpec((tm, D_MODEL), lambda j, i: (i, 0)),
                  pl.BlockSpec((None, D_MODEL, tn), lambda j, i: (layer, 0, j))],
        out_specs=pl.BlockSpec((tm, tn), lambda j, i: (i, j)),
        out_shape=jax.ShapeDtypeStruct((h2d.shape[0], D_FF), BF16),
        compiler_params=_params("parallel", "parallel"),
        name="mlp_up",
    )(h2d, w1)


def _mm_res_kernel(*refs, has_bias, single_k):
    if has_bias:
        a_ref, w_ref, b_ref, x_ref, gate_ref, o_ref = refs
    else:
        a_ref, w_ref, x_ref, gate_ref, o_ref = refs
    def finish(out):
        if has_bias:
            out = out + b_ref[...]
        o_ref[...] = x_ref[...] + gate_ref[...] * out

    def product():
        return _dot(a_ref[...], w_ref[...].astype(BF16)).reshape(o_ref.shape)

    if single_k:
        finish(product())
        return
    k = pl.program_id(2)

    @pl.when(k == 0)
    def _():
        o_ref[...] = jnp.zeros(o_ref.shape, F32)

    o_ref[...] += product()

    @pl.when(k == pl.num_programs(2) - 1)
    def _():
        finish(o_ref[...])


def _mm_res(tr, a2d, w, w_idx, x3, mod, layer, part, bias=None):
    tm, tn = tr.tm, MLP_TILE
    kdim = a2d.shape[1]
    tk = min(kdim, MLP_TK)
    nb = tm // SUBLANES
    xspec = pl.BlockSpec((nb, SUBLANES, tn), lambda j, i, k: (i, 0, j))
    has_bias = bias is not None
    bias_specs = [pl.BlockSpec((1, tn), lambda j, i, k: (0, j))] if has_bias else []
    bias_args = (bias,) if has_bias else ()
    return pl.pallas_call(
        functools.partial(_mm_res_kernel, has_bias=has_bias, single_k=kdim == tk),
        grid=(D_MODEL // tn, a2d.shape[0] // tm, kdim // tk),
        in_specs=[pl.BlockSpec((tm, tk), lambda j, i, k: (i, k)),
                  pl.BlockSpec((None, tk, tn), lambda j, i, k: (w_idx, k, j))] + bias_specs
        + [xspec, _mod_spec(tr, layer, part, tn, lambda j, i, k: i, lambda j, i, k: j)],
        out_specs=xspec,
        out_shape=jax.ShapeDtypeStruct(x3.shape, F32),
        compiler_params=_params("parallel", "parallel", "arbitrary"),
        name="mm_res",
    )(a2d, w, *bias_args, x3, mod)


def _final_norm_kernel(x_ref, g_ref, o_ref):
    x = x_ref[...]
    ms = jnp.mean(x * x, axis=-1, keepdims=True)
    o_ref[...] = x * lax.rsqrt(ms + RMS_EPS) * g_ref[...]


def _final_norm(tr, x3, g):
    nb = _norm_rows(x3) // SUBLANES
    spec = pl.BlockSpec((nb, SUBLANES, D_MODEL), lambda i: (i, 0, 0))
    return pl.pallas_call(
        _final_norm_kernel,
        grid=(x3.shape[0] // nb,),
        in_specs=[spec, pl.BlockSpec((1, D_MODEL), lambda i: (0, 0))],
        out_specs=spec,
        out_shape=jax.ShapeDtypeStruct(x3.shape, F32),
        compiler_params=_params("parallel"),
        name="final_norm",
    )(x3, g)


def _mlp(tr, x3, g, w1, w2, mod, layer):
    h2d = _prenorm(tr, x3, g, mod, layer, 1, BF16)
    return _mm_res(tr, _mlp_up(tr, h2d, w1, layer), w2, layer, x3, mod, layer, 5)


def _trunk(tr, x, mod, h0, cache, tabs, p):
    tokens = tr.n_seq * tr.seq_len
    x3 = x.reshape(tokens // SUBLANES, SUBLANES, D_MODEL)

    u2d = _prenorm(tr, x3, p["rms_g_mix"], mod, 0, 0, F32)
    if h0 is None:
        g2d, s_re, s_im = _s5_long(tr, u2d, p["ssm_d"], tabs)
    else:
        g2d, s_re, s_im = _s5_step(tr, u2d, p["ssm_d"], tabs, h0)
    x3 = _glu(tr, g2d, p["ssm_w_glu"], p["ssm_b_glu"], x3, mod, 0)
    x3 = _mlp(tr, x3, p["rms_g_mlp"], p["mlp_w1"], p["mlp_w2"], mod, 0)

    h2d = _prenorm(tr, x3, p["rms_g_mix"], mod, 1, 0, BF16)
    v2d = _glu(tr, h2d, p["conv_w_pw1"], p["conv_b_pw1"])
    conv_args = (p["conv_w_dw"], p["conv_b_dw"], p["conv_ln_g"], p["conv_ln_b"])
    if cache is None:
        hc2d = _conv_ln_long(tr, v2d, *conv_args)
        new_cache = v2d.reshape(1, tr.n_seq, tr.seq_len, D_MODEL)[:, :, tr.seq_len - CONV_HIST:]
    else:
        hc2d, cache_t = _conv_ln_step(tr, v2d, jnp.transpose(cache[0], (1, 0, 2)), *conv_args)
        new_cache = jnp.transpose(cache_t, (1, 0, 2))[None]
    x3 = _mm_res(tr, hc2d, p["conv_w_pw2"], 0, x3, mod, 1, 2, bias=p["conv_b_pw2"])
    x3 = _mlp(tr, x3, p["rms_g_mlp"], p["mlp_w1"], p["mlp_w2"], mod, 1)
    y3 = _final_norm(tr, x3, p["final_g"])

    state_shape = (1, tr.n_seq, SSM_GROUPS, SSM_STATE)
    return (y3.reshape(tr.n_seq, tr.seq_len, D_MODEL), s_re.reshape(state_shape), s_im.reshape(state_shape),
            new_cache)


def kernel(x_prompt, x_sample, state_ssm_re, state_ssm_im, cache_conv, c_prompt, c_sample, rms_g_mix, rms_g_mlp, w_ada, b_ada, ssm_a_re, ssm_a_im, ssm_log_dt, ssm_b_re, ssm_b_im, ssm_c_re, ssm_c_im, ssm_d, ssm_w_glu, ssm_b_glu, conv_w_pw1, conv_b_pw1, conv_w_dw, conv_b_dw, conv_ln_g, conv_ln_b, conv_w_pw2, conv_b_pw2, mlp_w1, mlp_w2, final_g):
    bp, lp, _ = x_prompt.shape
    bs, ls, _ = x_sample.shape
    assert w_ada.shape[0] == 2 and ssm_a_re.shape[0] == 1 and conv_w_dw.shape[0] == 1
    prompt = Trunk(bp, lp, 1024, mod_row=bs)
    sample = Trunk(bs, ls, bs * ls, mod_row=0)

    depth = w_ada.shape[0]
    p = dict(rms_g_mix=rms_g_mix.reshape(depth, 1, D_MODEL), rms_g_mlp=rms_g_mlp.reshape(depth, 1, D_MODEL),
             ssm_d=ssm_d, ssm_w_glu=ssm_w_glu, ssm_b_glu=ssm_b_glu,
             conv_w_pw1=conv_w_pw1, conv_b_pw1=conv_b_pw1, conv_w_dw=conv_w_dw, conv_b_dw=conv_b_dw,
             conv_ln_g=conv_ln_g, conv_ln_b=conv_ln_b, conv_w_pw2=conv_w_pw2, conv_b_pw2=conv_b_pw2,
             mlp_w1=mlp_w1, mlp_w2=mlp_w2, final_g=final_g.reshape(1, D_MODEL))

    n_c = bp + bs
    pad_rows = -n_c % SUBLANES
    c_all = jnp.concatenate([c_sample, c_prompt, jnp.zeros((pad_rows, D_MODEL), F32)], axis=0)
    mod = _ada(c_all, w_ada, b_ada)

    tabs = _s5_prep(ssm_a_re[0], ssm_a_im[0], ssm_log_dt[0], ssm_b_re[0], ssm_b_im[0],
                    ssm_c_re[0], ssm_c_im[0])

    n_state = SSM_GROUPS * SSM_STATE
    h0 = (state_ssm_re.reshape(bs, n_state), state_ssm_im.reshape(bs, n_state))
    y_p, p_re, p_im, p_buf = _trunk(prompt, x_prompt, mod, None, None, tabs, p)
    y_s, s_re, s_im, s_buf = _trunk(sample, x_sample, mod, h0, cache_conv, tabs, p)
    return (y_p, y_s, p_re, p_im, p_buf, s_re, s_im, s_buf)
```

```python
import collections
import functools

import jax
import jax.numpy as jnp
from jax import lax
from jax.experimental import pallas as pl
from jax.experimental.pallas import tpu as pltpu

F32 = jnp.float32
BF16 = jnp.bfloat16

D_MODEL = 2048
D_FF = 4 * D_MODEL
SSM_GROUP = 16
SSM_GROUPS = D_MODEL // SSM_GROUP
SSM_STATE = 64
LOG2_GROUP = SSM_GROUP.bit_length() - 1
LOG2_STATE = SSM_STATE.bit_length() - 1
assert SSM_GROUP == 1 << LOG2_GROUP and SSM_STATE == 1 << LOG2_STATE
CONV_WIDTH = 31
CONV_HIST = CONV_WIDTH - 1
RMS_EPS = 1e-6
LN_EPS = 1e-5

LANES = 128
SUBLANES = 8
VMEM_LIMIT_BYTES = 56 * 1024 * 1024

GROUPS_PER_TILE = LANES // SSM_GROUP
STATE_TILE = GROUPS_PER_TILE * SSM_STATE
N_LANE_TILES = D_MODEL // LANES
HALO = 32
HIST_OFF = HALO - CONV_HIST

Trunk = collections.namedtuple("Trunk", "n_seq seq_len tm mod_row")


def _params(*sem):
    return pltpu.CompilerParams(dimension_semantics=sem, vmem_limit_bytes=VMEM_LIMIT_BYTES)


def _dot(a, b):
    return jnp.dot(a, b, preferred_element_type=F32)


def _norm_mod(x3, g, sc, sh):
    ms = jnp.mean(x3 * x3, axis=-1, keepdims=True)
    return x3 * lax.rsqrt(ms + RMS_EPS) * (g * (1.0 + sc)) + sh


def _mod_spec(tr, layer, part, tn, ti, tj):
    nblk = D_MODEL // tn
    if tr.seq_len >= tr.tm:
        per = tr.seq_len // tr.tm
        return pl.BlockSpec((None, 1, 1, tn),
                            lambda *g: (layer, tr.mod_row + ti(*g) // per, 0, part * nblk + tj(*g)))
    nbm = tr.tm // SUBLANES
    assert tr.seq_len == SUBLANES and tr.mod_row % nbm == 0
    return pl.BlockSpec((None, nbm, 1, tn),
                        lambda *g: (layer, tr.mod_row // nbm + ti(*g), 0, part * nblk + tj(*g)))


def _ada_kernel(c_ref, w_ref, b_ref, o_ref):
    ca = jax.nn.silu(c_ref[...]).astype(BF16)
    mod = _dot(ca, w_ref[...].astype(BF16)) + b_ref[...]
    for r in range(o_ref.shape[0]):
        o_ref[r] = mod[r:r + 1, :]


def _ada(c_all, w_ada, b_ada):
    depth, d, n = w_ada.shape
    rows = c_all.shape[0]
    tn = 1024
    return pl.pallas_call(
        _ada_kernel,
        grid=(depth, n // tn),
        in_specs=[pl.BlockSpec((rows, d), lambda l, j: (0, 0)),
                  pl.BlockSpec((None, d, tn), lambda l, j: (l, 0, j)),
                  pl.BlockSpec((None, 1, tn), lambda l, j: (l, 0, j))],
        out_specs=pl.BlockSpec((None, rows, 1, tn), lambda l, j: (l, 0, 0, j)),
        out_shape=jax.ShapeDtypeStruct((depth, rows, 1, n), F32),
        compiler_params=_params("parallel", "parallel"),
        name="ada",
    )(c_all, w_ada, b_ada.reshape(depth, 1, n))


def _norm_rows(x3):
    tokens = x3.shape[0] * SUBLANES
    return min(1024, tokens // SUBLANES)


def _prenorm_kernel(x_ref, g_ref, sc_ref, sh_ref, o_ref):
    h = _norm_mod(x_ref[...], g_ref[...], sc_ref[...], sh_ref[...])
    o_ref[...] = h.reshape(o_ref.shape).astype(o_ref.dtype)


def _prenorm(tr, x3, g, mod, layer, sublayer, dtype):
    tm = _norm_rows(x3)
    trp = tr._replace(tm=tm)
    nb = tm // SUBLANES
    ti, tj = (lambda i: i), (lambda i: 0)
    return pl.pallas_call(
        _prenorm_kernel,
        grid=(x3.shape[0] // nb,),
        in_specs=[pl.BlockSpec((nb, SUBLANES, D_MODEL), lambda i: (i, 0, 0)),
                  pl.BlockSpec((None, 1, D_MODEL), lambda i: (layer, 0, 0)),
                  _mod_spec(trp, layer, 3 * sublayer + 1, D_MODEL, ti, tj),
                  _mod_spec(trp, layer, 3 * sublayer, D_MODEL, ti, tj)],
        out_specs=pl.BlockSpec((tm, D_MODEL), lambda i: (i, 0)),
        out_shape=jax.ShapeDtypeStruct((x3.shape[0] * SUBLANES, D_MODEL), dtype),
        compiler_params=_params("parallel"),
        name="prenorm",
    )(x3, g, mod, mod)


def _s5_prep_kernel(lre_ref, lim_ref, ldt_ref, bre_ref, bim_ref, cre_ref, cim_ref,
                    are_ref, aim_ref, bbre_ref, bbim_ref, ccre_ref, ccim_ref):
    lr, li = lre_ref[...], lim_ref[...]
    dt = jnp.exp(ldt_ref[...])
    mag = jnp.exp(lr * dt)
    are = mag * jnp.cos(li * dt)
    aim = mag * jnp.sin(li * dt)
    er, ei = are - 1.0, aim
    den = lr * lr + li * li
    qre = (er * lr + ei * li) / den
    qim = (ei * lr - er * li) / den
    are_ref[...] = are
    aim_ref[...] = aim

    br, bi = bre_ref[...], bim_ref[...]
    keep = (jnp.right_shift(lax.broadcasted_iota(jnp.int32, br.shape, 0), LOG2_GROUP)
            == jnp.right_shift(lax.broadcasted_iota(jnp.int32, br.shape, 1), LOG2_STATE))
    bbre_ref[...] = jnp.where(keep, qre * br - qim * bi, 0.0).astype(BF16)
    bbim_ref[...] = jnp.where(keep, qre * bi + qim * br, 0.0).astype(BF16)

    cr, ci = cre_ref[...], cim_ref[...]
    keep = (jnp.right_shift(lax.broadcasted_iota(jnp.int32, cr.shape, 0), LOG2_STATE)
            == jnp.right_shift(lax.broadcasted_iota(jnp.int32, cr.shape, 1), LOG2_GROUP))
    ccre_ref[...] = jnp.where(keep, cr, 0.0).astype(BF16)
    ccim_ref[...] = jnp.where(keep, -ci, 0.0).astype(BF16)


def _s5_prep(a_re, a_im, log_dt, b_re, b_im, c_re, c_im):
    nt = N_LANE_TILES
    tile3 = lambda a: a.reshape(nt, 1, STATE_TILE)
    ldt = jnp.broadcast_to(log_dt[:, None], (SSM_GROUPS, SSM_STATE))
    b_rows = lambda b: jnp.tile(b.transpose(0, 2, 1).reshape(D_MODEL, SSM_STATE), (1, GROUPS_PER_TILE))
    c_rows = lambda c: jnp.tile(c.transpose(0, 2, 1).reshape(SSM_GROUPS * SSM_STATE, SSM_GROUP),
                                (1, GROUPS_PER_TILE))
    vec = pl.BlockSpec((None, 1, STATE_TILE), lambda k: (k, 0, 0))
    bspec = pl.BlockSpec((LANES, STATE_TILE), lambda k: (k, 0))
    cspec = pl.BlockSpec((STATE_TILE, LANES), lambda k: (k, 0))
    return pl.pallas_call(
        _s5_prep_kernel,
        grid=(nt,),
        in_specs=[vec, vec, vec, bspec, bspec, cspec, cspec],
        out_specs=[vec, vec, bspec, bspec, cspec, cspec],
        out_shape=[jax.ShapeDtypeStruct((nt, 1, STATE_TILE), F32)] * 2
        + [jax.ShapeDtypeStruct((D_MODEL, STATE_TILE), BF16)] * 2
        + [jax.ShapeDtypeStruct((SSM_GROUPS * SSM_STATE, LANES), BF16)] * 2,
        compiler_params=_params("parallel"),
        name="s5_prep",
    )(tile3(a_re), tile3(a_im), tile3(ldt), b_rows(b_re), b_rows(b_im), c_rows(c_re), c_rows(c_im))


S5_CHUNK = 256
S5_STEP_TILES = 2


def _s5_step_kernel(u_ref, d_ref, bbre_ref, bbim_ref, ccre_ref, ccim_ref, are_ref, aim_ref, x0_ref, x1_ref,
                    g_ref, sre_ref, sim_ref, up, gp, gn, *chunks, n_blocks, seg_len):
    n_chunks = len(chunks) // 2
    hre, him = chunks[:n_chunks], chunks[n_chunks:]

    def natural(lane):
        return pl.ds(pl.multiple_of(lane * seg_len, SUBLANES), seg_len)

    def regrouped(lane):
        return pl.ds((lane // SUBLANES) * (seg_len * SUBLANES) + lane % SUBLANES, seg_len, stride=SUBLANES)

    def for_each_lane(body):
        lax.fori_loop(0, n_blocks * SUBLANES, body, 0, unroll=8)

    groups_per_chunk = S5_CHUNK // SUBLANES

    def chunk_rows(q):
        return slice(q * S5_CHUNK, (q + 1) * S5_CHUNK)

    def local_rows(i):
        j = i % groups_per_chunk
        return slice(j * SUBLANES, (j + 1) * SUBLANES)

    for kk in range(are_ref.shape[0]):
        ls = slice(kk * LANES, (kk + 1) * LANES)
        ss = slice(kk * STATE_TILE, (kk + 1) * STATE_TILE)

        def regroup(lane, c):
            up[regrouped(lane), :] = u_ref[natural(lane), ls]
            return c

        for_each_lane(regroup)

        ar = jnp.broadcast_to(are_ref[kk], (SUBLANES, STATE_TILE))
        ai = jnp.broadcast_to(aim_ref[kk], (SUBLANES, STATE_TILE))
        d = d_ref[:, ls]

        def input_chunk(q):
            ub = up[chunk_rows(q), :].astype(BF16)
            hre[q][...] = _dot(ub, bbre_ref[ls, :])
            him[q][...] = _dot(ub, bbim_ref[ls, :])

        def scan_chunk(q, carry):
            for i in range(q * groups_per_chunk, (q + 1) * groups_per_chunk):
                nb, r = i // seg_len, local_rows(i)
                blk = slice(nb * SUBLANES, (nb + 1) * SUBLANES)
                if i % seg_len == 0:
                    carry = (x0_ref[blk, ss], x1_ref[blk, ss])
                hr, hi = carry
                carry = (ar * hr - ai * hi + hre[q][r, :], ar * hi + ai * hr + him[q][r, :])
                hre[q][r, :], him[q][r, :] = carry
                if i % seg_len == seg_len - 1:
                    sre_ref[blk, ss], sim_ref[blk, ss] = carry
            return carry

        def output_chunk(q):
            y = _dot(hre[q][...].astype(BF16), ccre_ref[ss, :]) + _dot(him[q][...].astype(BF16), ccim_ref[ss, :])
            gp[chunk_rows(q), :] = jax.nn.gelu(y + d * up[chunk_rows(q), :])

        input_chunk(0)
        carry = None
        for q in range(n_chunks):
            if q + 1 < n_chunks:
                input_chunk(q + 1)
            carry = scan_chunk(q, carry)
            if q >= 1:
                output_chunk(q - 1)
        output_chunk(n_chunks - 1)

        def ungroup(lane, c):
            gn[natural(lane), :] = gp[regrouped(lane), :]
            return c

        for_each_lane(ungroup)
        g_ref[:, ls] = gn[...].astype(BF16)


def _s5_step(tr, u2d, d_skip, tabs, h0):
    are, aim, bbre, bbim, ccre, ccim = tabs
    rows = tr.n_seq * tr.seq_len
    assert rows % S5_CHUNK == 0 and tr.n_seq % SUBLANES == 0 and tr.seq_len % SUBLANES == 0
    nt = S5_STEP_TILES
    assert N_LANE_TILES % nt == 0
    vec = pl.BlockSpec((nt, 1, STATE_TILE), lambda k: (k, 0, 0))
    bspec = pl.BlockSpec((nt * LANES, STATE_TILE), lambda k: (k, 0))
    cspec = pl.BlockSpec((nt * STATE_TILE, LANES), lambda k: (k, 0))
    tok = pl.BlockSpec((rows, nt * LANES), lambda k: (0, k))
    st_spec = pl.BlockSpec((tr.n_seq, nt * STATE_TILE), lambda k: (0, k))
    st_shape = jax.ShapeDtypeStruct((tr.n_seq, SSM_GROUPS * SSM_STATE), F32)
    return pl.pallas_call(
        functools.partial(_s5_step_kernel, n_blocks=tr.n_seq // SUBLANES, seg_len=tr.seq_len),
        grid=(N_LANE_TILES // nt,),
        in_specs=[tok, pl.BlockSpec((1, nt * LANES), lambda k: (0, k)), bspec, bspec, cspec, cspec, vec, vec,
                  st_spec, st_spec],
        out_specs=[tok, st_spec, st_spec],
        out_shape=[jax.ShapeDtypeStruct(u2d.shape, BF16), st_shape, st_shape],
        scratch_shapes=[pltpu.VMEM((rows, LANES), F32)] * 3
        + [pltpu.VMEM((S5_CHUNK, STATE_TILE), F32)] * (2 * (rows // S5_CHUNK)),
        compiler_params=_params("parallel"),
        name="s5_step",
    )(u2d, d_skip, bbre, bbim, ccre, ccim, are, aim, *h0)


S5_STEPS = 512


def _s5_long_kernel(u_ref, d_ref, bbre_ref, bbim_ref, ccre_ref, ccim_ref, are_ref, aim_ref,
                    g_ref, sre_ref, sim_ref, up, ys, cre, cim, *chunks, n_seq, n_tiles):
    n_chunks = len(chunks) // 2
    hre, him = chunks[:n_chunks], chunks[n_chunks:]
    rows = S5_STEPS * SUBLANES
    c = pl.program_id(1)

    @pl.when(c == 0)
    def _():
        cre[...] = jnp.zeros(cre.shape, F32)
        cim[...] = jnp.zeros(cim.shape, F32)

    def lane_rows(b, kk):
        return pl.ds(b * n_tiles + kk, S5_STEPS, stride=SUBLANES)

    def tile_lanes(kk):
        return slice(kk * LANES, (kk + 1) * LANES)

    for kk in range(n_tiles):
        up[kk] = jnp.zeros((rows, LANES), F32)
        for b in range(n_seq):
            up[kk, lane_rows(b, kk), :] = u_ref[b, :, tile_lanes(kk)]

    tile_of_row = jnp.bitwise_and(lax.broadcasted_iota(jnp.int32, (SUBLANES, STATE_TILE), 0), n_tiles - 1)
    ar = jnp.broadcast_to(are_ref[0], (SUBLANES, STATE_TILE))
    ai = jnp.broadcast_to(aim_ref[0], (SUBLANES, STATE_TILE))
    for kk in range(1, n_tiles):
        ar = jnp.where(tile_of_row == kk, are_ref[kk], ar)
        ai = jnp.where(tile_of_row == kk, aim_ref[kk], ai)
    state_rows = lambda kk: slice(kk * STATE_TILE, (kk + 1) * STATE_TILE)
    c_re = jnp.concatenate([ccre_ref[state_rows(kk), :] for kk in range(n_tiles)], axis=-1)
    c_im = jnp.concatenate([ccim_ref[state_rows(kk), :] for kk in range(n_tiles)], axis=-1)
    groups_per_chunk = S5_CHUNK // SUBLANES

    def chunk_rows(q):
        return slice(q * S5_CHUNK, (q + 1) * S5_CHUNK)

    def input_chunk(q):
        ub = jnp.concatenate([up[kk, chunk_rows(q), :] for kk in range(n_tiles)], axis=-1).astype(BF16)
        hre[q][...] = _dot(ub, bbre_ref[...])
        him[q][...] = _dot(ub, bbim_ref[...])

    def scan_chunk(q, carry):
        for j in range(groups_per_chunk):
            r = slice(j * SUBLANES, (j + 1) * SUBLANES)
            hr, hi = carry
            carry = (ar * hr - ai * hi + hre[q][r, :], ar * hi + ai * hr + him[q][r, :])
            hre[q][r, :], him[q][r, :] = carry
        return carry

    def output_chunk(q):
        y = _dot(hre[q][...].astype(BF16), c_re) + _dot(him[q][...].astype(BF16), c_im)
        for kk in range(n_tiles):
            ys[kk, chunk_rows(q), :] = y[:, tile_lanes(kk)]

    input_chunk(0)
    carry = (cre[...], cim[...])
    for q in range(n_chunks):
        if q + 1 < n_chunks:
            input_chunk(q + 1)
        carry = scan_chunk(q, carry)
        if q >= 1:
            output_chunk(q - 1)
    output_chunk(n_chunks - 1)
    cre[...], cim[...] = carry

    d = d_ref[...]
    for kk in range(n_tiles):
        for b in range(n_seq):
            u = u_ref[b, :, tile_lanes(kk)]
            y = ys[kk, lane_rows(b, kk), :]
            g_ref[b, :, tile_lanes(kk)] = jax.nn.gelu(y + d[:, tile_lanes(kk)] * u).astype(BF16)

    @pl.when(c == pl.num_programs(1) - 1)
    def _():
        for kk in range(n_tiles):
            for b in range(n_seq):
                lane = b * n_tiles + kk
                sre_ref[b, :, state_rows(kk)] = cre[lane:lane + 1, :]
                sim_ref[b, :, state_rows(kk)] = cim[lane:lane + 1, :]


def _s5_long(tr, u2d, d_skip, tabs):
    are, aim, bbre, bbim, ccre, ccim = tabs
    n_seq, seq_len = tr.n_seq, tr.seq_len
    assert SUBLANES % n_seq == 0 and seq_len % S5_STEPS == 0
    n_tiles = SUBLANES // n_seq
    assert N_LANE_TILES % n_tiles == 0 and n_tiles == 1 << (n_tiles.bit_length() - 1)
    cw = n_tiles * LANES
    rows = S5_STEPS * SUBLANES
    n_chunks = rows // S5_CHUNK
    n_state = SSM_GROUPS * SSM_STATE
    tok = pl.BlockSpec((n_seq, S5_STEPS, cw), lambda k, c: (0, c, k))
    vec = pl.BlockSpec((n_tiles, 1, STATE_TILE), lambda k, c: (k, 0, 0))
    bspec = pl.BlockSpec((cw, STATE_TILE), lambda k, c: (k, 0))
    cspec = pl.BlockSpec((n_tiles * STATE_TILE, LANES), lambda k, c: (k, 0))
    st_spec = pl.BlockSpec((n_seq, 1, n_tiles * STATE_TILE), lambda k, c: (0, 0, k))
    st_shape = jax.ShapeDtypeStruct((n_seq, 1, n_state), F32)
    g3, s_re, s_im = pl.pallas_call(
        functools.partial(_s5_long_kernel, n_seq=n_seq, n_tiles=n_tiles),
        grid=(N_LANE_TILES // n_tiles, seq_len // S5_STEPS),
        in_specs=[tok, pl.BlockSpec((1, cw), lambda k, c: (0, k)), bspec, bspec, cspec, cspec, vec, vec],
        out_specs=[tok, st_spec, st_spec],
        out_shape=[jax.ShapeDtypeStruct((n_seq, seq_len, D_MODEL), BF16), st_shape, st_shape],
        scratch_shapes=[pltpu.VMEM((n_tiles, rows, LANES), F32)] * 2
        + [pltpu.VMEM((SUBLANES, STATE_TILE), F32)] * 2
        + [pltpu.VMEM((S5_CHUNK, STATE_TILE), F32)] * (2 * n_chunks),
        compiler_params=_params("parallel", "arbitrary"),
        name="s5_long",
    )(u2d.reshape(n_seq, seq_len, D_MODEL), d_skip, bbre, bbim, ccre, ccim, are, aim)
    return g3.reshape(n_seq * seq_len, D_MODEL), s_re, s_im


GLU_COLS = 256


def _glu_kernel(*refs, residual):
    if residual:
        a_ref, wa_ref, wb_ref, ba_ref, bb_ref, x_ref, gate_ref, o_ref = refs
    else:
        a_ref, wa_ref, wb_ref, ba_ref, bb_ref, o_ref = refs
    a = a_ref[...]
    tn = wa_ref.shape[1]
    for c in range(0, tn, GLU_COLS):
        cs = slice(c, c + GLU_COLS)
        za = _dot(a, wa_ref[:, cs].astype(BF16)) + ba_ref[:, cs]
        zb = _dot(a, wb_ref[:, cs].astype(BF16)) + bb_ref[:, cs]
        out = za * jax.nn.sigmoid(zb)
        if residual:
            o_ref[:, :, cs] = x_ref[:, :, cs] + gate_ref[:, :, cs] * out.reshape(o_ref.shape[:2] + (GLU_COLS,))
        else:
            o_ref[:, cs] = out


def _glu(tr, a2d, w, b, x3=None, mod=None, layer=None):
    tm, tn = tr.tm, 512
    nb, nj = tm // SUBLANES, D_MODEL // tn
    b3 = b.reshape(1, 1, 2 * D_MODEL)
    residual = x3 is not None
    in_specs = [pl.BlockSpec((tm, D_MODEL), lambda j, i: (i, 0)),
                pl.BlockSpec((None, D_MODEL, tn), lambda j, i: (0, 0, j)),
                pl.BlockSpec((None, D_MODEL, tn), lambda j, i: (0, 0, j + nj)),
                pl.BlockSpec((None, 1, tn), lambda j, i: (0, 0, j)),
                pl.BlockSpec((None, 1, tn), lambda j, i: (0, 0, j + nj))]
    args = (a2d, w, w, b3, b3)
    if residual:
        out_spec = pl.BlockSpec((nb, SUBLANES, tn), lambda j, i: (i, 0, j))
        out_shape = jax.ShapeDtypeStruct(x3.shape, F32)
        in_specs += [out_spec, _mod_spec(tr, layer, 2, tn, lambda j, i: i, lambda j, i: j)]
        args += (x3, mod)
    else:
        out_spec = pl.BlockSpec((tm, tn), lambda j, i: (i, j))
        out_shape = jax.ShapeDtypeStruct(a2d.shape, F32)
    return pl.pallas_call(
        functools.partial(_glu_kernel, residual=residual),
        grid=(nj, a2d.shape[0] // tm),
        in_specs=in_specs,
        out_specs=out_spec,
        out_shape=out_shape,
        compiler_params=_params("parallel", "parallel"),
        name="glu",
    )(*args)


CONV_CHUNK = SUBLANES * SUBLANES
CONV_ROWS = 256


def _conv_weights(w_ref, b_ref, ls):
    w = [jnp.broadcast_to(w_ref[k:k + 1, ls], (SUBLANES, LANES)) for k in range(CONV_WIDTH)]
    return w, jnp.broadcast_to(b_ref[:, ls], (SUBLANES, LANES))


def _conv_taps(win, w, bias):
    acc = [bias] * SUBLANES
    for o in range(CONV_WIDTH + SUBLANES - 1):
        x = win(o)
        for r in range(SUBLANES):
            if 0 <= o - r < CONV_WIDTH:
                acc[r] = acc[r] + w[o - r] * x
    return acc


def _ln_silu_store(cbuf, lg_ref, lb_ref, o_ref):
    n = cbuf.shape[0]
    inv_d = 1.0 / (n * LANES)
    tot = cbuf[0]
    for l in range(1, n):
        tot = tot + cbuf[l]
    mean = jnp.sum(tot, axis=-1, keepdims=True) * inv_d
    sq = jnp.zeros_like(tot)
    for l in range(n):
        xc = cbuf[l] - mean
        sq = sq + xc * xc
    rstd = lax.rsqrt(jnp.sum(sq, axis=-1, keepdims=True) * inv_d + LN_EPS)
    for l in range(n):
        ls = slice(l * LANES, (l + 1) * LANES)
        y = (cbuf[l] - mean) * rstd * lg_ref[:, ls] + lb_ref[:, ls]
        o_ref[:, ls] = jax.nn.silu(y).astype(BF16)


def _conv_ln_kernel(v_ref, halo_ref, w_ref, b_ref, lg_ref, lb_ref, o_ref, pad, cbuf, *, tiles_per_seq):
    rows = v_ref.shape[0]
    first = (pl.program_id(0) % tiles_per_seq) == 0
    for l in range(N_LANE_TILES):
        ls = slice(l * LANES, (l + 1) * LANES)
        pad[l, 0:HALO, :] = jnp.where(first, 0.0, halo_ref[:, ls])
        pad[l, HALO:, :] = v_ref[:, ls]
        w, bias = _conv_weights(w_ref, b_ref, ls)
        for base in range(0, rows, CONV_CHUNK):
            acc = _conv_taps(
                lambda o: pad[l, pl.ds(base + HIST_OFF + o, SUBLANES, stride=SUBLANES), :], w, bias)
            for r in range(SUBLANES):
                cbuf[l, pl.ds(base + r, SUBLANES, stride=SUBLANES), :] = acc[r]
    _ln_silu_store(cbuf, lg_ref, lb_ref, o_ref)


def _conv_ln_long(tr, v2d, w, b, ln_g, ln_b):
    rows = CONV_ROWS
    assert tr.seq_len % rows == 0 and rows % CONV_CHUNK == 0 and rows % HALO == 0
    hb = rows // HALO
    row = pl.BlockSpec((1, D_MODEL), lambda i: (0, 0))
    return pl.pallas_call(
        functools.partial(_conv_ln_kernel, tiles_per_seq=tr.seq_len // rows),
        grid=(v2d.shape[0] // rows,),
        in_specs=[pl.BlockSpec((rows, D_MODEL), lambda i: (i, 0)),
                  pl.BlockSpec((HALO, D_MODEL), lambda i: (jnp.maximum(i * hb - 1, 0), 0)),
                  pl.BlockSpec((None, CONV_WIDTH, D_MODEL), lambda i: (0, 0, 0)),
                  row, row, row],
        out_specs=pl.BlockSpec((rows, D_MODEL), lambda i: (i, 0)),
        out_shape=jax.ShapeDtypeStruct(v2d.shape, BF16),
        scratch_shapes=[pltpu.VMEM((N_LANE_TILES, HALO + rows, LANES), F32),
                        pltpu.VMEM((N_LANE_TILES, rows, LANES), F32)],
        compiler_params=_params("parallel"),
        name="conv_ln_long",
    )(v2d, v2d, w, b, ln_g, ln_b)


def _conv_ln_step_kernel(v_ref, cache_ref, w_ref, b_ref, lg_ref, lb_ref, o_ref, nc_ref, vs, cbuf):
    for l in range(N_LANE_TILES):
        ls = slice(l * LANES, (l + 1) * LANES)
        vs[l] = v_ref[:, ls]
        new = [vs[l, pl.ds(t, SUBLANES, stride=SUBLANES), :] for t in range(SUBLANES)]

        def padded(o, ls=ls, new=new):
            return cache_ref[o, :, ls] if o < CONV_HIST else new[o - CONV_HIST]

        acc = _conv_taps(padded, *_conv_weights(w_ref, b_ref, ls))
        for t in range(SUBLANES):
            cbuf[l, pl.ds(t, SUBLANES, stride=SUBLANES), :] = acc[t]
        for q in range(CONV_HIST):
            nc_ref[q, :, ls] = padded(q + SUBLANES)
    _ln_silu_store(cbuf, lg_ref, lb_ref, o_ref)


def _conv_ln_step(tr, v2d, cache_t, w, b, ln_g, ln_b):
    assert tr.seq_len == SUBLANES
    rows = SUBLANES * tr.seq_len
    row = pl.BlockSpec((1, D_MODEL), lambda s: (0, 0))
    cspec = pl.BlockSpec((CONV_HIST, SUBLANES, D_MODEL), lambda s: (0, s, 0))
    return pl.pallas_call(
        _conv_ln_step_kernel,
        grid=(tr.n_seq // SUBLANES,),
        in_specs=[pl.BlockSpec((rows, D_MODEL), lambda s: (s, 0)), cspec,
                  pl.BlockSpec((None, CONV_WIDTH, D_MODEL), lambda s: (0, 0, 0)),
                  row, row, row],
        out_specs=[pl.BlockSpec((rows, D_MODEL), lambda s: (s, 0)), cspec],
        out_shape=[jax.ShapeDtypeStruct(v2d.shape, BF16), jax.ShapeDtypeStruct(cache_t.shape, F32)],
        scratch_shapes=[pltpu.VMEM((N_LANE_TILES, rows, LANES), F32),
                        pltpu.VMEM((N_LANE_TILES, rows, LANES), F32)],
        compiler_params=_params("parallel"),
        name="conv_ln_step",
    )(v2d, cache_t, w, b, ln_g, ln_b)


MLP_TILE = 1024
MLP_TK = 2048


def _mlp_up_kernel(h_ref, w_ref, o_ref):
    a = jnp.maximum(_dot(h_ref[...], w_ref[...].astype(BF16)), 0.0)
    o_ref[...] = (a * a).astype(BF16)


def _mlp_up(tr, h2d, w1, layer):
    tm, tn = tr.tm, MLP_TILE
    return pl.pallas_call(
        _mlp_up_kernel,
        grid=(D_FF // tn, h2d.shape[0] // tm),
        in_specs=[pl.BlockSpec((tm, D_MODEL), lambda j, i: (i, 0)),
                  pl.BlockSpec((None, D_MODEL, tn), lambda j, i: (layer, 0, j))],
        out_specs=pl.BlockSpec((tm, tn), lambda j, i: (i, j)),
        out_shape=jax.ShapeDtypeStruct((h2d.shape[0], D_FF), BF16),
        compiler_params=_params("parallel", "parallel"),
        name="mlp_up",
    )(h2d, w1)


def _mm_res_kernel(*refs, has_bias, single_k):
    if has_bias:
        a_ref, w_ref, b_ref, x_ref, gate_ref, o_ref = refs
    else:
        a_ref, w_ref, x_ref, gate_ref, o_ref = refs
    def finish(out):
        if has_bias:
            out = out + b_ref[...]
        o_ref[...] = x_ref[...] + gate_ref[...] * out

    def product():
        return _dot(a_ref[...], w_ref[...].astype(BF16)).reshape(o_ref.shape)

    if single_k:
        finish(product())
        return
    k = pl.program_id(2)

    @pl.when(k == 0)
    def _():
        o_ref[...] = jnp.zeros(o_ref.shape, F32)

    o_ref[...] += product()

    @pl.when(k == pl.num_programs(2) - 1)
    def _():
        finish(o_ref[...])


def _mm_res(tr, a2d, w, w_idx, x3, mod, layer, part, bias=None):
    tm, tn = tr.tm, MLP_TILE
    kdim = a2d.shape[1]
    tk = min(kdim, MLP_TK)
    nb = tm // SUBLANES
    xspec = pl.BlockSpec((nb, SUBLANES, tn), lambda j, i, k: (i, 0, j))
    has_bias = bias is not None
    bias_specs = [pl.BlockSpec((1, tn), lambda j, i, k: (0, j))] if has_bias else []
    bias_args = (bias,) if has_bias else ()
    return pl.pallas_call(
        functools.partial(_mm_res_kernel, has_bias=has_bias, single_k=kdim == tk),
        grid=(D_MODEL // tn, a2d.shape[0] // tm, kdim // tk),
        in_specs=[pl.BlockSpec((tm, tk), lambda j, i, k: (i, k)),
                  pl.BlockSpec((None, tk, tn), lambda j, i, k: (w_idx, k, j))] + bias_specs
        + [xspec, _mod_spec(tr, layer, part, tn, lambda j, i, k: i, lambda j, i, k: j)],
        out_specs=xspec,
        out_shape=jax.ShapeDtypeStruct(x3.shape, F32),
        compiler_params=_params("parallel", "parallel", "arbitrary"),
        name="mm_res",
    )(a2d, w, *bias_args, x3, mod)


def _final_norm_kernel(x_ref, g_ref, o_ref):
    x = x_ref[...]
    ms = jnp.mean(x * x, axis=-1, keepdims=True)
    o_ref[...] = x * lax.rsqrt(ms + RMS_EPS) * g_ref[...]


def _final_norm(tr, x3, g):
    nb = _norm_rows(x3) // SUBLANES
    spec = pl.BlockSpec((nb, SUBLANES, D_MODEL), lambda i: (i, 0, 0))
    return pl.pallas_call(
        _final_norm_kernel,
        grid=(x3.shape[0] // nb,),
        in_specs=[spec, pl.BlockSpec((1, D_MODEL), lambda i: (0, 0))],
        out_specs=spec,
        out_shape=jax.ShapeDtypeStruct(x3.shape, F32),
        compiler_params=_params("parallel"),
        name="final_norm",
    )(x3, g)


def _mlp(tr, x3, g, w1, w2, mod, layer):
    h2d = _prenorm(tr, x3, g, mod, layer, 1, BF16)
    return _mm_res(tr, _mlp_up(tr, h2d, w1, layer), w2, layer, x3, mod, layer, 5)


def _trunk(tr, x, mod, h0, cache, tabs, p):
    tokens = tr.n_seq * tr.seq_len
    x3 = x.reshape(tokens // SUBLANES, SUBLANES, D_MODEL)

    u2d = _prenorm(tr, x3, p["rms_g_mix"], mod, 0, 0, F32)
    if h0 is None:
        g2d, s_re, s_im = _s5_long(tr, u2d, p["ssm_d"], tabs)
    else:
        g2d, s_re, s_im = _s5_step(tr, u2d, p["ssm_d"], tabs, h0)
    x3 = _glu(tr, g2d, p["ssm_w_glu"], p["ssm_b_glu"], x3, mod, 0)
    x3 = _mlp(tr, x3, p["rms_g_mlp"], p["mlp_w1"], p["mlp_w2"], mod, 0)

    h2d = _prenorm(tr, x3, p["rms_g_mix"], mod, 1, 0, BF16)
    v2d = _glu(tr, h2d, p["conv_w_pw1"], p["conv_b_pw1"])
    conv_args = (p["conv_w_dw"], p["conv_b_dw"], p["conv_ln_g"], p["conv_ln_b"])
    if cache is None:
        hc2d = _conv_ln_long(tr, v2d, *conv_args)
        new_cache = v2d.reshape(1, tr.n_seq, tr.seq_len, D_MODEL)[:, :, tr.seq_len - CONV_HIST:]
    else:
        hc2d, cache_t = _conv_ln_step(tr, v2d, jnp.transpose(cache[0], (1, 0, 2)), *conv_args)
        new_cache = jnp.transpose(cache_t, (1, 0, 2))[None]
    x3 = _mm_res(tr, hc2d, p["conv_w_pw2"], 0, x3, mod, 1, 2, bias=p["conv_b_pw2"])
    x3 = _mlp(tr, x3, p["rms_g_mlp"], p["mlp_w1"], p["mlp_w2"], mod, 1)
    y3 = _final_norm(tr, x3, p["final_g"])

    state_shape = (1, tr.n_seq, SSM_GROUPS, SSM_STATE)
    return (y3.reshape(tr.n_seq, tr.seq_len, D_MODEL), s_re.reshape(state_shape), s_im.reshape(state_shape),
            new_cache)


def kernel(x_prompt, x_sample, state_ssm_re, state_ssm_im, cache_conv, c_prompt, c_sample, rms_g_mix, rms_g_mlp, w_ada, b_ada, ssm_a_re, ssm_a_im, ssm_log_dt, ssm_b_re, ssm_b_im, ssm_c_re, ssm_c_im, ssm_d, ssm_w_glu, ssm_b_glu, conv_w_pw1, conv_b_pw1, conv_w_dw, conv_b_dw, conv_ln_g, conv_ln_b, conv_w_pw2, conv_b_pw2, mlp_w1, mlp_w2, final_g):
    bp, lp, _ = x_prompt.shape
    bs, ls, _ = x_sample.shape
    assert w_ada.shape[0] == 2 and ssm_a_re.shape[0] == 1 and conv_w_dw.shape[0] == 1
    prompt = Trunk(bp, lp, 1024, mod_row=bs)
    sample = Trunk(bs, ls, bs * ls, mod_row=0)

    depth = w_ada.shape[0]
    p = dict(rms_g_mix=rms_g_mix.reshape(depth, 1, D_MODEL), rms_g_mlp=rms_g_mlp.reshape(depth, 1, D_MODEL),
             ssm_d=ssm_d, ssm_w_glu=ssm_w_glu, ssm_b_glu=ssm_b_glu,
             conv_w_pw1=conv_w_pw1, conv_b_pw1=conv_b_pw1, conv_w_dw=conv_w_dw, conv_b_dw=conv_b_dw,
             conv_ln_g=conv_ln_g, conv_ln_b=conv_ln_b, conv_w_pw2=conv_w_pw2, conv_b_pw2=conv_b_pw2,
             mlp_w1=mlp_w1, mlp_w2=mlp_w2, final_g=final_g.reshape(1, D_MODEL))

    n_c = bp + bs
    pad_rows = -n_c % SUBLANES
    c_all = jnp.concatenate([c_sample, c_prompt, jnp.zeros((pad_rows, D_MODEL), F32)], axis=0)
    mod = _ada(c_all, w_ada, b_ada)

    tabs = _s5_prep(ssm_a_re[0], ssm_a_im[0], ssm_log_dt[0], ssm_b_re[0], ssm_b_im[0],
                    ssm_c_re[0], ssm_c_im[0])

    n_state = SSM_GROUPS * SSM_STATE
    h0 = (state_ssm_re.reshape(bs, n_state), state_ssm_im.reshape(bs, n_state))
    y_p, p_re, p_im, p_buf = _trunk(prompt, x_prompt, mod, None, None, tabs, p)
    y_s, s_re, s_im, s_buf = _trunk(sample, x_sample, mod, h0, cache_conv, tabs, p)
    return (y_p, y_s, p_re, p_im, p_buf, s_re, s_im, s_buf)
```

```python
import collections
import functools

import jax
import jax.numpy as jnp
from jax import lax
from jax.experimental import pallas as pl
from jax.experimental.pallas import tpu as pltpu

F32 = jnp.float32
BF16 = jnp.bfloat16

D_MODEL = 2048
D_FF = 4 * D_MODEL
SSM_GROUP = 16
SSM_GROUPS = D_MODEL // SSM_GROUP
SSM_STATE = 64
LOG2_GROUP = SSM_GROUP.bit_length() - 1
LOG2_STATE = SSM_STATE.bit_length() - 1
assert SSM_GROUP == 1 << LOG2_GROUP and SSM_STATE == 1 << LOG2_STATE
CONV_WIDTH = 31
CONV_HIST = CONV_WIDTH - 1
RMS_EPS = 1e-6
LN_EPS = 1e-5

LANES = 128
SUBLANES = 8
VMEM_LIMIT_BYTES = 56 * 1024 * 1024

GROUPS_PER_TILE = LANES // SSM_GROUP
STATE_TILE = GROUPS_PER_TILE * SSM_STATE
N_LANE_TILES = D_MODEL // LANES
HALO = 32
HIST_OFF = HALO - CONV_HIST

Trunk = collections.namedtuple("Trunk", "n_seq seq_len tm mod_row")


def _params(*sem):
    return pltpu.CompilerParams(dimension_semantics=sem, vmem_limit_bytes=VMEM_LIMIT_BYTES)


def _dot(a, b):
    return jnp.dot(a, b, preferred_element_type=F32)


def _norm_mod(x3, g, sc, sh):
    ms = jnp.mean(x3 * x3, axis=-1, keepdims=True)
    return x3 * lax.rsqrt(ms + RMS_EPS) * (g * (1.0 + sc)) + sh


def _mod_spec(tr, layer, part, tn, ti, tj):
    nblk = D_MODEL // tn
    if tr.seq_len >= tr.tm:
        per = tr.seq_len // tr.tm
        return pl.BlockSpec((None, 1, 1, tn),
                            lambda *g: (layer, tr.mod_row + ti(*g) // per, 0, part * nblk + tj(*g)))
    nbm = tr.tm // SUBLANES
    assert tr.seq_len == SUBLANES and tr.mod_row % nbm == 0
    return pl.BlockSpec((None, nbm, 1, tn),
                        lambda *g: (layer, tr.mod_row // nbm + ti(*g), 0, part * nblk + tj(*g)))


def _ada_kernel(c_ref, w_ref, b_ref, o_ref):
    ca = jax.nn.silu(c_ref[...]).astype(BF16)
    mod = _dot(ca, w_ref[...].astype(BF16)) + b_ref[...]
    for r in range(o_ref.shape[0]):
        o_ref[r] = mod[r:r + 1, :]


def _ada(c_all, w_ada, b_ada):
    depth, d, n = w_ada.shape
    rows = c_all.shape[0]
    tn = 1024
    return pl.pallas_call(
        _ada_kernel,
        grid=(depth, n // tn),
        in_specs=[pl.BlockSpec((rows, d), lambda l, j: (0, 0)),
                  pl.BlockSpec((None, d, tn), lambda l, j: (l, 0, j)),
                  pl.BlockSpec((None, 1, tn), lambda l, j: (l, 0, j))],
        out_specs=pl.BlockSpec((None, rows, 1, tn), lambda l, j: (l, 0, 0, j)),
        out_shape=jax.ShapeDtypeStruct((depth, rows, 1, n), F32),
        compiler_params=_params("parallel", "parallel"),
        name="ada",
    )(c_all, w_ada, b_ada.reshape(depth, 1, n))


def _norm_rows(x3):
    tokens = x3.shape[0] * SUBLANES
    return min(1024, tokens // SUBLANES)


def _prenorm_kernel(x_ref, g_ref, sc_ref, sh_ref, o_ref):
    h = _norm_mod(x_ref[...], g_ref[...], sc_ref[...], sh_ref[...])
    o_ref[...] = h.reshape(o_ref.shape).astype(o_ref.dtype)


def _prenorm(tr, x3, g, mod, layer, sublayer, dtype):
    tm = _norm_rows(x3)
    trp = tr._replace(tm=tm)
    nb = tm // SUBLANES
    ti, tj = (lambda i: i), (lambda i: 0)
    return pl.pallas_call(
        _prenorm_kernel,
        grid=(x3.shape[0] // nb,),
        in_specs=[pl.BlockSpec((nb, SUBLANES, D_MODEL), lambda i: (i, 0, 0)),
                  pl.BlockSpec((None, 1, D_MODEL), lambda i: (layer, 0, 0)),
                  _mod_spec(trp, layer, 3 * sublayer + 1, D_MODEL, ti, tj),
                  _mod_spec(trp, layer, 3 * sublayer, D_MODEL, ti, tj)],
        out_specs=pl.BlockSpec((tm, D_MODEL), lambda i: (i, 0)),
        out_shape=jax.ShapeDtypeStruct((x3.shape[0] * SUBLANES, D_MODEL), dtype),
        compiler_params=_params("parallel"),
        name="prenorm",
    )(x3, g, mod, mod)


def _s5_prep_kernel(lre_ref, lim_ref, ldt_ref, bre_ref, bim_ref, cre_ref, cim_ref,
                    are_ref, aim_ref, bbre_ref, bbim_ref, ccre_ref, ccim_ref):
    lr, li = lre_ref[...], lim_ref[...]
    dt = jnp.exp(ldt_ref[...])
    mag = jnp.exp(lr * dt)
    are = mag * jnp.cos(li * dt)
    aim = mag * jnp.sin(li * dt)
    er, ei = are - 1.0, aim
    den = lr * lr + li * li
    qre = (er * lr + ei * li) / den
    qim = (ei * lr - er * li) / den
    are_ref[...] = are
    aim_ref[...] = aim

    br, bi = bre_ref[...], bim_ref[...]
    keep = (jnp.right_shift(lax.broadcasted_iota(jnp.int32, br.shape, 0), LOG2_GROUP)
            == jnp.right_shift(lax.broadcasted_iota(jnp.int32, br.shape, 1), LOG2_STATE))
    bbre_ref[...] = jnp.where(keep, qre * br - qim * bi, 0.0).astype(BF16)
    bbim_ref[...] = jnp.where(keep, qre * bi + qim * br, 0.0).astype(BF16)

    cr, ci = cre_ref[...], cim_ref[...]
    keep = (jnp.right_shift(lax.broadcasted_iota(jnp.int32, cr.shape, 0), LOG2_STATE)
            == jnp.right_shift(lax.broadcasted_iota(jnp.int32, cr.shape, 1), LOG2_GROUP))
    ccre_ref[...] = jnp.where(keep, cr, 0.0).astype(BF16)
    ccim_ref[...] = jnp.where(keep, -ci, 0.0).astype(BF16)


def _s5_prep(a_re, a_im, log_dt, b_re, b_im, c_re, c_im):
    nt = N_LANE_TILES
    tile3 = lambda a: a.reshape(nt, 1, STATE_TILE)
    ldt = jnp.broadcast_to(log_dt[:, None], (SSM_GROUPS, SSM_STATE))
    b_rows = lambda b: jnp.tile(b.transpose(0, 2, 1).reshape(D_MODEL, SSM_STATE), (1, GROUPS_PER_TILE))
    c_rows = lambda c: jnp.tile(c.transpose(0, 2, 1).reshape(SSM_GROUPS * SSM_STATE, SSM_GROUP),
                                (1, GROUPS_PER_TILE))
    vec = pl.BlockSpec((None, 1, STATE_TILE), lambda k: (k, 0, 0))
    bspec = pl.BlockSpec((LANES, STATE_TILE), lambda k: (k, 0))
    cspec = pl.BlockSpec((STATE_TILE, LANES), lambda k: (k, 0))
    return pl.pallas_call(
        _s5_prep_kernel,
        grid=(nt,),
        in_specs=[vec, vec, vec, bspec, bspec, cspec, cspec],
        out_specs=[vec, vec, bspec, bspec, cspec, cspec],
        out_shape=[jax.ShapeDtypeStruct((nt, 1, STATE_TILE), F32)] * 2
        + [jax.ShapeDtypeStruct((D_MODEL, STATE_TILE), BF16)] * 2
        + [jax.ShapeDtypeStruct((SSM_GROUPS * SSM_STATE, LANES), BF16)] * 2,
        compiler_params=_params("parallel"),
        name="s5_prep",
    )(tile3(a_re), tile3(a_im), tile3(ldt), b_rows(b_re), b_rows(b_im), c_rows(c_re), c_rows(c_im))


S5_CHUNK = 256


def _s5_step_kernel(u_ref, d_ref, bbre_ref, bbim_ref, ccre_ref, ccim_ref, are_ref, aim_ref, x0_ref, x1_ref,
                    g_ref, sre_ref, sim_ref, up, gp, gn, *chunks, n_blocks, seg_len):
    n_chunks = len(chunks) // 2
    hre, him = chunks[:n_chunks], chunks[n_chunks:]

    def natural(lane):
        return pl.ds(pl.multiple_of(lane * seg_len, SUBLANES), seg_len)

    def regrouped(lane):
        return pl.ds((lane // SUBLANES) * (seg_len * SUBLANES) + lane % SUBLANES, seg_len, stride=SUBLANES)

    def for_each_lane(body):
        lax.fori_loop(0, n_blocks * SUBLANES, body, 0, unroll=8)

    def regroup(lane, c):
        up[regrouped(lane), :] = u_ref[natural(lane), :]
        return c

    for_each_lane(regroup)

    ar = jnp.broadcast_to(are_ref[...], (SUBLANES, STATE_TILE))
    ai = jnp.broadcast_to(aim_ref[...], (SUBLANES, STATE_TILE))
    d = d_ref[...]
    groups_per_chunk = S5_CHUNK // SUBLANES

    def chunk_rows(q):
        return slice(q * S5_CHUNK, (q + 1) * S5_CHUNK)

    def local_rows(i):
        j = i % groups_per_chunk
        return slice(j * SUBLANES, (j + 1) * SUBLANES)

    def input_chunk(q):
        ub = up[chunk_rows(q), :].astype(BF16)
        hre[q][...] = _dot(ub, bbre_ref[...])
        him[q][...] = _dot(ub, bbim_ref[...])

    def scan_chunk(q, carry):
        for i in range(q * groups_per_chunk, (q + 1) * groups_per_chunk):
            nb, r = i // seg_len, local_rows(i)
            blk = slice(nb * SUBLANES, (nb + 1) * SUBLANES)
            if i % seg_len == 0:
                carry = (x0_ref[blk, :], x1_ref[blk, :])
            hr, hi = carry
            carry = (ar * hr - ai * hi + hre[q][r, :], ar * hi + ai * hr + him[q][r, :])
            hre[q][r, :], him[q][r, :] = carry
            if i % seg_len == seg_len - 1:
                sre_ref[blk, :], sim_ref[blk, :] = carry
        return carry

    def output_chunk(q):
        y = _dot(hre[q][...].astype(BF16), ccre_ref[...]) + _dot(him[q][...].astype(BF16), ccim_ref[...])
        gp[chunk_rows(q), :] = jax.nn.gelu(y + d * up[chunk_rows(q), :])

    input_chunk(0)
    carry = None
    for q in range(n_chunks):
        if q + 1 < n_chunks:
            input_chunk(q + 1)
        carry = scan_chunk(q, carry)
        if q >= 1:
            output_chunk(q - 1)
    output_chunk(n_chunks - 1)

    def ungroup(lane, c):
        gn[natural(lane), :] = gp[regrouped(lane), :]
        return c

    for_each_lane(ungroup)
    g_ref[...] = gn[...].astype(BF16)


def _s5_step(tr, u2d, d_skip, tabs, h0):
    are, aim, bbre, bbim, ccre, ccim = tabs
    rows = tr.n_seq * tr.seq_len
    assert rows % S5_CHUNK == 0 and tr.n_seq % SUBLANES == 0 and tr.seq_len % SUBLANES == 0
    vec = pl.BlockSpec((None, 1, STATE_TILE), lambda k: (k, 0, 0))
    bspec = pl.BlockSpec((LANES, STATE_TILE), lambda k: (k, 0))
    cspec = pl.BlockSpec((STATE_TILE, LANES), lambda k: (k, 0))
    tok = pl.BlockSpec((rows, LANES), lambda k: (0, k))
    st_spec = pl.BlockSpec((tr.n_seq, STATE_TILE), lambda k: (0, k))
    st_shape = jax.ShapeDtypeStruct((tr.n_seq, SSM_GROUPS * SSM_STATE), F32)
    return pl.pallas_call(
        functools.partial(_s5_step_kernel, n_blocks=tr.n_seq // SUBLANES, seg_len=tr.seq_len),
        grid=(N_LANE_TILES,),
        in_specs=[tok, pl.BlockSpec((1, LANES), lambda k: (0, k)), bspec, bspec, cspec, cspec, vec, vec,
                  st_spec, st_spec],
        out_specs=[tok, st_spec, st_spec],
        out_shape=[jax.ShapeDtypeStruct(u2d.shape, BF16), st_shape, st_shape],
        scratch_shapes=[pltpu.VMEM((rows, LANES), F32)] * 3
        + [pltpu.VMEM((S5_CHUNK, STATE_TILE), F32)] * (2 * (rows // S5_CHUNK)),
        compiler_params=_params("parallel"),
        name="s5_step",
    )(u2d, d_skip, bbre, bbim, ccre, ccim, are, aim, *h0)


S5_STEPS = 512


def _s5_long_kernel(u_ref, d_ref, bbre_ref, bbim_ref, ccre_ref, ccim_ref, are_ref, aim_ref,
                    g_ref, sre_ref, sim_ref, up, ys, cre, cim, *chunks, n_seq, n_tiles):
    n_chunks = len(chunks) // 2
    hre, him = chunks[:n_chunks], chunks[n_chunks:]
    rows = S5_STEPS * SUBLANES
    c = pl.program_id(1)

    @pl.when(c == 0)
    def _():
        cre[...] = jnp.zeros(cre.shape, F32)
        cim[...] = jnp.zeros(cim.shape, F32)

    def lane_rows(b, kk):
        return pl.ds(b * n_tiles + kk, S5_STEPS, stride=SUBLANES)

    def tile_lanes(kk):
        return slice(kk * LANES, (kk + 1) * LANES)

    for kk in range(n_tiles):
        up[kk] = jnp.zeros((rows, LANES), F32)
        for b in range(n_seq):
            up[kk, lane_rows(b, kk), :] = u_ref[b, :, tile_lanes(kk)]

    tile_of_row = jnp.bitwise_and(lax.broadcasted_iota(jnp.int32, (SUBLANES, STATE_TILE), 0), n_tiles - 1)
    ar = jnp.broadcast_to(are_ref[0], (SUBLANES, STATE_TILE))
    ai = jnp.broadcast_to(aim_ref[0], (SUBLANES, STATE_TILE))
    for kk in range(1, n_tiles):
        ar = jnp.where(tile_of_row == kk, are_ref[kk], ar)
        ai = jnp.where(tile_of_row == kk, aim_ref[kk], ai)
    state_rows = lambda kk: slice(kk * STATE_TILE, (kk + 1) * STATE_TILE)
    c_re = jnp.concatenate([ccre_ref[state_rows(kk), :] for kk in range(n_tiles)], axis=-1)
    c_im = jnp.concatenate([ccim_ref[state_rows(kk), :] for kk in range(n_tiles)], axis=-1)
    groups_per_chunk = S5_CHUNK // SUBLANES

    def chunk_rows(q):
        return slice(q * S5_CHUNK, (q + 1) * S5_CHUNK)

    def input_chunk(q):
        ub = jnp.concatenate([up[kk, chunk_rows(q), :] for kk in range(n_tiles)], axis=-1).astype(BF16)
        hre[q][...] = _dot(ub, bbre_ref[...])
        him[q][...] = _dot(ub, bbim_ref[...])

    def scan_chunk(q, carry):
        for j in range(groups_per_chunk):
            r = slice(j * SUBLANES, (j + 1) * SUBLANES)
            hr, hi = carry
            carry = (ar * hr - ai * hi + hre[q][r, :], ar * hi + ai * hr + him[q][r, :])
            hre[q][r, :], him[q][r, :] = carry
        return carry

    def output_chunk(q):
        y = _dot(hre[q][...].astype(BF16), c_re) + _dot(him[q][...].astype(BF16), c_im)
        for kk in range(n_tiles):
            ys[kk, chunk_rows(q), :] = y[:, tile_lanes(kk)]

    input_chunk(0)
    carry = (cre[...], cim[...])
    for q in range(n_chunks):
        if q + 1 < n_chunks:
            input_chunk(q + 1)
        carry = scan_chunk(q, carry)
        if q >= 1:
            output_chunk(q - 1)
    output_chunk(n_chunks - 1)
    cre[...], cim[...] = carry

    d = d_ref[...]
    for kk in range(n_tiles):
        for b in range(n_seq):
            u = u_ref[b, :, tile_lanes(kk)]
            y = ys[kk, lane_rows(b, kk), :]
            g_ref[b, :, tile_lanes(kk)] = jax.nn.gelu(y + d[:, tile_lanes(kk)] * u).astype(BF16)

    @pl.when(c == pl.num_programs(1) - 1)
    def _():
        for kk in range(n_tiles):
            for b in range(n_seq):
                lane = b * n_tiles + kk
                sre_ref[b, :, state_rows(kk)] = cre[lane:lane + 1, :]
                sim_ref[b, :, state_rows(kk)] = cim[lane:lane + 1, :]


def _s5_long(tr, u2d, d_skip, tabs):
    are, aim, bbre, bbim, ccre, ccim = tabs
    n_seq, seq_len = tr.n_seq, tr.seq_len
    assert SUBLANES % n_seq == 0 and seq_len % S5_STEPS == 0
    n_tiles = SUBLANES // n_seq
    assert N_LANE_TILES % n_tiles == 0 and n_tiles == 1 << (n_tiles.bit_length() - 1)
    cw = n_tiles * LANES
    rows = S5_STEPS * SUBLANES
    n_chunks = rows // S5_CHUNK
    n_state = SSM_GROUPS * SSM_STATE
    tok = pl.BlockSpec((n_seq, S5_STEPS, cw), lambda k, c: (0, c, k))
    vec = pl.BlockSpec((n_tiles, 1, STATE_TILE), lambda k, c: (k, 0, 0))
    bspec = pl.BlockSpec((cw, STATE_TILE), lambda k, c: (k, 0))
    cspec = pl.BlockSpec((n_tiles * STATE_TILE, LANES), lambda k, c: (k, 0))
    st_spec = pl.BlockSpec((n_seq, 1, n_tiles * STATE_TILE), lambda k, c: (0, 0, k))
    st_shape = jax.ShapeDtypeStruct((n_seq, 1, n_state), F32)
    g3, s_re, s_im = pl.pallas_call(
        functools.partial(_s5_long_kernel, n_seq=n_seq, n_tiles=n_tiles),
        grid=(N_LANE_TILES // n_tiles, seq_len // S5_STEPS),
        in_specs=[tok, pl.BlockSpec((1, cw), lambda k, c: (0, k)), bspec, bspec, cspec, cspec, vec, vec],
        out_specs=[tok, st_spec, st_spec],
        out_shape=[jax.ShapeDtypeStruct((n_seq, seq_len, D_MODEL), BF16), st_shape, st_shape],
        scratch_shapes=[pltpu.VMEM((n_tiles, rows, LANES), F32)] * 2
        + [pltpu.VMEM((SUBLANES, STATE_TILE), F32)] * 2
        + [pltpu.VMEM((S5_CHUNK, STATE_TILE), F32)] * (2 * n_chunks),
        compiler_params=_params("parallel", "arbitrary"),
        name="s5_long",
    )(u2d.reshape(n_seq, seq_len, D_MODEL), d_skip, bbre, bbim, ccre, ccim, are, aim)
    return g3.reshape(n_seq * seq_len, D_MODEL), s_re, s_im


GLU_COLS = 256


def _glu_kernel(*refs, residual):
    if residual:
        a_ref, wa_ref, wb_ref, ba_ref, bb_ref, x_ref, gate_ref, o_ref = refs
    else:
        a_ref, wa_ref, wb_ref, ba_ref, bb_ref, o_ref = refs
    a = a_ref[...]
    tn = wa_ref.shape[1]
    for c in range(0, tn, GLU_COLS):
        cs = slice(c, c + GLU_COLS)
        za = _dot(a, wa_ref[:, cs].astype(BF16)) + ba_ref[:, cs]
        zb = _dot(a, wb_ref[:, cs].astype(BF16)) + bb_ref[:, cs]
        out = za * jax.nn.sigmoid(zb)
        if residual:
            o_ref[:, :, cs] = x_ref[:, :, cs] + gate_ref[:, :, cs] * out.reshape(o_ref.shape[:2] + (GLU_COLS,))
        else:
            o_ref[:, cs] = out


def _glu(tr, a2d, w, b, x3=None, mod=None, layer=None):
    tm, tn = tr.tm, 512
    nb, nj = tm // SUBLANES, D_MODEL // tn
    b3 = b.reshape(1, 1, 2 * D_MODEL)
    residual = x3 is not None
    in_specs = [pl.BlockSpec((tm, D_MODEL), lambda j, i: (i, 0)),
                pl.BlockSpec((None, D_MODEL, tn), lambda j, i: (0, 0, j)),
                pl.BlockSpec((None, D_MODEL, tn), lambda j, i: (0, 0, j + nj)),
                pl.BlockSpec((None, 1, tn), lambda j, i: (0, 0, j)),
                pl.BlockSpec((None, 1, tn), lambda j, i: (0, 0, j + nj))]
    args = (a2d, w, w, b3, b3)
    if residual:
        out_spec = pl.BlockSpec((nb, SUBLANES, tn), lambda j, i: (i, 0, j))
        out_shape = jax.ShapeDtypeStruct(x3.shape, F32)
        in_specs += [out_spec, _mod_spec(tr, layer, 2, tn, lambda j, i: i, lambda j, i: j)]
        args += (x3, mod)
    else:
        out_spec = pl.BlockSpec((tm, tn), lambda j, i: (i, j))
        out_shape = jax.ShapeDtypeStruct(a2d.shape, F32)
    return pl.pallas_call(
        functools.partial(_glu_kernel, residual=residual),
        grid=(nj, a2d.shape[0] // tm),
        in_specs=in_specs,
        out_specs=out_spec,
        out_shape=out_shape,
        compiler_params=_params("parallel", "parallel"),
        name="glu",
    )(*args)


CONV_CHUNK = SUBLANES * SUBLANES
CONV_ROWS = 256


def _conv_weights(w_ref, b_ref, ls):
    w = [jnp.broadcast_to(w_ref[k:k + 1, ls], (SUBLANES, LANES)) for k in range(CONV_WIDTH)]
    return w, jnp.broadcast_to(b_ref[:, ls], (SUBLANES, LANES))


def _conv_taps(win, w, bias):
    acc = [bias] * SUBLANES
    for o in range(CONV_WIDTH + SUBLANES - 1):
        x = win(o)
        for r in range(SUBLANES):
            if 0 <= o - r < CONV_WIDTH:
                acc[r] = acc[r] + w[o - r] * x
    return acc


def _ln_silu_store(cbuf, lg_ref, lb_ref, o_ref):
    n = cbuf.shape[0]
    inv_d = 1.0 / (n * LANES)
    tot = cbuf[0]
    for l in range(1, n):
        tot = tot + cbuf[l]
    mean = jnp.sum(tot, axis=-1, keepdims=True) * inv_d
    sq = jnp.zeros_like(tot)
    for l in range(n):
        xc = cbuf[l] - mean
        sq = sq + xc * xc
    rstd = lax.rsqrt(jnp.sum(sq, axis=-1, keepdims=True) * inv_d + LN_EPS)
    for l in range(n):
        ls = slice(l * LANES, (l + 1) * LANES)
        y = (cbuf[l] - mean) * rstd * lg_ref[:, ls] + lb_ref[:, ls]
        o_ref[:, ls] = jax.nn.silu(y).astype(BF16)


def _conv_ln_kernel(v_ref, halo_ref, w_ref, b_ref, lg_ref, lb_ref, o_ref, pad, cbuf, *, tiles_per_seq):
    rows = v_ref.shape[0]
    first = (pl.program_id(0) % tiles_per_seq) == 0
    for l in range(N_LANE_TILES):
        ls = slice(l * LANES, (l + 1) * LANES)
        pad[l, 0:HALO, :] = jnp.where(first, 0.0, halo_ref[:, ls])
        pad[l, HALO:, :] = v_ref[:, ls]
        w, bias = _conv_weights(w_ref, b_ref, ls)
        for base in range(0, rows, CONV_CHUNK):
            acc = _conv_taps(
                lambda o: pad[l, pl.ds(base + HIST_OFF + o, SUBLANES, stride=SUBLANES), :], w, bias)
            for r in range(SUBLANES):
                cbuf[l, pl.ds(base + r, SUBLANES, stride=SUBLANES), :] = acc[r]
    _ln_silu_store(cbuf, lg_ref, lb_ref, o_ref)


def _conv_ln_long(tr, v2d, w, b, ln_g, ln_b):
    rows = CONV_ROWS
    assert tr.seq_len % rows == 0 and rows % CONV_CHUNK == 0 and rows % HALO == 0
    hb = rows // HALO
    row = pl.BlockSpec((1, D_MODEL), lambda i: (0, 0))
    return pl.pallas_call(
        functools.partial(_conv_ln_kernel, tiles_per_seq=tr.seq_len // rows),
        grid=(v2d.shape[0] // rows,),
        in_specs=[pl.BlockSpec((rows, D_MODEL), lambda i: (i, 0)),
                  pl.BlockSpec((HALO, D_MODEL), lambda i: (jnp.maximum(i * hb - 1, 0), 0)),
                  pl.BlockSpec((None, CONV_WIDTH, D_MODEL), lambda i: (0, 0, 0)),
                  row, row, row],
        out_specs=pl.BlockSpec((rows, D_MODEL), lambda i: (i, 0)),
        out_shape=jax.ShapeDtypeStruct(v2d.shape, BF16),
        scratch_shapes=[pltpu.VMEM((N_LANE_TILES, HALO + rows, LANES), F32),
                        pltpu.VMEM((N_LANE_TILES, rows, LANES), F32)],
        compiler_params=_params("parallel"),
        name="conv_ln_long",
    )(v2d, v2d, w, b, ln_g, ln_b)


def _cast_kernel(w_ref, o_ref):
    o_ref[...] = w_ref[...].astype(o_ref.dtype)


def _cast_bf16(w2d):
    rows = 512
    return pl.pallas_call(
        _cast_kernel,
        grid=(w2d.shape[0] // rows,),
        in_specs=[pl.BlockSpec((rows, w2d.shape[1]), lambda i: (i, 0))],
        out_specs=pl.BlockSpec((rows, w2d.shape[1]), lambda i: (i, 0)),
        out_shape=jax.ShapeDtypeStruct(w2d.shape, BF16),
        compiler_params=_params("parallel"),
        name="cast_bf16",
    )(w2d)


PW2_COLS = 256


def _ordered_after(x, dep, zero_bits):
    zero = jnp.bitwise_and(pltpu.bitcast(dep, jnp.uint32), zero_bits)
    zrow = pltpu.bitcast(zero, F32)[0:1, :]
    return x + jnp.concatenate([zrow] * (x.shape[1] // LANES), axis=-1).astype(x.dtype)


def _conv_pw2_kernel(v_ref, halo_ref, w_ref, b_ref, lg_ref, lb_ref, pw_ref, pb_ref, x_ref, gate_ref, zb_ref,
                     o_ref, pad, cbuf, hc, *, tiles_per_seq):
    rows = v_ref.shape[0]
    i = pl.program_id(0)
    slot = lax.rem(i, 2)

    @pl.when(i == 0)
    def _():
        hc[1] = jnp.zeros(hc.shape[1:], BF16)

    prev = hc[1 - slot]
    pieces = D_MODEL // PW2_COLS
    slabs_per_piece = N_LANE_TILES // pieces

    def pointwise(c, lhs):
        cs = slice(c * PW2_COLS, (c + 1) * PW2_COLS)
        out = _dot(lhs, pw_ref[:, cs]) + pb_ref[:, cs]
        o_ref[:, :, cs] = x_ref[:, :, cs] + gate_ref[:, :, cs] * out.reshape(o_ref.shape[:2] + (PW2_COLS,))

    first = (i % tiles_per_seq) == 0
    for l in range(N_LANE_TILES):
        ls = slice(l * LANES, (l + 1) * LANES)
        pad[l, 0:HALO, :] = jnp.where(first, 0.0, halo_ref[:, ls])
        pad[l, HALO:, :] = v_ref[:, ls]
        w, bias = _conv_weights(w_ref, b_ref, ls)
        for base in range(0, rows, CONV_CHUNK):
            acc = _conv_taps(
                lambda o: pad[l, pl.ds(base + HIST_OFF + o, SUBLANES, stride=SUBLANES), :], w, bias)
            for r in range(SUBLANES):
                cbuf[l, pl.ds(base + r, SUBLANES, stride=SUBLANES), :] = acc[r]
        if l % slabs_per_piece == 0:
            pointwise(l // slabs_per_piece, _ordered_after(prev, acc[0], zb_ref[...]))
    _ln_silu_store(cbuf, lg_ref, lb_ref, hc.at[slot])


def _conv_pw2_long(tr, v2d, w, b, ln_g, ln_b, pw_bf16, pb, x3, mod, layer):
    rows = CONV_ROWS
    assert tr.seq_len % rows == 0 and rows % CONV_CHUNK == 0 and rows % HALO == 0
    hb, nb = rows // HALO, rows // SUBLANES
    n_tiles = v2d.shape[0] // rows
    prev = lambda i: jnp.maximum(i - 1, 0)
    row = pl.BlockSpec((1, D_MODEL), lambda i: (0, 0))
    xspec = pl.BlockSpec((nb, SUBLANES, D_MODEL), lambda i: (prev(i), 0, 0))
    return pl.pallas_call(
        functools.partial(_conv_pw2_kernel, tiles_per_seq=tr.seq_len // rows),
        grid=(n_tiles + 1,),
        in_specs=[pl.BlockSpec((rows, D_MODEL), lambda i: (jnp.minimum(i, n_tiles - 1), 0)),
                  pl.BlockSpec((HALO, D_MODEL), lambda i: (jnp.maximum(i * hb - 1, 0), 0)),
                  pl.BlockSpec((None, CONV_WIDTH, D_MODEL), lambda i: (0, 0, 0)),
                  row, row, row,
                  pl.BlockSpec((D_MODEL, D_MODEL), lambda i: (0, 0), pipeline_mode=pl.Buffered(1)),
                  row, xspec,
                  _mod_spec(tr._replace(tm=rows), layer, 2, D_MODEL, prev, lambda i: 0),
                  pl.BlockSpec((SUBLANES, LANES), lambda i: (0, 0))],
        out_specs=xspec,
        out_shape=jax.ShapeDtypeStruct(x3.shape, F32),
        scratch_shapes=[pltpu.VMEM((N_LANE_TILES, HALO + rows, LANES), F32),
                        pltpu.VMEM((N_LANE_TILES, rows, LANES), F32),
                        pltpu.VMEM((2, rows, D_MODEL), BF16)],
        compiler_params=_params("arbitrary"),
        name="conv_pw2_long",
    )(v2d, v2d, w, b, ln_g, ln_b, pw_bf16, pb, x3, mod, jnp.zeros((SUBLANES, LANES), jnp.uint32))


def _conv_ln_step_kernel(v_ref, cache_ref, w_ref, b_ref, lg_ref, lb_ref, o_ref, nc_ref, vs, cbuf):
    for l in range(N_LANE_TILES):
        ls = slice(l * LANES, (l + 1) * LANES)
        vs[l] = v_ref[:, ls]
        new = [vs[l, pl.ds(t, SUBLANES, stride=SUBLANES), :] for t in range(SUBLANES)]

        def padded(o, ls=ls, new=new):
            return cache_ref[o, :, ls] if o < CONV_HIST else new[o - CONV_HIST]

        acc = _conv_taps(padded, *_conv_weights(w_ref, b_ref, ls))
        for t in range(SUBLANES):
            cbuf[l, pl.ds(t, SUBLANES, stride=SUBLANES), :] = acc[t]
        for q in range(CONV_HIST):
            nc_ref[q, :, ls] = padded(q + SUBLANES)
    _ln_silu_store(cbuf, lg_ref, lb_ref, o_ref)


def _conv_ln_step(tr, v2d, cache_t, w, b, ln_g, ln_b):
    assert tr.seq_len == SUBLANES
    rows = SUBLANES * tr.seq_len
    row = pl.BlockSpec((1, D_MODEL), lambda s: (0, 0))
    cspec = pl.BlockSpec((CONV_HIST, SUBLANES, D_MODEL), lambda s: (0, s, 0))
    return pl.pallas_call(
        _conv_ln_step_kernel,
        grid=(tr.n_seq // SUBLANES,),
        in_specs=[pl.BlockSpec((rows, D_MODEL), lambda s: (s, 0)), cspec,
                  pl.BlockSpec((None, CONV_WIDTH, D_MODEL), lambda s: (0, 0, 0)),
                  row, row, row],
        out_specs=[pl.BlockSpec((rows, D_MODEL), lambda s: (s, 0)), cspec],
        out_shape=[jax.ShapeDtypeStruct(v2d.shape, BF16), jax.ShapeDtypeStruct(cache_t.shape, F32)],
        scratch_shapes=[pltpu.VMEM((N_LANE_TILES, rows, LANES), F32),
                        pltpu.VMEM((N_LANE_TILES, rows, LANES), F32)],
        compiler_params=_params("parallel"),
        name="conv_ln_step",
    )(v2d, cache_t, w, b, ln_g, ln_b)


MLP_TILE = 1024
MLP_TK = 2048


def _mlp_up_kernel(h_ref, w_ref, o_ref):
    a = jnp.maximum(_dot(h_ref[...], w_ref[...].astype(BF16)), 0.0)
    o_ref[...] = (a * a).astype(BF16)


def _mlp_up(tr, h2d, w1, layer):
    tm, tn = tr.tm, MLP_TILE
    return pl.pallas_call(
        _mlp_up_kernel,
        grid=(D_FF // tn, h2d.shape[0] // tm),
        in_specs=[pl.BlockSpec((tm, D_MODEL), lambda j, i: (i, 0)),
                  pl.BlockSpec((None, D_MODEL, tn), lambda j, i: (layer, 0, j))],
        out_specs=pl.BlockSpec((tm, tn), lambda j, i: (i, j)),
        out_shape=jax.ShapeDtypeStruct((h2d.shape[0], D_FF), BF16),
        compiler_params=_params("parallel", "parallel"),
        name="mlp_up",
    )(h2d, w1)


def _mm_res_kernel(*refs, has_bias, single_k):
    if has_bias:
        a_ref, w_ref, b_ref, x_ref, gate_ref, o_ref = refs
    else:
        a_ref, w_ref, x_ref, gate_ref, o_ref = refs
    def finish(out):
        if has_bias:
            out = out + b_ref[...]
        o_ref[...] = x_ref[...] + gate_ref[...] * out

    def product():
        return _dot(a_ref[...], w_ref[...].astype(BF16)).reshape(o_ref.shape)

    if single_k:
        finish(product())
        return
    k = pl.program_id(2)

    @pl.when(k == 0)
    def _():
        o_ref[...] = jnp.zeros(o_ref.shape, F32)

    o_ref[...] += product()

    @pl.when(k == pl.num_programs(2) - 1)
    def _():
        finish(o_ref[...])


def _mm_res(tr, a2d, w, w_idx, x3, mod, layer, part, bias=None):
    tm, tn = tr.tm, MLP_TILE
    kdim = a2d.shape[1]
    tk = min(kdim, MLP_TK)
    nb = tm // SUBLANES
    xspec = pl.BlockSpec((nb, SUBLANES, tn), lambda j, i, k: (i, 0, j))
    has_bias = bias is not None
    bias_specs = [pl.BlockSpec((1, tn), lambda j, i, k: (0, j))] if has_bias else []
    bias_args = (bias,) if has_bias else ()
    return pl.pallas_call(
        functools.partial(_mm_res_kernel, has_bias=has_bias, single_k=kdim == tk),
        grid=(D_MODEL // tn, a2d.shape[0] // tm, kdim // tk),
        in_specs=[pl.BlockSpec((tm, tk), lambda j, i, k: (i, k)),
                  pl.BlockSpec((None, tk, tn), lambda j, i, k: (w_idx, k, j))] + bias_specs
        + [xspec, _mod_spec(tr, layer, part, tn, lambda j, i, k: i, lambda j, i, k: j)],
        out_specs=xspec,
        out_shape=jax.ShapeDtypeStruct(x3.shape, F32),
        compiler_params=_params("parallel", "parallel", "arbitrary"),
        name="mm_res",
    )(a2d, w, *bias_args, x3, mod)


def _final_norm_kernel(x_ref, g_ref, o_ref):
    x = x_ref[...]
    ms = jnp.mean(x * x, axis=-1, keepdims=True)
    o_ref[...] = x * lax.rsqrt(ms + RMS_EPS) * g_ref[...]


def _final_norm(tr, x3, g):
    nb = _norm_rows(x3) // SUBLANES
    spec = pl.BlockSpec((nb, SUBLANES, D_MODEL), lambda i: (i, 0, 0))
    return pl.pallas_call(
        _final_norm_kernel,
        grid=(x3.shape[0] // nb,),
        in_specs=[spec, pl.BlockSpec((1, D_MODEL), lambda i: (0, 0))],
        out_specs=spec,
        out_shape=jax.ShapeDtypeStruct(x3.shape, F32),
        compiler_params=_params("parallel"),
        name="final_norm",
    )(x3, g)


def _mlp(tr, x3, g, w1, w2, mod, layer):
    h2d = _prenorm(tr, x3, g, mod, layer, 1, BF16)
    return _mm_res(tr, _mlp_up(tr, h2d, w1, layer), w2, layer, x3, mod, layer, 5)


def _trunk(tr, x, mod, h0, cache, tabs, p):
    tokens = tr.n_seq * tr.seq_len
    x3 = x.reshape(tokens // SUBLANES, SUBLANES, D_MODEL)

    u2d = _prenorm(tr, x3, p["rms_g_mix"], mod, 0, 0, F32)
    if h0 is None:
        g2d, s_re, s_im = _s5_long(tr, u2d, p["ssm_d"], tabs)
    else:
        g2d, s_re, s_im = _s5_step(tr, u2d, p["ssm_d"], tabs, h0)
    x3 = _glu(tr, g2d, p["ssm_w_glu"], p["ssm_b_glu"], x3, mod, 0)
    x3 = _mlp(tr, x3, p["rms_g_mlp"], p["mlp_w1"], p["mlp_w2"], mod, 0)

    h2d = _prenorm(tr, x3, p["rms_g_mix"], mod, 1, 0, BF16)
    v2d = _glu(tr, h2d, p["conv_w_pw1"], p["conv_b_pw1"])
    conv_args = (p["conv_w_dw"], p["conv_b_dw"], p["conv_ln_g"], p["conv_ln_b"])
    if cache is None:
        x3 = _conv_pw2_long(tr, v2d, *conv_args, _cast_bf16(p["conv_w_pw2"][0]), p["conv_b_pw2"], x3, mod, 1)
        new_cache = v2d.reshape(1, tr.n_seq, tr.seq_len, D_MODEL)[:, :, tr.seq_len - CONV_HIST:]
    else:
        hc2d, cache_t = _conv_ln_step(tr, v2d, jnp.transpose(cache[0], (1, 0, 2)), *conv_args)
        new_cache = jnp.transpose(cache_t, (1, 0, 2))[None]
        x3 = _mm_res(tr, hc2d, p["conv_w_pw2"], 0, x3, mod, 1, 2, bias=p["conv_b_pw2"])
    x3 = _mlp(tr, x3, p["rms_g_mlp"], p["mlp_w1"], p["mlp_w2"], mod, 1)
    y3 = _final_norm(tr, x3, p["final_g"])

    state_shape = (1, tr.n_seq, SSM_GROUPS, SSM_STATE)
    return (y3.reshape(tr.n_seq, tr.seq_len, D_MODEL), s_re.reshape(state_shape), s_im.reshape(state_shape),
            new_cache)


def kernel(x_prompt, x_sample, state_ssm_re, state_ssm_im, cache_conv, c_prompt, c_sample, rms_g_mix, rms_g_mlp, w_ada, b_ada, ssm_a_re, ssm_a_im, ssm_log_dt, ssm_b_re, ssm_b_im, ssm_c_re, ssm_c_im, ssm_d, ssm_w_glu, ssm_b_glu, conv_w_pw1, conv_b_pw1, conv_w_dw, conv_b_dw, conv_ln_g, conv_ln_b, conv_w_pw2, conv_b_pw2, mlp_w1, mlp_w2, final_g):
    bp, lp, _ = x_prompt.shape
    bs, ls, _ = x_sample.shape
    assert w_ada.shape[0] == 2 and ssm_a_re.shape[0] == 1 and conv_w_dw.shape[0] == 1
    prompt = Trunk(bp, lp, 1024, mod_row=bs)
    sample = Trunk(bs, ls, bs * ls, mod_row=0)

    depth = w_ada.shape[0]
    p = dict(rms_g_mix=rms_g_mix.reshape(depth, 1, D_MODEL), rms_g_mlp=rms_g_mlp.reshape(depth, 1, D_MODEL),
             ssm_d=ssm_d, ssm_w_glu=ssm_w_glu, ssm_b_glu=ssm_b_glu,
             conv_w_pw1=conv_w_pw1, conv_b_pw1=conv_b_pw1, conv_w_dw=conv_w_dw, conv_b_dw=conv_b_dw,
             conv_ln_g=conv_ln_g, conv_ln_b=conv_ln_b, conv_w_pw2=conv_w_pw2, conv_b_pw2=conv_b_pw2,
             mlp_w1=mlp_w1, mlp_w2=mlp_w2, final_g=final_g.reshape(1, D_MODEL))

    n_c = bp + bs
    pad_rows = -n_c % SUBLANES
    c_all = jnp.concatenate([c_sample, c_prompt, jnp.zeros((pad_rows, D_MODEL), F32)], axis=0)
    mod = _ada(c_all, w_ada, b_ada)

    tabs = _s5_prep(ssm_a_re[0], ssm_a_im[0], ssm_log_dt[0], ssm_b_re[0], ssm_b_im[0],
                    ssm_c_re[0], ssm_c_im[0])

    n_state = SSM_GROUPS * SSM_STATE
    h0 = (state_ssm_re.reshape(bs, n_state), state_ssm_im.reshape(bs, n_state))
    y_p, p_re, p_im, p_buf = _trunk(prompt, x_prompt, mod, None, None, tabs, p)
    y_s, s_re, s_im, s_buf = _trunk(sample, x_sample, mod, h0, cache_conv, tabs, p)
    return (y_p, y_s, p_re, p_im, p_buf, s_re, s_im, s_buf)
```
